```python
import math
import jax, jax.numpy as jnp
from jax import lax
import numpy as np

D_MODEL = 2048
BATCH = 4
SEQ = 2048
DEPTH = 4

N_MIXERS = 3
PLE_DIM = 256
D_FF = 5632
A_HEADS = 8
A_HEAD_DIM = 128
B_HEADS = 16
B_HEAD_DIM = 128
MOBA_BLOCK = 256
MOBA_TOPK = 3
MOBA_Q_BLOCK = 16
C_HEADS = 16
C_KV_HEADS = 4
C_HEAD_DIM = 128
IDX_HEADS = 16
IDX_DIM = 64
DSA_TOPK_MAX = 256
C_Q = C_HEADS * C_HEAD_DIM
C_KV = C_KV_HEADS * C_HEAD_DIM
C_IQ = IDX_HEADS * IDX_DIM
C_IN_COLS = C_Q + 2 * C_KV + C_IQ + IDX_DIM + IDX_HEADS

Q_BLOCK = 128
LN_EPS = 1e-5
RMS_EPS = 1e-6
ALPHA = (2.0 * DEPTH) ** 0.25
BETA = (8.0 * DEPTH) ** -0.25
N_A = (DEPTH + 2) // 3
N_B = (DEPTH + 1) // 3
N_C = DEPTH // 3

kernel_name = 'hybrid_diff_moba_dsa_macaron_deepnorm'


def alibi_slopes(n):
    return jnp.asarray(2.0 ** (-8.0 * np.arange(1, n + 1, dtype=np.float32) / n), dtype=jnp.float32)


def layer_norm(x, g, b):
    xf = x.astype(jnp.float32)
    mu = jnp.mean(xf, -1, keepdims=True)
    var = jnp.mean(jnp.square(xf - mu), -1, keepdims=True)
    y = (xf - mu) * lax.rsqrt(var + LN_EPS) * g.astype(jnp.float32) + b.astype(jnp.float32)
    return y.astype(x.dtype)


def swiglu_ffn(x, w_in, w_out):
    gate, up = jnp.split(x @ w_in, 2, axis=-1)
    return (jax.nn.silu(gate) * up) @ w_out


def diff_attention(x, w_in, w_out, lam_q1, lam_k1, lam_q2, lam_k2, subln_g, lambda_init):
    B, S, _ = x.shape
    H, d = A_HEADS, A_HEAD_DIM
    f32 = jnp.float32
    q, k, v = jnp.split(x @ w_in, 3, axis=-1)
    q = q.reshape(B, S, H, 2, d)
    k = k.reshape(B, S, H, 2, d)
    v = v.reshape(B, S, H, 2 * d)
    lam = (jnp.exp(jnp.sum(lam_q1.astype(f32) * lam_k1.astype(f32)))
           - jnp.exp(jnp.sum(lam_q2.astype(f32) * lam_k2.astype(f32))) + lambda_init)
    slopes = alibi_slopes(H)[:, None, None, None]
    key_pos = jnp.arange(S)
    nqb = S // Q_BLOCK
    q_blocks = q.reshape(B, nqb, Q_BLOCK, H, 2, d).transpose(1, 0, 2, 3, 4, 5)
    starts = jnp.arange(nqb) * Q_BLOCK
    scale = d ** -0.5

    def one_block(args):
        qb, start = args
        t = start + jnp.arange(Q_BLOCK)
        dist = (t[:, None] - key_pos[None, :]).astype(f32)
        s = jnp.einsum('bqhmd,bkhmd->bhmqk', qb, k).astype(f32) * scale
        s = jnp.where(dist >= 0, s - slopes * dist, -jnp.inf)
        pr = jax.nn.softmax(s, axis=-1)
        a = pr[:, :, 0] - lam * pr[:, :, 1]
        return jnp.einsum('bhqk,bkhe->bqhe', a.astype(v.dtype), v)

    o = lax.map(one_block, (q_blocks, starts))
    o = o.transpose(1, 0, 2, 3, 4).reshape(B, S, H, 2 * d).astype(f32)
    o = o * lax.rsqrt(jnp.mean(o * o, -1, keepdims=True) + RMS_EPS) * subln_g.astype(f32) * (1.0 - lambda_init)
    return o.reshape(B, S, H * 2 * d).astype(x.dtype) @ w_out


def moba_attention(x, w_in, w_out):
    B, S, _ = x.shape
    H, d, BS, QC = B_HEADS, B_HEAD_DIM, MOBA_BLOCK, MOBA_Q_BLOCK
    f32 = jnp.float32
    q, k, v = jnp.split(x @ w_in, 3, axis=-1)
    q = q.reshape(B, S, H, d)
    k = k.reshape(B, S, H, d)
    v = v.reshape(B, S, H, d)
    nb = -(-S // BS)
    pad = nb * BS - S
    kp = jnp.pad(k, ((0, 0), (0, pad), (0, 0), (0, 0)))
    vp = jnp.pad(v, ((0, 0), (0, pad), (0, 0), (0, 0)))
    k_blocks = kp.reshape(B, nb, BS, H, d).transpose(0, 3, 1, 2, 4)
    v_blocks = vp.reshape(B, nb, BS, H, d).transpose(0, 3, 1, 2, 4)
    k_mean = jnp.mean(k_blocks.astype(f32), axis=3)
    n_sel = min(MOBA_TOPK, nb - 1)
    slopes = alibi_slopes(H)
    in_block = jnp.arange(BS)
    scale = d ** -0.5
    gather_blocks = jax.vmap(jax.vmap(lambda blocks, ids: blocks[ids]))
    nqc = S // QC
    q_chunks = q.reshape(B, nqc, QC, H, d).transpose(1, 0, 2, 3, 4)
    starts = jnp.arange(nqc) * QC

    def one_chunk(args):
        qc, start = args
        t = start + jnp.arange(QC)
        own = start // BS
        k_own = lax.dynamic_slice_in_dim(kp, own * BS, BS, axis=1)
        v_own = lax.dynamic_slice_in_dim(vp, own * BS, BS, axis=1)
        dist_own = (t[:, None] - (own * BS + in_block)[None, :]).astype(f32)
        s_own = jnp.einsum('bqhd,bkhd->bhqk', qc, k_own).astype(f32) * scale
        s_own = jnp.where(dist_own >= 0, s_own - slopes[:, None, None] * dist_own, -jnp.inf)
        if n_sel == 0:
            p_own = jax.nn.softmax(s_own, axis=-1).astype(v.dtype)
            return jnp.einsum('bhqk,bkhd->bqhd', p_own, v_own)
        gate = jnp.einsum('bqhd,bhnd->bhqn', qc.astype(f32), k_mean)
        gate = jnp.where(jnp.arange(nb) < own, gate, -jnp.inf)
        _, sel = lax.top_k(gate, n_sel)
        valid = sel < own
        k_sel = gather_blocks(k_blocks, sel)
        v_sel = gather_blocks(v_blocks, sel)
        dist_sel = (t[None, None, :, None, None] - (sel[..., None] * BS + in_block)).astype(f32)
        s_sel = jnp.einsum('bqhd,bhqnkd->bhqnk', qc, k_sel).astype(f32) * scale
        s_sel = jnp.where(valid[..., None], s_sel - slopes[None, :, None, None, None] * dist_sel, -jnp.inf)
        n_g = n_sel * BS
        pr = jax.nn.softmax(jnp.concatenate([s_sel.reshape(B, H, QC, n_g), s_own], axis=-1), axis=-1)
        p_sel = pr[..., :n_g].reshape(B, H, QC, n_sel, BS).astype(v.dtype)
        p_own = pr[..., n_g:].astype(v.dtype)
        return (jnp.einsum('bhqnk,bhqnkd->bqhd', p_sel, v_sel)
                + jnp.einsum('bhqk,bkhd->bqhd', p_own, v_own))

    o = lax.map(one_chunk, (q_chunks, starts))
    o = o.transpose(1, 0, 2, 3, 4).reshape(B, S, H * d)
    return o @ w_out


def dsa_attention(x, w_in, w_out):
    B, S, _ = x.shape
    H, G, d = C_HEADS, C_KV_HEADS, C_HEAD_DIM
    R = H // G
    f32 = jnp.float32
    cuts = [C_Q, C_Q + C_KV, C_Q + 2 * C_KV, C_Q + 2 * C_KV + C_IQ, C_Q + 2 * C_KV + C_IQ + IDX_DIM]
    q, k, v, q_idx, k_idx, w_idx = jnp.split(x @ w_in, cuts, axis=-1)
    q = q.reshape(B, S, G, R, d)
    k = k.reshape(B, S, G, d)
    v = v.reshape(B, S, G, d)
    q_idx = q_idx.reshape(B, S, IDX_HEADS, IDX_DIM)
    k_idx = k_idx.astype(f32)
    w_idx = w_idx.astype(f32) * IDX_HEADS ** -0.5
    n_keep = min(DSA_TOPK_MAX, S // 4)
    slopes = alibi_slopes(H).reshape(G, R)
    key_pos = jnp.arange(S)
    scale = d ** -0.5
    gather_tokens = jax.vmap(lambda tokens, ids: tokens[ids])
    nqb = S // Q_BLOCK
    q_b = q.reshape(B, nqb, Q_BLOCK, G, R, d).transpose(1, 0, 2, 3, 4, 5)
    qi_b = q_idx.reshape(B, nqb, Q_BLOCK, IDX_HEADS, IDX_DIM).transpose(1, 0, 2, 3, 4)
    wi_b = w_idx.reshape(B, nqb, Q_BLOCK, IDX_HEADS).transpose(1, 0, 2, 3)
    starts = jnp.arange(nqb) * Q_BLOCK

    def one_block(args):
        qb, qib, wib, start = args
        t = start + jnp.arange(Q_BLOCK)
        dots = jnp.einsum('bqhe,bse->bqhs', qib.astype(f32), k_idx) * IDX_DIM ** -0.5
        score = jnp.einsum('bqh,bqhs->bqs', wib, jax.nn.relu(dots))
        score = jnp.where(key_pos[None, :] <= t[:, None], score, -jnp.inf)
        _, idx = lax.top_k(score, n_keep)
        valid = idx <= t[None, :, None]
        k_sel = gather_tokens(k, idx)
        v_sel = gather_tokens(v, idx)
        dist = (t[None, :, None] - idx).astype(f32)
        s = jnp.einsum('bqgrd,bqkgd->bgrqk', qb, k_sel).astype(f32) * scale
        s = s - slopes[None, :, :, None, None] * dist[:, None, None]
        s = jnp.where(valid[:, None, None], s, -jnp.inf)
        pr = jax.nn.softmax(s, axis=-1).astype(v.dtype)
        return jnp.einsum('bgrqk,bqkgd->bqgrd', pr, v_sel)

    o = lax.map(one_block, (q_b, qi_b, wi_b, starts))
    o = o.transpose(1, 0, 2, 3, 4, 5).reshape(B, S, H * d)
    return o @ w_out


def per_layer_embedding(x, p_i, w_gate, w_proj):
    return jax.nn.sigmoid(x @ w_gate) * (p_i @ w_proj)


def setup_inputs(seed: int = 0) -> dict:
    key = jax.random.key(seed)
    ks = jax.random.split(key, 24)
    f32 = jnp.float32
    D, F = D_MODEL, D_FF
    nrm = lambda k, shape, s: jax.random.normal(k, shape, f32) * s
    return {
        'x': nrm(ks[0], (BATCH, SEQ, D), 1.0),
        'p': nrm(ks[1], (DEPTH, BATCH, SEQ, PLE_DIM), 1.0),
        'ffn1_w_in': nrm(ks[2], (DEPTH, D, 2 * F), D ** -0.5),
        'ffn1_w_out': nrm(ks[3], (DEPTH, F, D), F ** -0.5 * BETA),
        'ffn2_w_in': nrm(ks[4], (DEPTH, D, 2 * F), D ** -0.5),
        'ffn2_w_out': nrm(ks[5], (DEPTH, F, D), F ** -0.5 * BETA),
        'ln_g': 1.0 + nrm(ks[6], (DEPTH, 4, D), 0.02),
        'ln_b': nrm(ks[7], (DEPTH, 4, D), 0.02),
        'ple_w_gate': nrm(ks[8], (DEPTH, D, D), D ** -0.5),
        'ple_w_proj': nrm(ks[9], (DEPTH, PLE_DIM, D), PLE_DIM ** -0.5 * BETA),
        'a_w_in': nrm(ks[10], (N_A, D, 3 * D), D ** -0.5),
        'a_w_out': nrm(ks[11], (N_A, D, D), D ** -0.5 * BETA),
        'a_lam_q1': nrm(ks[12], (N_A, A_HEAD_DIM), 0.1),
        'a_lam_k1': nrm(ks[13], (N_A, A_HEAD_DIM), 0.1),
        'a_lam_q2': nrm(ks[14], (N_A, A_HEAD_DIM), 0.1),
        'a_lam_k2': nrm(ks[15], (N_A, A_HEAD_DIM), 0.1),
        'a_subln_g': 1.0 + nrm(ks[16], (N_A, 2 * A_HEAD_DIM), 0.02),
        'b_w_in': nrm(ks[17], (N_B, D, 3 * B_HEADS * B_HEAD_DIM), D ** -0.5),
        'b_w_out': nrm(ks[18], (N_B, B_HEADS * B_HEAD_DIM, D), (B_HEADS * B_HEAD_DIM) ** -0.5 * BETA),
        'c_w_in': nrm(ks[19], (N_C, D, C_IN_COLS), D ** -0.5),
        'c_w_out': nrm(ks[20], (N_C, C_Q, D), C_Q ** -0.5 * BETA),
    }


def reference(x, p, ffn1_w_in, ffn1_w_out, ffn2_w_in, ffn2_w_out, ln_g, ln_b,
              ple_w_gate, ple_w_proj, a_w_in, a_w_out, a_lam_q1, a_lam_k1, a_lam_q2,
              a_lam_k2, a_subln_g, b_w_in, b_w_out, c_w_in, c_w_out):
    for i in range(DEPTH):
        m, j = i % N_MIXERS, i // N_MIXERS
        x = layer_norm(ALPHA * x + 0.5 * swiglu_ffn(x, ffn1_w_in[i], ffn1_w_out[i]), ln_g[i, 0], ln_b[i, 0])
        if m == 0:
            lambda_init = 0.8 - 0.6 * math.exp(-0.3 * i)
            y = diff_attention(x, a_w_in[j], a_w_out[j], a_lam_q1[j], a_lam_k1[j],
                               a_lam_q2[j], a_lam_k2[j], a_subln_g[j], lambda_init)
        elif m == 1:
            y = moba_attention(x, b_w_in[j], b_w_out[j])
        else:
            y = dsa_attention(x, c_w_in[j], c_w_out[j])
        x = layer_norm(ALPHA * x + y, ln_g[i, 1], ln_b[i, 1])
        x = layer_norm(ALPHA * x + 0.5 * swiglu_ffn(x, ffn2_w_in[i], ffn2_w_out[i]), ln_g[i, 2], ln_b[i, 2])
        x = layer_norm(ALPHA * x + per_layer_embedding(x, p[i], ple_w_gate[i], ple_w_proj[i]), ln_g[i, 3], ln_b[i, 3])
    return x
```

```python
import functools
import math

import numpy as np
import jax
import jax.numpy as jnp
from jax import lax
from jax.experimental import pallas as pl
from jax.experimental.pallas import tpu as pltpu

F32 = jnp.float32
BF16 = jnp.bfloat16

N_MIXERS = 3
A_HEAD_DIM = 128
B_HEADS = 16
B_HEAD_DIM = 128
MOBA_BLOCK = 256
MOBA_TOPK = 3
C_HEADS = 16
C_KV_HEADS = 4
C_HEAD_DIM = 128
IDX_HEADS = 16
IDX_DIM = 64
DSA_TOPK_MAX = 256
LN_EPS = 1e-5
RMS_EPS = 1e-6

LANES = 128
NEG = -1e30
INT_MIN = -(2 ** 31)
VMEM_LIMIT = 56 * 1024 * 1024


def _alibi_slopes(n):
    return 2.0 ** (-8.0 * np.arange(1, n + 1, dtype=np.float32) / n)


def _params(*sem):
    return pltpu.CompilerParams(dimension_semantics=sem, vmem_limit_bytes=VMEM_LIMIT)


def _dot(a, b):
    return jnp.dot(a, b, preferred_element_type=F32)


def _dot_nt(a, b):
    return lax.dot_general(a, b, (((1,), (1,)), ((), ())), preferred_element_type=F32)


def _layer_norm(z, g, b):
    mu = jnp.mean(z, -1, keepdims=True)
    zc = z - mu
    var = jnp.mean(zc * zc, -1, keepdims=True)
    return zc * lax.rsqrt(var + LN_EPS) * g + b


def _ffn_kernel(x_ref, wg_ref, wu_ref, wo_ref, g_ref, b_ref, o_ref, xb_ref, acc_ref, *, alpha):
    j = pl.program_id(1)

    @pl.when(j == 0)
    def _():
        xb_ref[...] = x_ref[...].astype(BF16)
        acc_ref[...] = jnp.zeros_like(acc_ref)

    xb = xb_ref[...]
    gate = _dot(xb, wg_ref[...])
    up = _dot(xb, wu_ref[...])
    h = (gate * jax.nn.sigmoid(gate) * up).astype(BF16)
    acc_ref[...] += _dot(h, wo_ref[...])

    @pl.when(j == pl.num_programs(1) - 1)
    def _():
        z = alpha * x_ref[...] + 0.5 * acc_ref[...]
        o_ref[...] = _layer_norm(z, g_ref[...], b_ref[...])


def _ffn(x, w_in, w_out, g, b, alpha, tm=512, tf=512):
    T, D = x.shape
    F = w_out.shape[0]
    nf = F // tf
    assert T % tm == 0 and F % tf == 0 and w_in.shape == (D, 2 * F)
    return pl.pallas_call(
        functools.partial(_ffn_kernel, alpha=alpha),
        grid=(T // tm, nf),
        in_specs=[
            pl.BlockSpec((tm, D), lambda i, j: (i, 0)),
            pl.BlockSpec((D, tf), lambda i, j: (0, j)),
            pl.BlockSpec((D, tf), lambda i, j: (0, j + nf)),
            pl.BlockSpec((tf, D), lambda i, j: (j, 0)),
            pl.BlockSpec((1, D), lambda i, j: (0, 0)),
            pl.BlockSpec((1, D), lambda i, j: (0, 0)),
        ],
        out_specs=pl.BlockSpec((tm, D), lambda i, j: (i, 0)),
        out_shape=jax.ShapeDtypeStruct((T, D), F32),
        scratch_shapes=[pltpu.VMEM((tm, D), BF16), pltpu.VMEM((tm, D), F32)],
        compiler_params=_params("parallel", "arbitrary"),
        name="ffn",
    )(x, w_in, w_in, w_out, g, b)


def _proj_kernel(x_ref, w_ref, o_ref, xb_ref):
    @pl.when(pl.program_id(1) == 0)
    def _():
        xb_ref[...] = x_ref[...].astype(BF16)

    o_ref[...] = _dot(xb_ref[...], w_ref[...]).astype(o_ref.dtype)


def _proj(x, w, out_dtype, tm=512, tn=512):
    T, D = x.shape
    N = w.shape[1]
    tn = min(tn, N)
    assert T % tm == 0 and N % tn == 0
    return pl.pallas_call(
        _proj_kernel,
        grid=(T // tm, N // tn),
        in_specs=[
            pl.BlockSpec((tm, D), lambda i, j: (i, 0)),
            pl.BlockSpec((D, tn), lambda i, j: (0, j)),
        ],
        out_specs=pl.BlockSpec((tm, tn), lambda i, j: (i, j)),
        out_shape=jax.ShapeDtypeStruct((T, N), out_dtype),
        scratch_shapes=[pltpu.VMEM((tm, D), BF16)],
        compiler_params=_params("parallel", "arbitrary"),
        name="proj",
    )(x, w)


def _out_ln_kernel(x_ref, o_ref, w_ref, g_ref, b_ref, y_ref, *, alpha):
    z = alpha * x_ref[...] + _dot(o_ref[...], w_ref[...])
    y_ref[...] = _layer_norm(z, g_ref[...], b_ref[...])


def _out_ln(x, o, w, g, b, alpha, tm=256):
    T, D = x.shape
    K = o.shape[1]
    assert T % tm == 0 and w.shape == (K, D)
    return pl.pallas_call(
        functools.partial(_out_ln_kernel, alpha=alpha),
        grid=(T // tm,),
        in_specs=[
            pl.BlockSpec((tm, D), lambda i: (i, 0)),
            pl.BlockSpec((tm, K), lambda i: (i, 0)),
            pl.BlockSpec((K, D), lambda i: (0, 0)),
            pl.BlockSpec((1, D), lambda i: (0, 0)),
            pl.BlockSpec((1, D), lambda i: (0, 0)),
        ],
        out_specs=pl.BlockSpec((tm, D), lambda i: (i, 0)),
        out_shape=jax.ShapeDtypeStruct((T, D), F32),
        compiler_params=_params("parallel"),
        name="out_ln",
    )(x, o, w, g, b)


def _ple_kernel(x_ref, p_ref, wg_ref, wp_ref, g_ref, b_ref, y_ref, *, alpha):
    x = x_ref[...]
    gate = jax.nn.sigmoid(_dot(x.astype(BF16), wg_ref[...]))
    emb = _dot(p_ref[...].astype(BF16), wp_ref[...])
    y_ref[...] = _layer_norm(alpha * x + gate * emb, g_ref[...], b_ref[...])


def _ple(x, p, wg, wp, g, b, alpha, tm=256):
    T, D = x.shape
    P = p.shape[1]
    assert T % tm == 0
    return pl.pallas_call(
        functools.partial(_ple_kernel, alpha=alpha),
        grid=(T // tm,),
        in_specs=[
            pl.BlockSpec((tm, D), lambda i: (i, 0)),
            pl.BlockSpec((tm, P), lambda i: (i, 0)),
            pl.BlockSpec((D, D), lambda i: (0, 0)),
            pl.BlockSpec((P, D), lambda i: (0, 0)),
            pl.BlockSpec((1, D), lambda i: (0, 0)),
            pl.BlockSpec((1, D), lambda i: (0, 0)),
        ],
        out_specs=pl.BlockSpec((tm, D), lambda i: (i, 0)),
        out_shape=jax.ShapeDtypeStruct((T, D), F32),
        compiler_params=_params("parallel"),
        name="ple",
    )(x, p, wg, wp, g, b)


def _flash_update(s, valid, v, m_ref, l_ref, acc_ref, zero_masked):
    if valid is not None:
        s = jnp.where(valid, s, NEG)
    m_old = m_ref[...]
    m_new = jnp.maximum(m_old, jnp.max(s, -1, keepdims=True))
    p = jnp.exp(s - m_new)
    if zero_masked:
        p = jnp.where(valid, p, 0.0)
    a = jnp.exp(m_old - m_new)
    l_ref[...] = a * l_ref[...] + jnp.sum(p, -1, keepdims=True)
    acc_ref[...] = a * acc_ref[...] + _dot(p.astype(BF16), v)
    m_ref[...] = m_new


def _row_minus_col(rows, cols):
    r = lax.broadcasted_iota(jnp.int32, (rows, cols), 0)
    c = lax.broadcasted_iota(jnp.int32, (rows, cols), 1)
    return r - c


def _diff_attn_kernel(slopes_ref, lam_ref, subg_ref, q_ref, k_ref, v_ref, o_ref,
                      m_ref, l_ref, acc_ref, *, d, scale, lambda_init):
    h = pl.program_id(1)
    qi = pl.program_id(2)
    tq = q_ref.shape[0]
    slope = slopes_ref[h]
    m_ref[...] = jnp.full_like(m_ref, NEG)
    l_ref[...] = jnp.zeros_like(l_ref)
    acc_ref[...] = jnp.zeros_like(acc_ref)
    rel_i = _row_minus_col(tq, tq)
    rel = rel_i.astype(F32)
    q = q_ref[...]

    def chunk(c, diagonal):
        start = pl.multiple_of(c * tq, tq)
        k = k_ref[pl.ds(start, tq), :]
        v = v_ref[pl.ds(start, tq), :]
        off = ((qi - c) * tq).astype(F32)
        bias = -slope * (rel + off)
        valid = (rel_i >= 0) if diagonal else None
        for mm in range(2):
            s = _dot_nt(q[:, mm * d:(mm + 1) * d], k[:, mm * d:(mm + 1) * d]) * scale + bias
            _flash_update(s, valid, v, m_ref.at[mm], l_ref.at[mm], acc_ref.at[mm], False)

    chunk(qi, True)

    def body(c, carry):
        chunk(c, False)
        return carry

    lax.fori_loop(0, qi, body, 0)

    lam_p = lam_ref[...]
    lam = (jnp.exp(jnp.sum(lam_p[0:1] * lam_p[1:2], -1, keepdims=True))
           - jnp.exp(jnp.sum(lam_p[2:3] * lam_p[3:4], -1, keepdims=True)) + lambda_init)
    o = acc_ref[0] / l_ref[0] - lam * (acc_ref[1] / l_ref[1])
    o = o * lax.rsqrt(jnp.mean(o * o, -1, keepdims=True) + RMS_EPS) * subg_ref[...] * (1.0 - lambda_init)
    o_ref[...] = o.astype(o_ref.dtype)


def _diff_attn(qkv, lam_p, subln_g, B, S, H, d, lambda_init, tq=256):
    T = B * S
    nq = S // tq
    assert S % tq == 0 and qkv.shape == (T, 3 * H * 2 * d)
    slopes = jnp.asarray(_alibi_slopes(H), F32)
    return pl.pallas_call(
        functools.partial(_diff_attn_kernel, d=d, scale=d ** -0.5, lambda_init=lambda_init),
        grid=(B, H, nq),
        in_specs=[
            pl.BlockSpec(memory_space=pltpu.SMEM),
            pl.BlockSpec((4, d), lambda b, h, i: (0, 0)),
            pl.BlockSpec((1, 2 * d), lambda b, h, i: (0, 0)),
            pl.BlockSpec((tq, 2 * d), lambda b, h, i: (b * nq + i, h)),
            pl.BlockSpec((S, 2 * d), lambda b, h, i: (b, H + h)),
            pl.BlockSpec((S, 2 * d), lambda b, h, i: (b, 2 * H + h)),
        ],
        out_specs=pl.BlockSpec((tq, 2 * d), lambda b, h, i: (b * nq + i, h)),
        out_shape=jax.ShapeDtypeStruct((T, H * 2 * d), BF16),
        scratch_shapes=[pltpu.VMEM((2, tq, 1), F32), pltpu.VMEM((2, tq, 1), F32),
                        pltpu.VMEM((2, tq, 2 * d), F32)],
        compiler_params=_params("parallel", "parallel", "arbitrary"),
        name="diff_attn",
    )(slopes, lam_p, subln_g, qkv, qkv, qkv)


def _moba_kernel(slopes_ref, q_ref, k_ref, v_ref, o_ref, km_ref, m_ref, l_ref, acc_ref,
                 *, nb, n_sel, scale):
    h = pl.program_id(1)
    own = pl.program_id(2)
    bs = q_ref.shape[0]
    slope = slopes_ref[h]

    @pl.when(own == 0)
    def _():
        km_ref[...] = jnp.zeros_like(km_ref)
        for n in range(nb):
            km_ref[n:n + 1, :] = jnp.mean(k_ref[n * bs:(n + 1) * bs, :].astype(F32), 0, keepdims=True)

    q = q_ref[...]
    km = km_ref[...]
    km_hi = km.astype(BF16)
    km_lo = (km - km_hi.astype(F32)).astype(BF16)
    gate = _dot_nt(q, km_hi) + _dot_nt(q, km_lo)
    lane = lax.broadcasted_iota(jnp.int32, gate.shape, 1)
    rank = jnp.zeros(gate.shape, jnp.int32)
    for m in range(nb - 1):
        gm = gate[:, m:m + 1]
        beats = (gm > gate) | ((gm == gate) & (lane > m))
        rank = rank + beats.astype(jnp.int32) * (own > m).astype(jnp.int32)
    sel = ((rank < n_sel) & (lane < own)).astype(F32)

    m_ref[...] = jnp.full_like(m_ref, NEG)
    l_ref[...] = jnp.zeros_like(l_ref)
    acc_ref[...] = jnp.zeros_like(acc_ref)
    rel_i = _row_minus_col(bs, bs)
    rel = rel_i.astype(F32)

    start = pl.multiple_of(own * bs, bs)
    s = _dot_nt(q, k_ref[pl.ds(start, bs), :]) * scale - slope * rel
    _flash_update(s, rel_i >= 0, v_ref[pl.ds(start, bs), :], m_ref, l_ref, acc_ref, False)

    for n in range(nb - 1):
        @pl.when(n < own)
        def _(n=n):
            off = ((own - n) * bs).astype(F32)
            s = _dot_nt(q, k_ref[n * bs:(n + 1) * bs, :]) * scale - slope * (rel + off)
            _flash_update(s, sel[:, n:n + 1] > 0.5, v_ref[n * bs:(n + 1) * bs, :],
                          m_ref, l_ref, acc_ref, False)

    o_ref[...] = (acc_ref[...] / l_ref[...]).astype(o_ref.dtype)


def _moba_attn(qkv, B, S, H, d, bs):
    T = B * S
    nb = S // bs
    assert S % bs == 0 and nb <= LANES and qkv.shape == (T, 3 * H * d)
    n_sel = min(MOBA_TOPK, nb - 1)
    slopes = jnp.asarray(_alibi_slopes(H), F32)
    return pl.pallas_call(
        functools.partial(_moba_kernel, nb=nb, n_sel=n_sel, scale=d ** -0.5),
        grid=(B, H, nb),
        in_specs=[
            pl.BlockSpec(memory_space=pltpu.SMEM),
            pl.BlockSpec((bs, d), lambda b, h, i: (b * nb + i, h)),
            pl.BlockSpec((S, d), lambda b, h, i: (b, H + h)),
            pl.BlockSpec((S, d), lambda b, h, i: (b, 2 * H + h)),
        ],
        out_specs=pl.BlockSpec((bs, d), lambda b, h, i: (b * nb + i, h)),
        out_shape=jax.ShapeDtypeStruct((T, H * d), BF16),
        scratch_shapes=[pltpu.VMEM((LANES, d), F32), pltpu.VMEM((bs, 1), F32),
                        pltpu.VMEM((bs, 1), F32), pltpu.VMEM((bs, d), F32)],
        compiler_params=_params("parallel", "parallel", "arbitrary"),
        name="moba_attn",
    )(slopes, qkv, qkv, qkv)


def _dsa_kernel(q_ref, k_ref, v_ref, qi_ref, tailq_ref, tailk_ref, o_ref,
                kib_ref, key_ref, m_ref, l_ref, acc_ref, *, n_keep, tkc, scale):
    i = pl.program_id(1)
    tq = q_ref.shape[0]
    G, R, d = C_KV_HEADS, C_HEADS // C_KV_HEADS, C_HEAD_DIM
    qstart = i * tq
    nkc = (qstart + tq + tkc - 1) // tkc

    @pl.when(i == 0)
    def _():
        kib_ref[...] = tailk_ref[:, 0:IDX_DIM].astype(BF16)

    rel_i = _row_minus_col(tq, tkc)

    w = tailq_ref[:, IDX_DIM:IDX_DIM + IDX_HEADS] * (IDX_HEADS ** -0.5 * IDX_DIM ** -0.5)

    def score_chunk(c, carry):
        kc = kib_ref[pl.ds(pl.multiple_of(c * tkc, tkc), tkc), :]
        acc = jnp.zeros((tq, tkc), F32)
        for hh in range(IDX_HEADS):
            dots = _dot_nt(qi_ref[:, hh * IDX_DIM:(hh + 1) * IDX_DIM], kc)
            acc = acc + w[:, hh:hh + 1] * jnp.maximum(dots, 0.0)
        bits = lax.bitcast_convert_type(acc, jnp.int32)
        key = bits ^ ((bits >> 31) & jnp.int32(0x7FFFFFFF))
        causal = rel_i + (qstart - c * tkc) >= 0
        key_ref[c] = jnp.where(causal, key, jnp.int32(INT_MIN))
        return carry

    lax.fori_loop(0, nkc, score_chunk, 0)

    def count_ge(cand):
        def body(c, part):
            hit = (key_ref[c] >= cand).astype(jnp.int32)
            for t in range(tkc // LANES):
                part = part + hit[:, t * LANES:(t + 1) * LANES]
            return part
        part = lax.fori_loop(0, nkc, body, jnp.zeros((tq, LANES), jnp.int32))
        return jnp.sum(part, -1, keepdims=True)

    thr = jnp.where(count_ge(jnp.zeros((tq, 1), jnp.int32)) >= n_keep, jnp.int32(0), jnp.int32(INT_MIN))

    def bit_step(b, thr):
        cand = thr | jnp.left_shift(jnp.int32(1), 30 - b)
        return jnp.where(count_ge(cand) >= n_keep, cand, thr)

    thr = lax.fori_loop(0, 31, bit_step, thr)
    thr = jnp.maximum(thr, jnp.int32(INT_MIN + 1))

    m_ref[...] = jnp.full_like(m_ref, NEG)
    l_ref[...] = jnp.zeros_like(l_ref)
    acc_ref[...] = jnp.zeros_like(acc_ref)
    slopes = _alibi_slopes(C_HEADS).reshape(G, R)
    rel = rel_i.astype(F32)

    def attend(c, carry):
        start = pl.multiple_of(c * tkc, tkc)
        sel = key_ref[c] >= thr
        dist = rel + (qstart - c * tkc).astype(F32)
        sel_r = jnp.concatenate([sel] * R, axis=0)
        kc = k_ref[pl.ds(start, tkc), :]
        vc = v_ref[pl.ds(start, tkc), :]
        for g in range(G):
            qs = jnp.concatenate(
                [q_ref[:, (g * R + r) * d:(g * R + r + 1) * d] for r in range(R)], axis=0)
            bias = jnp.concatenate([-float(slopes[g, r]) * dist for r in range(R)], axis=0)
            s = _dot_nt(qs, kc[:, g * d:(g + 1) * d]) * scale + bias
            _flash_update(s, sel_r, vc[:, g * d:(g + 1) * d],
                          m_ref.at[g], l_ref.at[g], acc_ref.at[g], True)
        return carry

    lax.fori_loop(0, nkc, attend, 0)

    for g in range(G):
        o = acc_ref[g] / l_ref[g]
        for r in range(R):
            o_ref[:, (g * R + r) * d:(g * R + r + 1) * d] = o[r * tq:(r + 1) * tq].astype(o_ref.dtype)


def _dsa_attn(main, tail, B, S, tq=128, tkc=256):
    T = B * S
    nq = S // tq
    G, H, d = C_KV_HEADS, C_HEADS, C_HEAD_DIM
    R = H // G
    c_q, c_kv, c_iq = H * d, G * d, IDX_HEADS * IDX_DIM
    assert S % tq == 0 and S % tkc == 0 and tkc % tq == 0
    assert main.shape == (T, c_q + 2 * c_kv + c_iq) and tail.shape == (T, LANES)
    assert c_q % c_kv == 0 and (c_q + 2 * c_kv) % c_iq == 0
    n_keep = min(DSA_TOPK_MAX, S // 4)
    return pl.pallas_call(
        functools.partial(_dsa_kernel, n_keep=n_keep, tkc=tkc, scale=d ** -0.5),
        grid=(B, nq),
        in_specs=[
            pl.BlockSpec((tq, c_q), lambda b, i: (b * nq + i, 0)),
            pl.BlockSpec((S, c_kv), lambda b, i: (b, c_q // c_kv)),
            pl.BlockSpec((S, c_kv), lambda b, i: (b, c_q // c_kv + 1)),
            pl.BlockSpec((tq, c_iq), lambda b, i: (b * nq + i, (c_q + 2 * c_kv) // c_iq)),
            pl.BlockSpec((tq, LANES), lambda b, i: (b * nq + i, 0)),
            pl.BlockSpec((S, LANES), lambda b, i: (b, 0)),
        ],
        out_specs=pl.BlockSpec((tq, c_q), lambda b, i: (b * nq + i, 0)),
        out_shape=jax.ShapeDtypeStruct((T, c_q), BF16),
        scratch_shapes=[pltpu.VMEM((S, IDX_DIM), BF16), pltpu.VMEM((S // tkc, tq, tkc), jnp.int32),
                        pltpu.VMEM((G, R * tq, 1), F32), pltpu.VMEM((G, R * tq, 1), F32),
                        pltpu.VMEM((G, R * tq, d), F32)],
        compiler_params=_params("parallel", "arbitrary"),
        name="dsa_attn",
    )(main, main, main, main, tail, tail)


def kernel(x, p, ffn1_w_in, ffn1_w_out, ffn2_w_in, ffn2_w_out, ln_g, ln_b, ple_w_gate, ple_w_proj,
           a_w_in, a_w_out, a_lam_q1, a_lam_k1, a_lam_q2, a_lam_k2, a_subln_g, b_w_in, b_w_out,
           c_w_in, c_w_out):
    B, S, D = x.shape
    depth = p.shape[0]
    T = B * S
    alpha = (2.0 * depth) ** 0.25
    bf = lambda w: w.astype(BF16)
    x = x.reshape(T, D)
    for i in range(depth):
        m, j = i % N_MIXERS, i // N_MIXERS
        g = lambda s: ln_g[i, s].reshape(1, D)
        b = lambda s: ln_b[i, s].reshape(1, D)
        x = _ffn(x, bf(ffn1_w_in[i]), bf(ffn1_w_out[i]), g(0), b(0), alpha)
        if m == 0:
            d = A_HEAD_DIM
            H = D // (2 * d)
            lambda_init = 0.8 - 0.6 * math.exp(-0.3 * i)
            qkv = _proj(x, bf(a_w_in[j]), BF16)
            lam_p = jnp.stack([a_lam_q1[j], a_lam_k1[j], a_lam_q2[j], a_lam_k2[j]]).astype(F32)
            o = _diff_attn(qkv, lam_p, a_subln_g[j].reshape(1, 2 * d).astype(F32), B, S, H, d, lambda_init)
            w_out = a_w_out[j]
        elif m == 1:
            qkv = _proj(x, bf(b_w_in[j]), BF16)
            o = _moba_attn(qkv, B, S, B_HEADS, B_HEAD_DIM, MOBA_BLOCK)
            w_out = b_w_out[j]
        else:
            n_main = C_HEADS * C_HEAD_DIM + 2 * C_KV_HEADS * C_HEAD_DIM + IDX_HEADS * IDX_DIM
            w_main = bf(c_w_in[j][:, :n_main])
            w_tail = c_w_in[j][:, n_main:]
            w_tail = bf(jnp.pad(w_tail, ((0, 0), (0, LANES - w_tail.shape[1]))))
            main = _proj(x, w_main, BF16)
            tail = _proj(x, w_tail, F32)
            o = _dsa_attn(main, tail, B, S)
            w_out = c_w_out[j]
        x = _out_ln(x, o, bf(w_out), g(1), b(1), alpha)
        x = _ffn(x, bf(ffn2_w_in[i]), bf(ffn2_w_out[i]), g(2), b(2), alpha)
        x = _ple(x, p[i].reshape(T, -1), bf(ple_w_gate[i]), bf(ple_w_proj[i]), g(3), b(3), alpha)
    return x.reshape(B, S, D)
```

```python
import functools
import math

import numpy as np
import jax
import jax.numpy as jnp
from jax import lax
from jax.experimental import pallas as pl
from jax.experimental.pallas import tpu as pltpu

F32 = jnp.float32
BF16 = jnp.bfloat16

N_MIXERS = 3
A_HEAD_DIM = 128
B_HEADS = 16
B_HEAD_DIM = 128
MOBA_BLOCK = 256
MOBA_TOPK = 3
C_HEADS = 16
C_KV_HEADS = 4
C_HEAD_DIM = 128
IDX_HEADS = 16
IDX_DIM = 64
DSA_TOPK_MAX = 256
LN_EPS = 1e-5
RMS_EPS = 1e-6

LANES = 128
NEG = -1e30
INT_MIN = -(2 ** 31)
VMEM_LIMIT = 56 * 1024 * 1024


def _alibi_slopes(n):
    return 2.0 ** (-8.0 * np.arange(1, n + 1, dtype=np.float32) / n)


def _params(*sem):
    return pltpu.CompilerParams(dimension_semantics=sem, vmem_limit_bytes=VMEM_LIMIT)


def _dot(a, b):
    return jnp.dot(a, b, preferred_element_type=F32)


def _dot_nt(a, b):
    return lax.dot_general(a, b, (((1,), (1,)), ((), ())), preferred_element_type=F32)


def _layer_norm(z, g, b):
    mu = jnp.mean(z, -1, keepdims=True)
    zc = z - mu
    var = jnp.mean(zc * zc, -1, keepdims=True)
    return zc * lax.rsqrt(var + LN_EPS) * g + b


def _ffn_kernel(x_ref, wg_ref, wu_ref, wo_ref, g_ref, b_ref, o_ref, xb_ref, acc_ref, *, alpha):
    j = pl.program_id(1)

    @pl.when(j == 0)
    def _():
        xb_ref[...] = x_ref[...].astype(BF16)
        acc_ref[...] = jnp.zeros_like(acc_ref)

    xb = xb_ref[...]
    gate = _dot(xb, wg_ref[...])
    up = _dot(xb, wu_ref[...])
    h = (gate * jax.nn.sigmoid(gate) * up).astype(BF16)
    acc_ref[...] += _dot(h, wo_ref[...])

    @pl.when(j == pl.num_programs(1) - 1)
    def _():
        z = alpha * x_ref[...] + 0.5 * acc_ref[...]
        o_ref[...] = _layer_norm(z, g_ref[...], b_ref[...])


def _ffn(x, w_in, w_out, g, b, alpha, tm=512, tf=512):
    T, D = x.shape
    F = w_out.shape[0]
    nf = F // tf
    assert T % tm == 0 and F % tf == 0 and w_in.shape == (D, 2 * F)
    return pl.pallas_call(
        functools.partial(_ffn_kernel, alpha=alpha),
        grid=(T // tm, nf),
        in_specs=[
            pl.BlockSpec((tm, D), lambda i, j: (i, 0)),
            pl.BlockSpec((D, tf), lambda i, j: (0, j)),
            pl.BlockSpec((D, tf), lambda i, j: (0, j + nf)),
            pl.BlockSpec((tf, D), lambda i, j: (j, 0)),
            pl.BlockSpec((1, D), lambda i, j: (0, 0)),
            pl.BlockSpec((1, D), lambda i, j: (0, 0)),
        ],
        out_specs=pl.BlockSpec((tm, D), lambda i, j: (i, 0)),
        out_shape=jax.ShapeDtypeStruct((T, D), F32),
        scratch_shapes=[pltpu.VMEM((tm, D), BF16), pltpu.VMEM((tm, D), F32)],
        compiler_params=_params("parallel", "arbitrary"),
        name="ffn",
    )(x, w_in, w_in, w_out, g, b)


def _proj_kernel(x_ref, w_ref, o_ref, xb_ref):
    @pl.when(pl.program_id(1) == 0)
    def _():
        xb_ref[...] = x_ref[...].astype(BF16)

    o_ref[...] = _dot(xb_ref[...], w_ref[...]).astype(o_ref.dtype)


def _proj(x, w, out_dtype, tm=512, tn=512):
    T, D = x.shape
    N = w.shape[1]
    tn = min(tn, N)
    assert T % tm == 0 and N % tn == 0
    return pl.pallas_call(
        _proj_kernel,
        grid=(T // tm, N // tn),
        in_specs=[
            pl.BlockSpec((tm, D), lambda i, j: (i, 0)),
            pl.BlockSpec((D, tn), lambda i, j: (0, j)),
        ],
        out_specs=pl.BlockSpec((tm, tn), lambda i, j: (i, j)),
        out_shape=jax.ShapeDtypeStruct((T, N), out_dtype),
        scratch_shapes=[pltpu.VMEM((tm, D), BF16)],
        compiler_params=_params("parallel", "arbitrary"),
        name="proj",
    )(x, w)


def _out_ln_kernel(x_ref, o_ref, w_ref, g_ref, b_ref, y_ref, *, alpha):
    z = alpha * x_ref[...] + _dot(o_ref[...], w_ref[...])
    y_ref[...] = _layer_norm(z, g_ref[...], b_ref[...])


def _out_ln(x, o, w, g, b, alpha, tm=256):
    T, D = x.shape
    K = o.shape[1]
    assert T % tm == 0 and w.shape == (K, D)
    return pl.pallas_call(
        functools.partial(_out_ln_kernel, alpha=alpha),
        grid=(T // tm,),
        in_specs=[
            pl.BlockSpec((tm, D), lambda i: (i, 0)),
            pl.BlockSpec((tm, K), lambda i: (i, 0)),
            pl.BlockSpec((K, D), lambda i: (0, 0)),
            pl.BlockSpec((1, D), lambda i: (0, 0)),
            pl.BlockSpec((1, D), lambda i: (0, 0)),
        ],
        out_specs=pl.BlockSpec((tm, D), lambda i: (i, 0)),
        out_shape=jax.ShapeDtypeStruct((T, D), F32),
        compiler_params=_params("parallel"),
        name="out_ln",
    )(x, o, w, g, b)


def _ple_kernel(x_ref, p_ref, wg_ref, wp_ref, g_ref, b_ref, y_ref, *, alpha):
    x = x_ref[...]
    gate = jax.nn.sigmoid(_dot(x.astype(BF16), wg_ref[...]))
    emb = _dot(p_ref[...].astype(BF16), wp_ref[...])
    y_ref[...] = _layer_norm(alpha * x + gate * emb, g_ref[...], b_ref[...])


def _ple(x, p, wg, wp, g, b, alpha, tm=256):
    T, D = x.shape
    P = p.shape[1]
    assert T % tm == 0
    return pl.pallas_call(
        functools.partial(_ple_kernel, alpha=alpha),
        grid=(T // tm,),
        in_specs=[
            pl.BlockSpec((tm, D), lambda i: (i, 0)),
            pl.BlockSpec((tm, P), lambda i: (i, 0)),
            pl.BlockSpec((D, D), lambda i: (0, 0)),
            pl.BlockSpec((P, D), lambda i: (0, 0)),
            pl.BlockSpec((1, D), lambda i: (0, 0)),
            pl.BlockSpec((1, D), lambda i: (0, 0)),
        ],
        out_specs=pl.BlockSpec((tm, D), lambda i: (i, 0)),
        out_shape=jax.ShapeDtypeStruct((T, D), F32),
        compiler_params=_params("parallel"),
        name="ple",
    )(x, p, wg, wp, g, b)


def _row_max_bcast(mx_ref):
    m = jnp.max(mx_ref[...], -1, keepdims=True)
    mx_ref[...] = jnp.broadcast_to(m, mx_ref.shape)


def _causal_sweep(fn, qi):
    def body(j, carry):
        fn(2 * j, 2, False)
        return carry

    lax.fori_loop(0, qi // 2, body, 0)

    @pl.when(qi % 2 == 1)
    def _():
        fn(qi - 1, 2, True)

    @pl.when(qi % 2 == 0)
    def _():
        fn(qi, 1, True)


def _row_minus_col(rows, cols):
    r = lax.broadcasted_iota(jnp.int32, (rows, cols), 0)
    c = lax.broadcasted_iota(jnp.int32, (rows, cols), 1)
    return r - c


def _diff_attn_kernel(slopes_ref, lam_ref, subg_ref, q_ref, k_ref, v_ref, o_ref,
                      s_ref, bias_ref, mx_ref, ls_ref, acc_ref, *, d, scale, lambda_init):
    h = pl.program_id(1)
    qi = pl.program_id(2)
    tq = q_ref.shape[0]
    slope = slopes_ref[h]
    q = q_ref[...]
    mx_ref[...] = jnp.full_like(mx_ref, NEG)
    ls_ref[...] = jnp.zeros_like(ls_ref)
    acc_ref[...] = jnp.zeros_like(acc_ref)

    bias_ref[...] = -slope * _row_minus_col(tq, 2 * tq).astype(F32)

    def span_scores(c0, width, diagonal):
        rows = pl.ds(pl.multiple_of(c0 * tq, tq), width * tq)
        bias = bias_ref[:, :width * tq] - slope * ((qi - c0) * tq).astype(F32)
        for mm in range(2):
            s = _dot_nt(q[:, mm * d:(mm + 1) * d], k_ref[rows, mm * d:(mm + 1) * d]) * scale + bias
            if diagonal:
                s = jnp.where(_row_minus_col(tq, width * tq) + (qi - c0) * tq >= 0, s, NEG)
            mx = mx_ref[mm]
            for w in range(width):
                sw = s[:, w * tq:(w + 1) * tq]
                s_ref[mm, c0 + w] = sw
                mx = jnp.maximum(mx, sw)
            mx_ref[mm] = mx

    def span_accumulate(c0, width, diagonal):
        del diagonal
        v = v_ref[pl.ds(pl.multiple_of(c0 * tq, tq), width * tq), :]
        for mm in range(2):
            ps = [jnp.exp(s_ref[mm, c0 + w] - mx_ref[mm]) for w in range(width)]
            ls_ref[mm] += functools.reduce(lambda a, b: a + b, ps)
            p = ps[0] if width == 1 else jnp.concatenate(ps, axis=1)
            acc_ref[mm] += _dot(p.astype(BF16), v)

    _causal_sweep(span_scores, qi)
    for mm in range(2):
        _row_max_bcast(mx_ref.at[mm])
    _causal_sweep(span_accumulate, qi)

    lam_p = lam_ref[...]
    lam = (jnp.exp(jnp.sum(lam_p[0:1] * lam_p[1:2], -1, keepdims=True))
           - jnp.exp(jnp.sum(lam_p[2:3] * lam_p[3:4], -1, keepdims=True)) + lambda_init)
    l0 = jnp.sum(ls_ref[0], -1, keepdims=True)
    l1 = jnp.sum(ls_ref[1], -1, keepdims=True)
    o = acc_ref[0] / l0 - lam * (acc_ref[1] / l1)
    o = o * lax.rsqrt(jnp.mean(o * o, -1, keepdims=True) + RMS_EPS) * subg_ref[...] * (1.0 - lambda_init)
    o_ref[...] = o.astype(o_ref.dtype)


def _diff_attn(qkv, lam_p, subln_g, B, S, H, d, lambda_init, tq=256):
    T = B * S
    nq = S // tq
    assert S % tq == 0 and qkv.shape == (T, 3 * H * 2 * d)
    slopes = jnp.asarray(_alibi_slopes(H), F32)
    return pl.pallas_call(
        functools.partial(_diff_attn_kernel, d=d, scale=d ** -0.5, lambda_init=lambda_init),
        grid=(B, H, nq),
        in_specs=[
            pl.BlockSpec(memory_space=pltpu.SMEM),
            pl.BlockSpec((4, d), lambda b, h, i: (0, 0)),
            pl.BlockSpec((1, 2 * d), lambda b, h, i: (0, 0)),
            pl.BlockSpec((tq, 2 * d), lambda b, h, i: (b * nq + i, h)),
            pl.BlockSpec((S, 2 * d), lambda b, h, i: (b, H + h)),
            pl.BlockSpec((S, 2 * d), lambda b, h, i: (b, 2 * H + h)),
        ],
        out_specs=pl.BlockSpec((tq, 2 * d), lambda b, h, i: (b * nq + i, h)),
        out_shape=jax.ShapeDtypeStruct((T, H * 2 * d), BF16),
        scratch_shapes=[pltpu.VMEM((2, nq, tq, tq), F32), pltpu.VMEM((tq, 2 * tq), F32),
                        pltpu.VMEM((2, tq, tq), F32),
                        pltpu.VMEM((2, tq, tq), F32), pltpu.VMEM((2, tq, 2 * d), F32)],
        compiler_params=_params("parallel", "parallel", "arbitrary"),
        name="diff_attn",
    )(slopes, lam_p, subln_g, qkv, qkv, qkv)


def _moba_kernel(slopes_ref, q_ref, k_ref, v_ref, o_ref, km_ref, mb_ref, s_ref, bias_ref,
                 mx_ref, ls_ref, acc_ref, *, nb, n_sel, scale):
    h = pl.program_id(1)
    own = pl.program_id(2)
    bs = q_ref.shape[0]
    slope = slopes_ref[h]

    @pl.when(own == 0)
    def _():
        km_ref[...] = jnp.zeros_like(km_ref)
        for n in range(nb):
            km_ref[n:n + 1, :] = jnp.mean(k_ref[n * bs:(n + 1) * bs, :].astype(F32), 0, keepdims=True)

    q = q_ref[...]
    km = km_ref[...]
    km_hi = km.astype(BF16)
    km_lo = (km - km_hi.astype(F32)).astype(BF16)
    gate = _dot_nt(q, km_hi) + _dot_nt(q, km_lo)
    lane = lax.broadcasted_iota(jnp.int32, gate.shape, 1)
    rank = jnp.zeros(gate.shape, jnp.int32)
    for m in range(nb - 1):
        gm = gate[:, m:m + 1]
        beats = (gm > gate) | ((gm == gate) & (lane > m))
        rank = rank + beats.astype(jnp.int32) * (own > m).astype(jnp.int32)
    sel = (rank < n_sel) & (lane < own)

    for n in range(nb - 1):
        @pl.when(n < own)
        def _(n=n):
            off = -slope * ((own - n) * bs).astype(F32)
            mb_ref[n] = jnp.broadcast_to(jnp.where(sel[:, n:n + 1], off, NEG), (bs, bs))

    rel_i = _row_minus_col(bs, bs)
    bias_ref[...] = -slope * rel_i.astype(F32)
    mx_ref[...] = jnp.full_like(mx_ref, NEG)
    ls_ref[...] = jnp.zeros_like(ls_ref)
    acc_ref[...] = jnp.zeros_like(acc_ref)

    def span_scores(c0, width, diagonal):
        rows = pl.ds(pl.multiple_of(c0 * bs, bs), width * bs)
        qk = _dot_nt(q, k_ref[rows, :]) * scale
        mx = mx_ref[...]
        for w in range(width):
            sw = qk[:, w * bs:(w + 1) * bs] + bias_ref[...]
            if diagonal and w == width - 1:
                sw = jnp.where(rel_i >= 0, sw, NEG)
            else:
                sw = sw + mb_ref[c0 + w]
            s_ref[c0 + w] = sw
            mx = jnp.maximum(mx, sw)
        mx_ref[...] = mx

    def span_accumulate(c0, width, diagonal):
        del diagonal
        v = v_ref[pl.ds(pl.multiple_of(c0 * bs, bs), width * bs), :]
        ps = [jnp.exp(s_ref[c0 + w] - mx_ref[...]) for w in range(width)]
        ls_ref[...] += functools.reduce(lambda a, b: a + b, ps)
        p = ps[0] if width == 1 else jnp.concatenate(ps, axis=1)
        acc_ref[...] += _dot(p.astype(BF16), v)

    _causal_sweep(span_scores, own)
    _row_max_bcast(mx_ref)
    _causal_sweep(span_accumulate, own)

    l = jnp.sum(ls_ref[...], -1, keepdims=True)
    o_ref[...] = (acc_ref[...] / l).astype(o_ref.dtype)


def _moba_attn(qkv, B, S, H, d, bs):
    T = B * S
    nb = S // bs
    assert S % bs == 0 and nb <= LANES and qkv.shape == (T, 3 * H * d)
    n_sel = min(MOBA_TOPK, nb - 1)
    slopes = jnp.asarray(_alibi_slopes(H), F32)
    return pl.pallas_call(
        functools.partial(_moba_kernel, nb=nb, n_sel=n_sel, scale=d ** -0.5),
        grid=(B, H, nb),
        in_specs=[
            pl.BlockSpec(memory_space=pltpu.SMEM),
            pl.BlockSpec((bs, d), lambda b, h, i: (b * nb + i, h)),
            pl.BlockSpec((S, d), lambda b, h, i: (b, H + h)),
            pl.BlockSpec((S, d), lambda b, h, i: (b, 2 * H + h)),
        ],
        out_specs=pl.BlockSpec((bs, d), lambda b, h, i: (b * nb + i, h)),
        out_shape=jax.ShapeDtypeStruct((T, H * d), BF16),
        scratch_shapes=[pltpu.VMEM((LANES, d), F32), pltpu.VMEM((nb, bs, bs), F32),
                        pltpu.VMEM((nb, bs, bs), F32), pltpu.VMEM((bs, bs), F32),
                        pltpu.VMEM((bs, bs), F32), pltpu.VMEM((bs, bs), F32), pltpu.VMEM((bs, d), F32)],
        compiler_params=_params("parallel", "parallel", "arbitrary"),
        name="moba_attn",
    )(slopes, qkv, qkv, qkv)


def _dsa_kernel(q_ref, k_ref, v_ref, qi_ref, tailq_ref, tailk_ref, o_ref,
                kib_ref, key_ref, s_ref, mx_ref, ls_ref, acc_ref, *, n_keep, tkc, scale):
    i = pl.program_id(1)
    tq = q_ref.shape[0]
    G, R, d = C_KV_HEADS, C_HEADS // C_KV_HEADS, C_HEAD_DIM
    qstart = i * tq
    nkc = (qstart + tq + tkc - 1) // tkc

    @pl.when(i == 0)
    def _():
        kib_ref[...] = tailk_ref[:, 0:IDX_DIM].astype(BF16)

    rel_i = _row_minus_col(tq, tkc)

    w = tailq_ref[:, IDX_DIM:IDX_DIM + IDX_HEADS] * (IDX_HEADS ** -0.5 * IDX_DIM ** -0.5)

    def score_chunk(c, carry):
        kc = kib_ref[pl.ds(pl.multiple_of(c * tkc, tkc), tkc), :]
        acc = jnp.zeros((tq, tkc), F32)
        for hh in range(IDX_HEADS):
            dots = _dot_nt(qi_ref[:, hh * IDX_DIM:(hh + 1) * IDX_DIM], kc)
            acc = acc + w[:, hh:hh + 1] * jnp.maximum(dots, 0.0)
        bits = lax.bitcast_convert_type(acc, jnp.int32)
        key = bits ^ ((bits >> 31) & jnp.int32(0x7FFFFFFF))
        causal = rel_i + (qstart - c * tkc) >= 0
        key_ref[c] = jnp.where(causal, key, jnp.int32(INT_MIN))
        return carry

    lax.fori_loop(0, nkc, score_chunk, 0)

    def count_ge(cand):
        def body(c, part):
            hit = (key_ref[c] >= cand).astype(jnp.int32)
            for t in range(tkc // LANES):
                part = part + hit[:, t * LANES:(t + 1) * LANES]
            return part
        part = lax.fori_loop(0, nkc, body, jnp.zeros((tq, LANES), jnp.int32))
        return jnp.sum(part, -1, keepdims=True)

    thr = jnp.where(count_ge(jnp.zeros((tq, 1), jnp.int32)) >= n_keep, jnp.int32(0), jnp.int32(INT_MIN))

    def bit_step(b, thr):
        cand = thr | jnp.left_shift(jnp.int32(1), 30 - b)
        return jnp.where(count_ge(cand) >= n_keep, cand, thr)

    thr = lax.fori_loop(0, 31, bit_step, thr)
    thr = jnp.maximum(thr, jnp.int32(INT_MIN + 1))

    mx_ref[...] = jnp.full_like(mx_ref, NEG)
    ls_ref[...] = jnp.zeros_like(ls_ref)
    acc_ref[...] = jnp.zeros_like(acc_ref)
    slopes = _alibi_slopes(C_HEADS).reshape(G, R)
    rel = rel_i.astype(F32)

    def score_pass(c, carry):
        rows = pl.ds(pl.multiple_of(c * tkc, tkc), tkc)
        sel_r = jnp.concatenate([key_ref[c] >= thr] * R, axis=0)
        dist = rel + (qstart - c * tkc).astype(F32)
        for g in range(G):
            qs = jnp.concatenate(
                [q_ref[:, (g * R + r) * d:(g * R + r + 1) * d] for r in range(R)], axis=0)
            bias = jnp.concatenate([-float(slopes[g, r]) * dist for r in range(R)], axis=0)
            s = jnp.where(sel_r, _dot_nt(qs, k_ref[rows, g * d:(g + 1) * d]) * scale + bias, NEG)
            s_ref[g, c] = s
            mx_ref[g] = jnp.maximum(mx_ref[g], s)
        return carry

    def acc_pass(c, carry):
        rows = pl.ds(pl.multiple_of(c * tkc, tkc), tkc)
        for g in range(G):
            p = jnp.exp(s_ref[g, c] - mx_ref[g])
            ls_ref[g] += p
            acc_ref[g] += _dot(p.astype(BF16), v_ref[rows, g * d:(g + 1) * d])
        return carry

    lax.fori_loop(0, nkc, score_pass, 0)
    for g in range(G):
        _row_max_bcast(mx_ref.at[g])
    lax.fori_loop(0, nkc, acc_pass, 0)

    for g in range(G):
        o = acc_ref[g] / jnp.sum(ls_ref[g], -1, keepdims=True)
        for r in range(R):
            o_ref[:, (g * R + r) * d:(g * R + r + 1) * d] = o[r * tq:(r + 1) * tq].astype(o_ref.dtype)


def _dsa_attn(main, tail, B, S, tq=128, tkc=256):
    T = B * S
    nq = S // tq
    G, H, d = C_KV_HEADS, C_HEADS, C_HEAD_DIM
    R = H // G
    c_q, c_kv, c_iq = H * d, G * d, IDX_HEADS * IDX_DIM
    assert S % tq == 0 and S % tkc == 0 and tkc % tq == 0
    assert main.shape == (T, c_q + 2 * c_kv + c_iq) and tail.shape == (T, LANES)
    assert c_q % c_kv == 0 and (c_q + 2 * c_kv) % c_iq == 0
    n_keep = min(DSA_TOPK_MAX, S // 4)
    return pl.pallas_call(
        functools.partial(_dsa_kernel, n_keep=n_keep, tkc=tkc, scale=d ** -0.5),
        grid=(B, nq),
        in_specs=[
            pl.BlockSpec((tq, c_q), lambda b, i: (b * nq + i, 0)),
            pl.BlockSpec((S, c_kv), lambda b, i: (b, c_q // c_kv)),
            pl.BlockSpec((S, c_kv), lambda b, i: (b, c_q // c_kv + 1)),
            pl.BlockSpec((tq, c_iq), lambda b, i: (b * nq + i, (c_q + 2 * c_kv) // c_iq)),
            pl.BlockSpec((tq, LANES), lambda b, i: (b * nq + i, 0)),
            pl.BlockSpec((S, LANES), lambda b, i: (b, 0)),
        ],
        out_specs=pl.BlockSpec((tq, c_q), lambda b, i: (b * nq + i, 0)),
        out_shape=jax.ShapeDtypeStruct((T, c_q), BF16),
        scratch_shapes=[pltpu.VMEM((S, IDX_DIM), BF16), pltpu.VMEM((S // tkc, tq, tkc), jnp.int32),
                        pltpu.VMEM((G, S // tkc, R * tq, tkc), F32),
                        pltpu.VMEM((G, R * tq, tkc), F32), pltpu.VMEM((G, R * tq, tkc), F32),
                        pltpu.VMEM((G, R * tq, d), F32)],
        compiler_params=_params("parallel", "arbitrary"),
        name="dsa_attn",
    )(main, main, main, main, tail, tail)


def kernel(x, p, ffn1_w_in, ffn1_w_out, ffn2_w_in, ffn2_w_out, ln_g, ln_b, ple_w_gate, ple_w_proj,
           a_w_in, a_w_out, a_lam_q1, a_lam_k1, a_lam_q2, a_lam_k2, a_subln_g, b_w_in, b_w_out,
           c_w_in, c_w_out):
    B, S, D = x.shape
    depth = p.shape[0]
    T = B * S
    alpha = (2.0 * depth) ** 0.25
    bf = lambda w: w.astype(BF16)
    x = x.reshape(T, D)
    for i in range(depth):
        m, j = i % N_MIXERS, i // N_MIXERS
        g = lambda s: ln_g[i, s].reshape(1, D)
        b = lambda s: ln_b[i, s].reshape(1, D)
        x = _ffn(x, bf(ffn1_w_in[i]), bf(ffn1_w_out[i]), g(0), b(0), alpha)
        if m == 0:
            d = A_HEAD_DIM
            H = D // (2 * d)
            lambda_init = 0.8 - 0.6 * math.exp(-0.3 * i)
            qkv = _proj(x, bf(a_w_in[j]), BF16)
            lam_p = jnp.stack([a_lam_q1[j], a_lam_k1[j], a_lam_q2[j], a_lam_k2[j]]).astype(F32)
            o = _diff_attn(qkv, lam_p, a_subln_g[j].reshape(1, 2 * d).astype(F32), B, S, H, d, lambda_init)
            w_out = a_w_out[j]
        elif m == 1:
            qkv = _proj(x, bf(b_w_in[j]), BF16)
            o = _moba_attn(qkv, B, S, B_HEADS, B_HEAD_DIM, MOBA_BLOCK)
            w_out = b_w_out[j]
        else:
            n_main = C_HEADS * C_HEAD_DIM + 2 * C_KV_HEADS * C_HEAD_DIM + IDX_HEADS * IDX_DIM
            w_main = bf(c_w_in[j][:, :n_main])
            w_tail = c_w_in[j][:, n_main:]
            w_tail = bf(jnp.pad(w_tail, ((0, 0), (0, LANES - w_tail.shape[1]))))
            main = _proj(x, w_main, BF16)
            tail = _proj(x, w_tail, F32)
            o = _dsa_attn(main, tail, B, S)
            w_out = c_w_out[j]
        x = _out_ln(x, o, bf(w_out), g(1), b(1), alpha)
        x = _ffn(x, bf(ffn2_w_in[i]), bf(ffn2_w_out[i]), g(2), b(2), alpha)
        x = _ple(x, p[i].reshape(T, -1), bf(ple_w_gate[i]), bf(ple_w_proj[i]), g(3), b(3), alpha)
    return x.reshape(B, S, D)
```

```python
import functools
import math

import numpy as np
import jax
import jax.numpy as jnp
from jax import lax
from jax.experimental import pallas as pl
from jax.experimental.pallas import tpu as pltpu

F32 = jnp.float32
BF16 = jnp.bfloat16

N_MIXERS = 3
A_HEAD_DIM = 128
B_HEADS = 16
B_HEAD_DIM = 128
MOBA_BLOCK = 256
MOBA_TOPK = 3
C_HEADS = 16
C_KV_HEADS = 4
C_HEAD_DIM = 128
IDX_HEADS = 16
IDX_DIM = 64
DSA_TOPK_MAX = 256
LN_EPS = 1e-5
RMS_EPS = 1e-6

LANES = 128
NEG = -1e30
INT_MIN = -(2 ** 31)
VMEM_LIMIT = 56 * 1024 * 1024


def _alibi_slopes(n):
    return 2.0 ** (-8.0 * np.arange(1, n + 1, dtype=np.float32) / n)


def _params(*sem):
    return pltpu.CompilerParams(dimension_semantics=sem, vmem_limit_bytes=VMEM_LIMIT)


def _dot(a, b):
    return jnp.dot(a, b, preferred_element_type=F32)


def _dot_nt(a, b):
    return lax.dot_general(a, b, (((1,), (1,)), ((), ())), preferred_element_type=F32)


def _layer_norm(z, g, b):
    mu = jnp.mean(z, -1, keepdims=True)
    zc = z - mu
    var = jnp.mean(zc * zc, -1, keepdims=True)
    return zc * lax.rsqrt(var + LN_EPS) * g + b


def _ffn_kernel(x_ref, wg_ref, wu_ref, wo_ref, g_ref, b_ref, o_ref, xb_ref, acc_ref, *, alpha):
    j = pl.program_id(1)

    @pl.when(j == 0)
    def _():
        xb_ref[...] = x_ref[...].astype(BF16)
        acc_ref[...] = jnp.zeros_like(acc_ref)

    xb = xb_ref[...]
    gate = _dot(xb, wg_ref[...])
    up = _dot(xb, wu_ref[...])
    h = (gate * jax.nn.sigmoid(gate) * up).astype(BF16)
    acc_ref[...] += _dot(h, wo_ref[...])

    @pl.when(j == pl.num_programs(1) - 1)
    def _():
        z = alpha * x_ref[...] + 0.5 * acc_ref[...]
        o_ref[...] = _layer_norm(z, g_ref[...], b_ref[...])


def _ln_specs(ln, D, ngrid):
    if ngrid == 1:
        return pl.BlockSpec((None, 1, D), lambda i: (ln, 0, 0))
    return pl.BlockSpec((None, 1, D), lambda i, j: (ln, 0, 0))


def _ffn(x, w_in, w_out, layer, ln_g, ln_b, ln, alpha, tm=512, tf=512):
    T, D = x.shape
    F = w_out.shape[1]
    nf = F // tf
    assert T % tm == 0 and F % tf == 0 and w_in.shape[1:] == (D, 2 * F)
    return pl.pallas_call(
        functools.partial(_ffn_kernel, alpha=alpha),
        grid=(T // tm, nf),
        in_specs=[
            pl.BlockSpec((tm, D), lambda i, j: (i, 0)),
            pl.BlockSpec((None, D, tf), lambda i, j: (layer, 0, j)),
            pl.BlockSpec((None, D, tf), lambda i, j: (layer, 0, j + nf)),
            pl.BlockSpec((None, tf, D), lambda i, j: (layer, j, 0)),
            _ln_specs(ln, D, 2),
            _ln_specs(ln, D, 2),
        ],
        out_specs=pl.BlockSpec((tm, D), lambda i, j: (i, 0)),
        out_shape=jax.ShapeDtypeStruct((T, D), F32),
        scratch_shapes=[pltpu.VMEM((tm, D), BF16), pltpu.VMEM((tm, D), F32)],
        compiler_params=_params("parallel", "arbitrary"),
        name="ffn",
    )(x, w_in, w_in, w_out, ln_g, ln_b)


def _proj_kernel(x_ref, w_ref, o_ref, xb_ref):
    @pl.when(pl.program_id(1) == 0)
    def _():
        xb_ref[...] = x_ref[...].astype(BF16)

    o_ref[...] = _dot(xb_ref[...], w_ref[...]).astype(o_ref.dtype)


def _proj(x, w, layer, n_cols, out_dtype, tm=1024, tn=1024):
    T, D = x.shape
    tm, tn = min(tm, T), min(tn, n_cols)
    assert T % tm == 0 and n_cols % tn == 0 and w.shape[1] == D and w.shape[2] >= n_cols
    return pl.pallas_call(
        _proj_kernel,
        grid=(T // tm, n_cols // tn),
        in_specs=[
            pl.BlockSpec((tm, D), lambda i, j: (i, 0)),
            pl.BlockSpec((None, D, tn), lambda i, j: (layer, 0, j)),
        ],
        out_specs=pl.BlockSpec((tm, tn), lambda i, j: (i, j)),
        out_shape=jax.ShapeDtypeStruct((T, n_cols), out_dtype),
        scratch_shapes=[pltpu.VMEM((tm, D), BF16)],
        compiler_params=_params("parallel", "arbitrary"),
        name="proj",
    )(x, w)


ROW_SPLIT = 128


def _out_ln_kernel(x_ref, o_ref, w_ref, g_ref, b_ref, y_ref, *, alpha):
    for r in range(0, x_ref.shape[0], ROW_SPLIT):
        rows = slice(r, r + ROW_SPLIT)
        z = alpha * x_ref[rows, :] + _dot(o_ref[rows, :], w_ref[...])
        y_ref[rows, :] = _layer_norm(z, g_ref[...], b_ref[...])


def _out_ln(x, o, w, layer, ln_g, ln_b, ln, alpha, tm=512):
    T, D = x.shape
    K = o.shape[1]
    assert T % tm == 0 and tm % ROW_SPLIT == 0 and w.shape[1:] == (K, D)
    return pl.pallas_call(
        functools.partial(_out_ln_kernel, alpha=alpha),
        grid=(T // tm,),
        in_specs=[
            pl.BlockSpec((tm, D), lambda i: (i, 0)),
            pl.BlockSpec((tm, K), lambda i: (i, 0)),
            pl.BlockSpec((None, K, D), lambda i: (layer, 0, 0)),
            _ln_specs(ln, D, 1),
            _ln_specs(ln, D, 1),
        ],
        out_specs=pl.BlockSpec((tm, D), lambda i: (i, 0)),
        out_shape=jax.ShapeDtypeStruct((T, D), F32),
        compiler_params=_params("parallel"),
        name="out_ln",
    )(x, o, w, ln_g, ln_b)


def _ple_kernel(x_ref, p_ref, wg_ref, wp_ref, g_ref, b_ref, y_ref, *, alpha):
    for r in range(0, x_ref.shape[0], ROW_SPLIT):
        rows = slice(r, r + ROW_SPLIT)
        x = x_ref[rows, :]
        gate = jax.nn.sigmoid(_dot(x.astype(BF16), wg_ref[...]))
        emb = _dot(p_ref[rows, :].astype(BF16), wp_ref[...])
        y_ref[rows, :] = _layer_norm(alpha * x + gate * emb, g_ref[...], b_ref[...])


def _ple(x, p, wg, wp, layer, ln_g, ln_b, ln, alpha, tm=512):
    T, D = x.shape
    P = p.shape[2]
    assert T % tm == 0 and tm % ROW_SPLIT == 0
    return pl.pallas_call(
        functools.partial(_ple_kernel, alpha=alpha),
        grid=(T // tm,),
        in_specs=[
            pl.BlockSpec((tm, D), lambda i: (i, 0)),
            pl.BlockSpec((None, tm, P), lambda i: (layer, i, 0)),
            pl.BlockSpec((None, D, D), lambda i: (layer, 0, 0)),
            pl.BlockSpec((None, P, D), lambda i: (layer, 0, 0)),
            _ln_specs(ln, D, 1),
            _ln_specs(ln, D, 1),
        ],
        out_specs=pl.BlockSpec((tm, D), lambda i: (i, 0)),
        out_shape=jax.ShapeDtypeStruct((T, D), F32),
        compiler_params=_params("parallel"),
        name="ple",
    )(x, p, wg, wp, ln_g, ln_b)


def _row_max_bcast(mx_ref):
    m = jnp.max(mx_ref[...], -1, keepdims=True)
    mx_ref[...] = jnp.broadcast_to(m, mx_ref.shape)


def _causal_sweep(fn, qi):
    def body(j, carry):
        fn(2 * j, 2, False)
        return carry

    lax.fori_loop(0, qi // 2, body, 0)

    @pl.when(qi % 2 == 1)
    def _():
        fn(qi - 1, 2, True)

    @pl.when(qi % 2 == 0)
    def _():
        fn(qi, 1, True)


def _row_minus_col(rows, cols):
    r = lax.broadcasted_iota(jnp.int32, (rows, cols), 0)
    c = lax.broadcasted_iota(jnp.int32, (rows, cols), 1)
    return r - c


def _diff_attn_kernel(slopes_ref, lam_ref, subg_ref, q_ref, k_ref, v_ref, o_ref,
                      s_ref, bias_ref, mx_ref, ls_ref, acc_ref, *, d, scale, lambda_init):
    h = pl.program_id(1)
    qi = pl.program_id(2)
    tq = q_ref.shape[0]
    slope = slopes_ref[h]
    q = q_ref[...]
    mx_ref[...] = jnp.full_like(mx_ref, NEG)
    ls_ref[...] = jnp.zeros_like(ls_ref)
    acc_ref[...] = jnp.zeros_like(acc_ref)

    bias_ref[...] = -slope * _row_minus_col(tq, 2 * tq).astype(F32)

    def span_scores(c0, width, diagonal):
        rows = pl.ds(pl.multiple_of(c0 * tq, tq), width * tq)
        bias = bias_ref[:, :width * tq] - slope * ((qi - c0) * tq).astype(F32)
        for mm in range(2):
            s = _dot_nt(q[:, mm * d:(mm + 1) * d], k_ref[rows, mm * d:(mm + 1) * d]) * scale + bias
            if diagonal:
                s = jnp.where(_row_minus_col(tq, width * tq) + (qi - c0) * tq >= 0, s, NEG)
            mx = mx_ref[mm]
            for w in range(width):
                sw = s[:, w * tq:(w + 1) * tq]
                s_ref[mm, c0 + w] = sw
                mx = jnp.maximum(mx, sw)
            mx_ref[mm] = mx

    def span_accumulate(c0, width, diagonal):
        del diagonal
        v = v_ref[pl.ds(pl.multiple_of(c0 * tq, tq), width * tq), :]
        for mm in range(2):
            ps = [jnp.exp(s_ref[mm, c0 + w] - mx_ref[mm]) for w in range(width)]
            ls_ref[mm] += functools.reduce(lambda a, b: a + b, ps)
            p = ps[0] if width == 1 else jnp.concatenate(ps, axis=1)
            acc_ref[mm] += _dot(p.astype(BF16), v)

    _causal_sweep(span_scores, qi)
    for mm in range(2):
        _row_max_bcast(mx_ref.at[mm])
    _causal_sweep(span_accumulate, qi)

    lam_p = lam_ref[...]
    lam = (jnp.exp(jnp.sum(lam_p[0:1] * lam_p[1:2], -1, keepdims=True))
           - jnp.exp(jnp.sum(lam_p[2:3] * lam_p[3:4], -1, keepdims=True)) + lambda_init)
    l0 = jnp.sum(ls_ref[0], -1, keepdims=True)
    l1 = jnp.sum(ls_ref[1], -1, keepdims=True)
    o = acc_ref[0] / l0 - lam * (acc_ref[1] / l1)
    o = o * lax.rsqrt(jnp.mean(o * o, -1, keepdims=True) + RMS_EPS) * subg_ref[...] * (1.0 - lambda_init)
    o_ref[...] = o.astype(o_ref.dtype)


def _diff_attn(qkv, lam_p, subln_g, B, S, H, d, lambda_init, tq=256):
    T = B * S
    nq = S // tq
    assert S % tq == 0 and qkv.shape == (T, 3 * H * 2 * d)
    slopes = jnp.asarray(_alibi_slopes(H), F32)
    return pl.pallas_call(
        functools.partial(_diff_attn_kernel, d=d, scale=d ** -0.5, lambda_init=lambda_init),
        grid=(B, H, nq),
        in_specs=[
            pl.BlockSpec(memory_space=pltpu.SMEM),
            pl.BlockSpec((4, d), lambda b, h, i: (0, 0)),
            pl.BlockSpec((1, 2 * d), lambda b, h, i: (0, 0)),
            pl.BlockSpec((tq, 2 * d), lambda b, h, i: (b * nq + i, h)),
            pl.BlockSpec((S, 2 * d), lambda b, h, i: (b, H + h)),
            pl.BlockSpec((S, 2 * d), lambda b, h, i: (b, 2 * H + h)),
        ],
        out_specs=pl.BlockSpec((tq, 2 * d), lambda b, h, i: (b * nq + i, h)),
        out_shape=jax.ShapeDtypeStruct((T, H * 2 * d), BF16),
        scratch_shapes=[pltpu.VMEM((2, nq, tq, tq), F32), pltpu.VMEM((tq, 2 * tq), F32),
                        pltpu.VMEM((2, tq, tq), F32),
                        pltpu.VMEM((2, tq, tq), F32), pltpu.VMEM((2, tq, 2 * d), F32)],
        compiler_params=_params("parallel", "parallel", "arbitrary"),
        name="diff_attn",
    )(slopes, lam_p, subln_g, qkv, qkv, qkv)


def _moba_kernel(slopes_ref, q_ref, k_ref, v_ref, o_ref, km_ref, mb_ref, s_ref, bias_ref,
                 mx_ref, ls_ref, acc_ref, *, nb, n_sel, scale):
    h = pl.program_id(1)
    own = pl.program_id(2)
    bs = q_ref.shape[0]
    slope = slopes_ref[h]

    @pl.when(own == 0)
    def _():
        km_ref[...] = jnp.zeros_like(km_ref)
        for n in range(nb):
            km_ref[n:n + 1, :] = jnp.mean(k_ref[n * bs:(n + 1) * bs, :].astype(F32), 0, keepdims=True)

    q = q_ref[...]
    km = km_ref[...]
    km_hi = km.astype(BF16)
    km_lo = (km - km_hi.astype(F32)).astype(BF16)
    gate = _dot_nt(q, km_hi) + _dot_nt(q, km_lo)
    lane = lax.broadcasted_iota(jnp.int32, gate.shape, 1)
    rank = jnp.zeros(gate.shape, jnp.int32)
    for m in range(nb - 1):
        gm = gate[:, m:m + 1]
        beats = (gm > gate) | ((gm == gate) & (lane > m))
        rank = rank + beats.astype(jnp.int32) * (own > m).astype(jnp.int32)
    sel = (rank < n_sel) & (lane < own)

    for n in range(nb - 1):
        @pl.when(n < own)
        def _(n=n):
            off = -slope * ((own - n) * bs).astype(F32)
            mb_ref[n] = jnp.broadcast_to(jnp.where(sel[:, n:n + 1], off, NEG), (bs, bs))

    rel_i = _row_minus_col(bs, bs)
    bias_ref[...] = -slope * rel_i.astype(F32)
    mx_ref[...] = jnp.full_like(mx_ref, NEG)
    ls_ref[...] = jnp.zeros_like(ls_ref)
    acc_ref[...] = jnp.zeros_like(acc_ref)

    def span_scores(c0, width, diagonal):
        rows = pl.ds(pl.multiple_of(c0 * bs, bs), width * bs)
        qk = _dot_nt(q, k_ref[rows, :]) * scale
        mx = mx_ref[...]
        for w in range(width):
            sw = qk[:, w * bs:(w + 1) * bs] + bias_ref[...]
            if diagonal and w == width - 1:
                sw = jnp.where(rel_i >= 0, sw, NEG)
            else:
                sw = sw + mb_ref[c0 + w]
            s_ref[c0 + w] = sw
            mx = jnp.maximum(mx, sw)
        mx_ref[...] = mx

    def span_accumulate(c0, width, diagonal):
        del diagonal
        v = v_ref[pl.ds(pl.multiple_of(c0 * bs, bs), width * bs), :]
        ps = [jnp.exp(s_ref[c0 + w] - mx_ref[...]) for w in range(width)]
        ls_ref[...] += functools.reduce(lambda a, b: a + b, ps)
        p = ps[0] if width == 1 else jnp.concatenate(ps, axis=1)
        acc_ref[...] += _dot(p.astype(BF16), v)

    _causal_sweep(span_scores, own)
    _row_max_bcast(mx_ref)
    _causal_sweep(span_accumulate, own)

    l = jnp.sum(ls_ref[...], -1, keepdims=True)
    o_ref[...] = (acc_ref[...] / l).astype(o_ref.dtype)


def _moba_attn(qkv, B, S, H, d, bs):
    T = B * S
    nb = S // bs
    assert S % bs == 0 and nb <= LANES and qkv.shape == (T, 3 * H * d)
    n_sel = min(MOBA_TOPK, nb - 1)
    slopes = jnp.asarray(_alibi_slopes(H), F32)
    return pl.pallas_call(
        functools.partial(_moba_kernel, nb=nb, n_sel=n_sel, scale=d ** -0.5),
        grid=(B, H, nb),
        in_specs=[
            pl.BlockSpec(memory_space=pltpu.SMEM),
            pl.BlockSpec((bs, d), lambda b, h, i: (b * nb + i, h)),
            pl.BlockSpec((S, d), lambda b, h, i: (b, H + h)),
            pl.BlockSpec((S, d), lambda b, h, i: (b, 2 * H + h)),
        ],
        out_specs=pl.BlockSpec((bs, d), lambda b, h, i: (b * nb + i, h)),
        out_shape=jax.ShapeDtypeStruct((T, H * d), BF16),
        scratch_shapes=[pltpu.VMEM((LANES, d), F32), pltpu.VMEM((nb, bs, bs), F32),
                        pltpu.VMEM((nb, bs, bs), F32), pltpu.VMEM((bs, bs), F32),
                        pltpu.VMEM((bs, bs), F32), pltpu.VMEM((bs, bs), F32), pltpu.VMEM((bs, d), F32)],
        compiler_params=_params("parallel", "parallel", "arbitrary"),
        name="moba_attn",
    )(slopes, qkv, qkv, qkv)


def _dsa_kernel(q_ref, k_ref, v_ref, qi_ref, tailq_ref, tailk_ref, o_ref,
                kib_ref, qis_ref, key_ref, thr_ref, s_ref, mx_ref, ls_ref, acc_ref, *, n_keep, tkc, scale):
    i = pl.program_id(1)
    tq = q_ref.shape[0]
    G, R, d = C_KV_HEADS, C_HEADS // C_KV_HEADS, C_HEAD_DIM
    qstart = i * tq
    nkc = (qstart + tq + tkc - 1) // tkc

    @pl.when(i == 0)
    def _():
        kib_ref[...] = tailk_ref[:, 0:IDX_DIM].astype(BF16)

    w_t = (jnp.transpose(tailq_ref[...])[IDX_DIM:IDX_DIM + IDX_HEADS, :]
           * (IDX_HEADS ** -0.5 * IDX_DIM ** -0.5))
    key_minus_query = _row_minus_col(tkc, tq)

    for hh in range(IDX_HEADS):
        qis_ref[hh * tq:(hh + 1) * tq, :] = qi_ref[:, hh * IDX_DIM:(hh + 1) * IDX_DIM]

    def score_chunk(c, carry):
        kc = kib_ref[pl.ds(pl.multiple_of(c * tkc, tkc), tkc), :]
        dots = _dot_nt(kc, qis_ref[...])
        acc = jnp.zeros((tkc, tq), F32)
        for hh in range(IDX_HEADS):
            acc = acc + w_t[hh:hh + 1, :] * jnp.maximum(dots[:, hh * tq:(hh + 1) * tq], 0.0)
        bits = lax.bitcast_convert_type(acc, jnp.int32)
        key = bits ^ ((bits >> 31) & jnp.int32(0x7FFFFFFF))
        causal = key_minus_query + (c * tkc - qstart) <= 0
        key_ref[c] = jnp.where(causal, key, jnp.int32(INT_MIN))
        return carry

    lax.fori_loop(0, nkc, score_chunk, 0)

    def count(pred):
        def body(c, part):
            hit = pred(key_ref[c]).astype(jnp.int32)
            return part + jnp.sum(hit.reshape(tkc // 8, 8, tq), axis=0)
        part = lax.fori_loop(0, nkc, body, jnp.zeros((8, tq), jnp.int32))
        return jnp.sum(part, axis=0, keepdims=True)

    thr_ref[...] = jnp.full_like(thr_ref, INT_MIN + 1)

    @pl.when(qstart + tq > n_keep)
    def _():
        zero = jnp.zeros((1, tq), jnp.int32)
        thr0 = jnp.where(count(lambda k: k >= zero) >= n_keep, jnp.int32(0), jnp.int32(INT_MIN))

        def bit_step(b, thr):
            cand = thr | jnp.left_shift(jnp.int32(1), 30 - b)
            return jnp.where(count(lambda k: k >= cand) >= n_keep, cand, thr)

        thr = jnp.maximum(lax.fori_loop(0, 31, bit_step, thr0), jnp.int32(INT_MIN + 1))
        thr_ref[...] = thr

        @pl.when(jnp.max(count(lambda k: k >= thr)) > n_keep)
        def _():
            room = (n_keep - count(lambda k: k > thr)).astype(F32)
            lower_tri = jnp.where(_row_minus_col(tkc, tkc) >= 0, 1.0, 0.0).astype(BF16)

            def drop_excess(c, seen):
                key = key_ref[c]
                tie = key == thr
                tie_f = jnp.where(tie, 1.0, 0.0)
                rank = seen + _dot(lower_tri, tie_f.astype(BF16))
                key_ref[c] = jnp.where(tie & (rank > room), jnp.int32(INT_MIN), key)
                return seen + jnp.sum(tie_f, axis=0, keepdims=True)

            lax.fori_loop(0, nkc, drop_excess, jnp.zeros((1, tq), F32))

    thr = thr_ref[...]
    mx_ref[...] = jnp.full_like(mx_ref, NEG)
    ls_ref[...] = jnp.zeros_like(ls_ref)
    acc_ref[...] = jnp.zeros_like(acc_ref)
    slopes = _alibi_slopes(C_HEADS).reshape(G, R)
    rel = _row_minus_col(tq, tkc).astype(F32)

    def score_pass(c, carry):
        rows = pl.ds(pl.multiple_of(c * tkc, tkc), tkc)
        mask = jnp.transpose(jnp.where(key_ref[c] >= thr, 0.0, NEG))
        dist = rel + (qstart - c * tkc).astype(F32)
        for g in range(G):
            qs = jnp.concatenate(
                [q_ref[:, (g * R + r) * d:(g * R + r + 1) * d] for r in range(R)], axis=0)
            bias = jnp.concatenate([mask - float(slopes[g, r]) * dist for r in range(R)], axis=0)
            s = _dot_nt(qs, k_ref[rows, g * d:(g + 1) * d]) * scale + bias
            s_ref[g, c] = s
            mx_ref[g] = jnp.maximum(mx_ref[g], s)
        return carry

    def acc_pass(c, carry):
        rows = pl.ds(pl.multiple_of(c * tkc, tkc), tkc)
        for g in range(G):
            p = jnp.exp(s_ref[g, c] - mx_ref[g])
            ls_ref[g] += p
            acc_ref[g] += _dot(p.astype(BF16), v_ref[rows, g * d:(g + 1) * d])
        return carry

    lax.fori_loop(0, nkc, score_pass, 0)
    for g in range(G):
        _row_max_bcast(mx_ref.at[g])
    lax.fori_loop(0, nkc, acc_pass, 0)

    for g in range(G):
        o = acc_ref[g] / jnp.sum(ls_ref[g], -1, keepdims=True)
        for r in range(R):
            o_ref[:, (g * R + r) * d:(g * R + r + 1) * d] = o[r * tq:(r + 1) * tq].astype(o_ref.dtype)


def _dsa_attn(main, tail, B, S, tq=128, tkc=256):
    T = B * S
    nq = S // tq
    G, H, d = C_KV_HEADS, C_HEADS, C_HEAD_DIM
    R = H // G
    c_q, c_kv, c_iq = H * d, G * d, IDX_HEADS * IDX_DIM
    assert S % tq == 0 and S % tkc == 0 and tkc % tq == 0
    assert main.shape == (T, c_q + 2 * c_kv + c_iq) and tail.shape == (T, LANES)
    assert c_q % c_kv == 0 and (c_q + 2 * c_kv) % c_iq == 0
    n_keep = min(DSA_TOPK_MAX, S // 4)
    return pl.pallas_call(
        functools.partial(_dsa_kernel, n_keep=n_keep, tkc=tkc, scale=d ** -0.5),
        grid=(B, nq),
        in_specs=[
            pl.BlockSpec((tq, c_q), lambda b, i: (b * nq + i, 0)),
            pl.BlockSpec((S, c_kv), lambda b, i: (b, c_q // c_kv)),
            pl.BlockSpec((S, c_kv), lambda b, i: (b, c_q // c_kv + 1)),
            pl.BlockSpec((tq, c_iq), lambda b, i: (b * nq + i, (c_q + 2 * c_kv) // c_iq)),
            pl.BlockSpec((tq, LANES), lambda b, i: (b * nq + i, 0)),
            pl.BlockSpec((S, LANES), lambda b, i: (b, 0)),
        ],
        out_specs=pl.BlockSpec((tq, c_q), lambda b, i: (b * nq + i, 0)),
        out_shape=jax.ShapeDtypeStruct((T, c_q), BF16),
        scratch_shapes=[pltpu.VMEM((S, IDX_DIM), BF16), pltpu.VMEM((IDX_HEADS * tq, IDX_DIM), BF16),
                        pltpu.VMEM((S // tkc, tkc, tq), jnp.int32),
                        pltpu.VMEM((1, tq), jnp.int32),
                        pltpu.VMEM((G, S // tkc, R * tq, tkc), F32),
                        pltpu.VMEM((G, R * tq, tkc), F32), pltpu.VMEM((G, R * tq, tkc), F32),
                        pltpu.VMEM((G, R * tq, d), F32)],
        compiler_params=_params("parallel", "arbitrary"),
        name="dsa_attn",
    )(main, main, main, main, tail, tail)


def kernel(x, p, ffn1_w_in, ffn1_w_out, ffn2_w_in, ffn2_w_out, ln_g, ln_b, ple_w_gate, ple_w_proj,
           a_w_in, a_w_out, a_lam_q1, a_lam_k1, a_lam_q2, a_lam_k2, a_subln_g, b_w_in, b_w_out,
           c_w_in, c_w_out):
    B, S, D = x.shape
    depth = p.shape[0]
    T = B * S
    alpha = (2.0 * depth) ** 0.25
    bf = lambda w: w.astype(BF16)
    ffn1_in, ffn1_out, ffn2_in, ffn2_out = bf(ffn1_w_in), bf(ffn1_w_out), bf(ffn2_w_in), bf(ffn2_w_out)
    ple_g, ple_p = bf(ple_w_gate), bf(ple_w_proj)
    a_in, a_out, b_in, b_out, c_out = bf(a_w_in), bf(a_w_out), bf(b_w_in), bf(b_w_out), bf(c_w_out)
    n_main = C_HEADS * C_HEAD_DIM + 2 * C_KV_HEADS * C_HEAD_DIM + IDX_HEADS * IDX_DIM
    c_in = bf(c_w_in)
    c_tail = c_w_in[:, :, n_main:]
    c_tail = bf(jnp.pad(c_tail, ((0, 0), (0, 0), (0, LANES - c_tail.shape[2]))))
    lng = ln_g.reshape(-1, 1, D).astype(F32)
    lnb = ln_b.reshape(-1, 1, D).astype(F32)
    n_ln = ln_g.shape[1]
    p = p.reshape(depth, T, -1)
    x = x.reshape(T, D)
    for i in range(depth):
        m, j = i % N_MIXERS, i // N_MIXERS
        x = _ffn(x, ffn1_in, ffn1_out, i, lng, lnb, i * n_ln, alpha)
        if m == 0:
            d = A_HEAD_DIM
            H = D // (2 * d)
            lambda_init = 0.8 - 0.6 * math.exp(-0.3 * i)
            qkv = _proj(x, a_in, j, a_in.shape[2], BF16)
            lam_p = jnp.stack([a_lam_q1[j], a_lam_k1[j], a_lam_q2[j], a_lam_k2[j]]).astype(F32)
            o = _diff_attn(qkv, lam_p, a_subln_g[j].reshape(1, 2 * d).astype(F32), B, S, H, d, lambda_init)
            w_out = a_out
        elif m == 1:
            qkv = _proj(x, b_in, j, b_in.shape[2], BF16)
            o = _moba_attn(qkv, B, S, B_HEADS, B_HEAD_DIM, MOBA_BLOCK)
            w_out = b_out
        else:
            main = _proj(x, c_in, j, n_main, BF16)
            tail = _proj(x, c_tail, j, LANES, F32)
            o = _dsa_attn(main, tail, B, S)
            w_out = c_out
        x = _out_ln(x, o, w_out, j, lng, lnb, i * n_ln + 1, alpha)
        x = _ffn(x, ffn2_in, ffn2_out, i, lng, lnb, i * n_ln + 2, alpha)
        x = _ple(x, p, ple_g, ple_p, i, lng, lnb, i * n_ln + 3, alpha)
    return x.reshape(B, S, D)
```

```python
import functools
import math

import numpy as np
import jax
import jax.numpy as jnp
from jax import lax
from jax.experimental import pallas as pl
from jax.experimental.pallas import tpu as pltpu

F32 = jnp.float32
BF16 = jnp.bfloat16

N_MIXERS = 3
A_HEAD_DIM = 128
B_HEADS = 16
B_HEAD_DIM = 128
MOBA_BLOCK = 256
MOBA_TOPK = 3
C_HEADS = 16
C_KV_HEADS = 4
C_HEAD_DIM = 128
IDX_HEADS = 16
IDX_DIM = 64
DSA_TOPK_MAX = 256
LN_EPS = 1e-5
RMS_EPS = 1e-6

LANES = 128
NEG = -1e30
INT_MIN = -(2 ** 31)
VMEM_LIMIT = 56 * 1024 * 1024


def _alibi_slopes(n):
    return 2.0 ** (-8.0 * np.arange(1, n + 1, dtype=np.float32) / n)


def _params(*sem):
    return pltpu.CompilerParams(dimension_semantics=sem, vmem_limit_bytes=VMEM_LIMIT)


def _dot(a, b):
    return jnp.dot(a, b, preferred_element_type=F32)


def _dot_nt(a, b):
    return lax.dot_general(a, b, (((1,), (1,)), ((), ())), preferred_element_type=F32)


def _layer_norm(z, g, b):
    mu = jnp.mean(z, -1, keepdims=True)
    zc = z - mu
    var = jnp.mean(zc * zc, -1, keepdims=True)
    return zc * lax.rsqrt(var + LN_EPS) * g + b


def _ffn_kernel(x_ref, wg_ref, wu_ref, wo_ref, g_ref, b_ref, o_ref, xb_ref, acc_ref, *, alpha):
    j = pl.program_id(1)

    @pl.when(j == 0)
    def _():
        xb_ref[...] = x_ref[...].astype(BF16)
        acc_ref[...] = jnp.zeros_like(acc_ref)

    xb = xb_ref[...]
    gate = _dot(xb, wg_ref[...])
    up = _dot(xb, wu_ref[...])
    h = (gate * jax.nn.sigmoid(gate) * up).astype(BF16)
    acc_ref[...] += _dot(h, wo_ref[...])

    @pl.when(j == pl.num_programs(1) - 1)
    def _():
        z = alpha * x_ref[...] + 0.5 * acc_ref[...]
        o_ref[...] = _layer_norm(z, g_ref[...], b_ref[...])


def _ln_specs(ln, D, ngrid):
    if ngrid == 1:
        return pl.BlockSpec((None, 1, D), lambda i: (ln, 0, 0))
    return pl.BlockSpec((None, 1, D), lambda i, j: (ln, 0, 0))


def _ffn(x, w_in, w_out, layer, ln_g, ln_b, ln, alpha, tm=512, tf=512):
    T, D = x.shape
    F = w_out.shape[1]
    nf = F // tf
    assert T % tm == 0 and F % tf == 0 and w_in.shape[1:] == (D, 2 * F)
    return pl.pallas_call(
        functools.partial(_ffn_kernel, alpha=alpha),
        grid=(T // tm, nf),
        in_specs=[
            pl.BlockSpec((tm, D), lambda i, j: (i, 0)),
            pl.BlockSpec((None, D, tf), lambda i, j: (layer, 0, j)),
            pl.BlockSpec((None, D, tf), lambda i, j: (layer, 0, j + nf)),
            pl.BlockSpec((None, tf, D), lambda i, j: (layer, j, 0)),
            _ln_specs(ln, D, 2),
            _ln_specs(ln, D, 2),
        ],
        out_specs=pl.BlockSpec((tm, D), lambda i, j: (i, 0)),
        out_shape=jax.ShapeDtypeStruct((T, D), F32),
        scratch_shapes=[pltpu.VMEM((tm, D), BF16), pltpu.VMEM((tm, D), F32)],
        compiler_params=_params("parallel", "arbitrary"),
        name="ffn",
    )(x, w_in, w_in, w_out, ln_g, ln_b)


def _proj_kernel(x_ref, w_ref, o_ref, xb_ref):
    @pl.when(pl.program_id(1) == 0)
    def _():
        xb_ref[...] = x_ref[...].astype(BF16)

    o_ref[...] = _dot(xb_ref[...], w_ref[...]).astype(o_ref.dtype)


def _proj(x, w, layer, n_cols, out_dtype, tm=1024, tn=1024):
    T, D = x.shape
    tm, tn = min(tm, T), min(tn, n_cols)
    assert T % tm == 0 and n_cols % tn == 0 and w.shape[1] == D and w.shape[2] >= n_cols
    return pl.pallas_call(
        _proj_kernel,
        grid=(T // tm, n_cols // tn),
        in_specs=[
            pl.BlockSpec((tm, D), lambda i, j: (i, 0)),
            pl.BlockSpec((None, D, tn), lambda i, j: (layer, 0, j)),
        ],
        out_specs=pl.BlockSpec((tm, tn), lambda i, j: (i, j)),
        out_shape=jax.ShapeDtypeStruct((T, n_cols), out_dtype),
        scratch_shapes=[pltpu.VMEM((tm, D), BF16)],
        compiler_params=_params("parallel", "arbitrary"),
        name="proj",
    )(x, w)


ROW_SPLIT = 128


def _out_ln_kernel(x_ref, o_ref, w_ref, g_ref, b_ref, y_ref, *, alpha):
    for r in range(0, x_ref.shape[0], ROW_SPLIT):
        rows = slice(r, r + ROW_SPLIT)
        z = alpha * x_ref[rows, :] + _dot(o_ref[rows, :], w_ref[...])
        y_ref[rows, :] = _layer_norm(z, g_ref[...], b_ref[...])


def _out_ln(x, o, w, layer, ln_g, ln_b, ln, alpha, tm=512):
    T, D = x.shape
    K = o.shape[1]
    assert T % tm == 0 and tm % ROW_SPLIT == 0 and w.shape[1:] == (K, D)
    return pl.pallas_call(
        functools.partial(_out_ln_kernel, alpha=alpha),
        grid=(T // tm,),
        in_specs=[
            pl.BlockSpec((tm, D), lambda i: (i, 0)),
            pl.BlockSpec((tm, K), lambda i: (i, 0)),
            pl.BlockSpec((None, K, D), lambda i: (layer, 0, 0)),
            _ln_specs(ln, D, 1),
            _ln_specs(ln, D, 1),
        ],
        out_specs=pl.BlockSpec((tm, D), lambda i: (i, 0)),
        out_shape=jax.ShapeDtypeStruct((T, D), F32),
        compiler_params=_params("parallel"),
        name="out_ln",
    )(x, o, w, ln_g, ln_b)


def _ple_kernel(x_ref, p_ref, wg_ref, wp_ref, g_ref, b_ref, y_ref, *, alpha):
    for r in range(0, x_ref.shape[0], ROW_SPLIT):
        rows = slice(r, r + ROW_SPLIT)
        x = x_ref[rows, :]
        gate = jax.nn.sigmoid(_dot(x.astype(BF16), wg_ref[...]))
        emb = _dot(p_ref[rows, :].astype(BF16), wp_ref[...])
        y_ref[rows, :] = _layer_norm(alpha * x + gate * emb, g_ref[...], b_ref[...])


def _ple(x, p, wg, wp, layer, ln_g, ln_b, ln, alpha, tm=512):
    T, D = x.shape
    P = p.shape[2]
    assert T % tm == 0 and tm % ROW_SPLIT == 0
    return pl.pallas_call(
        functools.partial(_ple_kernel, alpha=alpha),
        grid=(T // tm,),
        in_specs=[
            pl.BlockSpec((tm, D), lambda i: (i, 0)),
            pl.BlockSpec((None, tm, P), lambda i: (layer, i, 0)),
            pl.BlockSpec((None, D, D), lambda i: (layer, 0, 0)),
            pl.BlockSpec((None, P, D), lambda i: (layer, 0, 0)),
            _ln_specs(ln, D, 1),
            _ln_specs(ln, D, 1),
        ],
        out_specs=pl.BlockSpec((tm, D), lambda i: (i, 0)),
        out_shape=jax.ShapeDtypeStruct((T, D), F32),
        compiler_params=_params("parallel"),
        name="ple",
    )(x, p, wg, wp, ln_g, ln_b)


def _row_max_bcast(mx_ref):
    m = jnp.max(mx_ref[...], -1, keepdims=True)
    mx_ref[...] = jnp.broadcast_to(m, mx_ref.shape)


def _causal_sweep(fn, qi):
    def body(j, carry):
        fn(2 * j, 2, False)
        return carry

    lax.fori_loop(0, qi // 2, body, 0)

    @pl.when(qi % 2 == 1)
    def _():
        fn(qi - 1, 2, True)

    @pl.when(qi % 2 == 0)
    def _():
        fn(qi, 1, True)


def _row_minus_col(rows, cols):
    r = lax.broadcasted_iota(jnp.int32, (rows, cols), 0)
    c = lax.broadcasted_iota(jnp.int32, (rows, cols), 1)
    return r - c


DIFF_HEADS_PER_STEP = 2


def _diff_attn_kernel(slopes_ref, lam_ref, subg_ref, q_ref, k_ref, v_ref, o_ref,
                      s_ref, bias_ref, mx_ref, ls_ref, acc_ref, *, d, scale, lambda_init):
    hp = DIFF_HEADS_PER_STEP
    h0 = pl.program_id(1) * hp
    qi = pl.program_id(2)
    tq = q_ref.shape[0]
    units = range(2 * hp)
    slopes = [slopes_ref[h0 + hh] for hh in range(hp)]
    q = [q_ref[:, u * d:(u + 1) * d] for u in units]
    mx_ref[...] = jnp.full_like(mx_ref, NEG)
    ls_ref[...] = jnp.zeros_like(ls_ref)
    acc_ref[...] = jnp.zeros_like(acc_ref)

    rel = _row_minus_col(tq, 2 * tq).astype(F32)
    for hh in range(hp):
        bias_ref[hh] = -slopes[hh] * rel

    def span_scores(c0, width, diagonal):
        rows = pl.ds(pl.multiple_of(c0 * tq, tq), width * tq)
        for hh in range(hp):
            bias = bias_ref[hh, :, :width * tq] - slopes[hh] * ((qi - c0) * tq).astype(F32)
            for u in (2 * hh, 2 * hh + 1):
                s = _dot_nt(q[u], k_ref[rows, u * d:(u + 1) * d]) * scale + bias
                if diagonal:
                    s = jnp.where(_row_minus_col(tq, width * tq) + (qi - c0) * tq >= 0, s, NEG)
                mx = mx_ref[u]
                for w in range(width):
                    sw = s[:, w * tq:(w + 1) * tq]
                    s_ref[u, c0 + w] = sw
                    mx = jnp.maximum(mx, sw)
                mx_ref[u] = mx

    def span_accumulate(c0, width, diagonal):
        del diagonal
        rows = pl.ds(pl.multiple_of(c0 * tq, tq), width * tq)
        for u in units:
            hh = u // 2
            ps = [jnp.exp(s_ref[u, c0 + w] - mx_ref[u]) for w in range(width)]
            ls_ref[u] += functools.reduce(lambda a, b: a + b, ps)
            p = ps[0] if width == 1 else jnp.concatenate(ps, axis=1)
            acc_ref[u] += _dot(p.astype(BF16), v_ref[rows, hh * 2 * d:(hh + 1) * 2 * d])

    _causal_sweep(span_scores, qi)
    for u in units:
        _row_max_bcast(mx_ref.at[u])
    _causal_sweep(span_accumulate, qi)

    lam_p = lam_ref[...]
    lam = (jnp.exp(jnp.sum(lam_p[0:1] * lam_p[1:2], -1, keepdims=True))
           - jnp.exp(jnp.sum(lam_p[2:3] * lam_p[3:4], -1, keepdims=True)) + lambda_init)
    for hh in range(hp):
        l0 = jnp.sum(ls_ref[2 * hh], -1, keepdims=True)
        l1 = jnp.sum(ls_ref[2 * hh + 1], -1, keepdims=True)
        o = acc_ref[2 * hh] / l0 - lam * (acc_ref[2 * hh + 1] / l1)
        o = o * lax.rsqrt(jnp.mean(o * o, -1, keepdims=True) + RMS_EPS) * subg_ref[...] * (1.0 - lambda_init)
        o_ref[:, hh * 2 * d:(hh + 1) * 2 * d] = o.astype(o_ref.dtype)


def _diff_attn(qkv, lam_p, subln_g, B, S, H, d, lambda_init, tq=256):
    T = B * S
    nq = S // tq
    hp = DIFF_HEADS_PER_STEP
    w = hp * 2 * d
    assert S % tq == 0 and H % hp == 0 and qkv.shape == (T, 3 * H * 2 * d)
    slopes = jnp.asarray(_alibi_slopes(H), F32)
    return pl.pallas_call(
        functools.partial(_diff_attn_kernel, d=d, scale=d ** -0.5, lambda_init=lambda_init),
        grid=(B, H // hp, nq),
        in_specs=[
            pl.BlockSpec(memory_space=pltpu.SMEM),
            pl.BlockSpec((4, d), lambda b, h, i: (0, 0)),
            pl.BlockSpec((1, 2 * d), lambda b, h, i: (0, 0)),
            pl.BlockSpec((tq, w), lambda b, h, i: (b * nq + i, h)),
            pl.BlockSpec((S, w), lambda b, h, i: (b, H // hp + h)),
            pl.BlockSpec((S, w), lambda b, h, i: (b, 2 * (H // hp) + h)),
        ],
        out_specs=pl.BlockSpec((tq, w), lambda b, h, i: (b * nq + i, h)),
        out_shape=jax.ShapeDtypeStruct((T, H * 2 * d), BF16),
        scratch_shapes=[pltpu.VMEM((2 * hp, nq, tq, tq), F32), pltpu.VMEM((hp, tq, 2 * tq), F32),
                        pltpu.VMEM((2 * hp, tq, tq), F32), pltpu.VMEM((2 * hp, tq, tq), F32),
                        pltpu.VMEM((2 * hp, tq, 2 * d), F32)],
        compiler_params=_params("parallel", "parallel", "arbitrary"),
        name="diff_attn",
    )(slopes, lam_p, subln_g, qkv, qkv, qkv)


MOBA_HEADS_PER_STEP = 2


def _moba_kernel(slopes_ref, q_ref, k_ref, v_ref, o_ref, km_ref, mb_ref, s_ref, bias_ref,
                 mx_ref, ls_ref, acc_ref, *, nb, n_sel, d, scale):
    hp = MOBA_HEADS_PER_STEP
    h0 = pl.program_id(1) * hp
    own = pl.program_id(2)
    bs = q_ref.shape[0]
    nbp = -(-nb // 8) * 8
    heads = range(hp)
    cols = [slice(hh * d, (hh + 1) * d) for hh in heads]
    slopes = [slopes_ref[h0 + hh] for hh in heads]

    @pl.when(own == 0)
    def _():
        km_ref[...] = jnp.zeros_like(km_ref)
        for hh in heads:
            for n in range(nb):
                km_ref[hh, n:n + 1, :] = jnp.mean(
                    k_ref[n * bs:(n + 1) * bs, cols[hh]].astype(F32), 0, keepdims=True)

    q = [q_ref[:, cols[hh]] for hh in heads]
    rel_i = _row_minus_col(bs, bs)
    block = lax.broadcasted_iota(jnp.int32, (nbp, bs), 0)
    for hh in heads:
        km = km_ref[hh]
        km_hi = km.astype(BF16)
        km_lo = (km - km_hi.astype(F32)).astype(BF16)
        gate = (_dot_nt(km_hi, q[hh]) + _dot_nt(km_lo, q[hh]))[0:nbp, :]
        rank = jnp.zeros((nbp, bs), jnp.int32)
        for m in range(nb - 1):
            gm = gate[m:m + 1, :]
            beats = (gm > gate) | ((gm == gate) & (block > m))
            rank = rank + beats.astype(jnp.int32) * (own > m).astype(jnp.int32)
        sel = (rank < n_sel) & (block < own)
        off = -slopes[hh] * ((own - block) * bs).astype(F32)
        mb_t = jnp.where(sel, off, NEG)
        mb = jnp.transpose(jnp.concatenate([mb_t, jnp.zeros((LANES - nbp, bs), F32)], axis=0))
        for n in range(nb - 1):
            @pl.when(n < own)
            def _(n=n, hh=hh, mb=mb):
                mb_ref[hh, n] = jnp.broadcast_to(mb[:, n:n + 1], (bs, bs))
        bias_ref[hh] = -slopes[hh] * rel_i.astype(F32)

    mx_ref[...] = jnp.full_like(mx_ref, NEG)
    ls_ref[...] = jnp.zeros_like(ls_ref)
    acc_ref[...] = jnp.zeros_like(acc_ref)

    def span_scores(c0, width, diagonal):
        rows = pl.ds(pl.multiple_of(c0 * bs, bs), width * bs)
        for hh in heads:
            qk = _dot_nt(q[hh], k_ref[rows, cols[hh]]) * scale
            mx = mx_ref[hh]
            for w in range(width):
                sw = qk[:, w * bs:(w + 1) * bs] + bias_ref[hh]
                if diagonal and w == width - 1:
                    sw = jnp.where(rel_i >= 0, sw, NEG)
                else:
                    sw = sw + mb_ref[hh, c0 + w]
                s_ref[hh, c0 + w] = sw
                mx = jnp.maximum(mx, sw)
            mx_ref[hh] = mx

    def span_accumulate(c0, width, diagonal):
        del diagonal
        rows = pl.ds(pl.multiple_of(c0 * bs, bs), width * bs)
        for hh in heads:
            ps = [jnp.exp(s_ref[hh, c0 + w] - mx_ref[hh]) for w in range(width)]
            ls_ref[hh] += functools.reduce(lambda a, b: a + b, ps)
            p = ps[0] if width == 1 else jnp.concatenate(ps, axis=1)
            acc_ref[hh] += _dot(p.astype(BF16), v_ref[rows, cols[hh]])

    _causal_sweep(span_scores, own)
    for hh in heads:
        _row_max_bcast(mx_ref.at[hh])
    _causal_sweep(span_accumulate, own)

    for hh in heads:
        l = jnp.sum(ls_ref[hh], -1, keepdims=True)
        o_ref[:, cols[hh]] = (acc_ref[hh] / l).astype(o_ref.dtype)


def _moba_attn(qkv, B, S, H, d, bs):
    T = B * S
    nb = S // bs
    hp = MOBA_HEADS_PER_STEP
    assert S % bs == 0 and nb <= LANES and H % hp == 0 and qkv.shape == (T, 3 * H * d)
    n_sel = min(MOBA_TOPK, nb - 1)
    slopes = jnp.asarray(_alibi_slopes(H), F32)
    return pl.pallas_call(
        functools.partial(_moba_kernel, nb=nb, n_sel=n_sel, d=d, scale=d ** -0.5),
        grid=(B, H // hp, nb),
        in_specs=[
            pl.BlockSpec(memory_space=pltpu.SMEM),
            pl.BlockSpec((bs, hp * d), lambda b, h, i: (b * nb + i, h)),
            pl.BlockSpec((S, hp * d), lambda b, h, i: (b, H // hp + h)),
            pl.BlockSpec((S, hp * d), lambda b, h, i: (b, 2 * (H // hp) + h)),
        ],
        out_specs=pl.BlockSpec((bs, hp * d), lambda b, h, i: (b * nb + i, h)),
        out_shape=jax.ShapeDtypeStruct((T, H * d), BF16),
        scratch_shapes=[pltpu.VMEM((hp, LANES, d), F32), pltpu.VMEM((hp, nb, bs, bs), F32),
                        pltpu.VMEM((hp, nb, bs, bs), F32), pltpu.VMEM((hp, bs, bs), F32),
                        pltpu.VMEM((hp, bs, bs), F32), pltpu.VMEM((hp, bs, bs), F32),
                        pltpu.VMEM((hp, bs, d), F32)],
        compiler_params=_params("parallel", "parallel", "arbitrary"),
        name="moba_attn",
    )(slopes, qkv, qkv, qkv)


def _dsa_kernel(q_ref, k_ref, v_ref, qi_ref, tailq_ref, tailk_ref, o_ref,
                kib_ref, qis_ref, key_ref, thr_ref, s_ref, mx_ref, ls_ref, acc_ref, *, n_keep, tkc, scale):
    i = pl.program_id(1)
    tq = q_ref.shape[0]
    G, R, d = C_KV_HEADS, C_HEADS // C_KV_HEADS, C_HEAD_DIM
    qstart = i * tq
    nkc = (qstart + tq + tkc - 1) // tkc

    @pl.when(i == 0)
    def _():
        kib_ref[...] = tailk_ref[:, 0:IDX_DIM].astype(BF16)

    w_t = (jnp.transpose(tailq_ref[...])[IDX_DIM:IDX_DIM + IDX_HEADS, :]
           * (IDX_HEADS ** -0.5 * IDX_DIM ** -0.5))
    key_minus_query = _row_minus_col(tkc, tq)

    for hh in range(IDX_HEADS):
        qis_ref[hh * tq:(hh + 1) * tq, :] = qi_ref[:, hh * IDX_DIM:(hh + 1) * IDX_DIM]

    def score_chunk(c, carry):
        kc = kib_ref[pl.ds(pl.multiple_of(c * tkc, tkc), tkc), :]
        dots = _dot_nt(kc, qis_ref[...])
        acc = jnp.zeros((tkc, tq), F32)
        for hh in range(IDX_HEADS):
            acc = acc + w_t[hh:hh + 1, :] * jnp.maximum(dots[:, hh * tq:(hh + 1) * tq], 0.0)
        bits = lax.bitcast_convert_type(acc, jnp.int32)
        key = bits ^ ((bits >> 31) & jnp.int32(0x7FFFFFFF))
        causal = key_minus_query + (c * tkc - qstart) <= 0
        key_ref[c] = jnp.where(causal, key, jnp.int32(INT_MIN))
        return carry

    lax.fori_loop(0, nkc, score_chunk, 0)

    def count(pred):
        def body(c, part):
            hit = pred(key_ref[c]).astype(jnp.int32)
            return part + jnp.sum(hit.reshape(tkc // 8, 8, tq), axis=0)
        part = lax.fori_loop(0, nkc, body, jnp.zeros((8, tq), jnp.int32))
        return jnp.sum(part, axis=0, keepdims=True)

    thr_ref[...] = jnp.full_like(thr_ref, INT_MIN + 1)

    @pl.when(qstart + tq > n_keep)
    def _():
        zero = jnp.zeros((1, tq), jnp.int32)
        thr0 = jnp.where(count(lambda k: k >= zero) >= n_keep, jnp.int32(0), jnp.int32(INT_MIN))

        def bit_step(b, thr):
            cand = thr | jnp.left_shift(jnp.int32(1), 30 - b)
            return jnp.where(count(lambda k: k >= cand) >= n_keep, cand, thr)

        thr = jnp.maximum(lax.fori_loop(0, 31, bit_step, thr0), jnp.int32(INT_MIN + 1))
        thr_ref[...] = thr

        @pl.when(jnp.max(count(lambda k: k >= thr)) > n_keep)
        def _():
            room = (n_keep - count(lambda k: k > thr)).astype(F32)
            lower_tri = jnp.where(_row_minus_col(tkc, tkc) >= 0, 1.0, 0.0).astype(BF16)

            def drop_excess(c, seen):
                key = key_ref[c]
                tie = key == thr
                tie_f = jnp.where(tie, 1.0, 0.0)
                rank = seen + _dot(lower_tri, tie_f.astype(BF16))
                key_ref[c] = jnp.where(tie & (rank > room), jnp.int32(INT_MIN), key)
                return seen + jnp.sum(tie_f, axis=0, keepdims=True)

            lax.fori_loop(0, nkc, drop_excess, jnp.zeros((1, tq), F32))

    thr = thr_ref[...]
    mx_ref[...] = jnp.full_like(mx_ref, NEG)
    ls_ref[...] = jnp.zeros_like(ls_ref)
    acc_ref[...] = jnp.zeros_like(acc_ref)
    slopes = _alibi_slopes(C_HEADS).reshape(G, R)
    rel = _row_minus_col(tq, tkc).astype(F32)

    def score_pass(c, carry):
        rows = pl.ds(pl.multiple_of(c * tkc, tkc), tkc)
        mask = jnp.transpose(jnp.where(key_ref[c] >= thr, 0.0, NEG))
        dist = rel + (qstart - c * tkc).astype(F32)
        for g in range(G):
            qs = jnp.concatenate(
                [q_ref[:, (g * R + r) * d:(g * R + r + 1) * d] for r in range(R)], axis=0)
            bias = jnp.concatenate([mask - float(slopes[g, r]) * dist for r in range(R)], axis=0)
            s = _dot_nt(qs, k_ref[rows, g * d:(g + 1) * d]) * scale + bias
            s_ref[g, c] = s
            mx_ref[g] = jnp.maximum(mx_ref[g], s)
        return carry

    def acc_pass(c, carry):
        rows = pl.ds(pl.multiple_of(c * tkc, tkc), tkc)
        for g in range(G):
            p = jnp.exp(s_ref[g, c] - mx_ref[g])
            ls_ref[g] += p
            acc_ref[g] += _dot(p.astype(BF16), v_ref[rows, g * d:(g + 1) * d])
        return carry

    lax.fori_loop(0, nkc, score_pass, 0)
    for g in range(G):
        _row_max_bcast(mx_ref.at[g])
    lax.fori_loop(0, nkc, acc_pass, 0)

    for g in range(G):
        o = acc_ref[g] / jnp.sum(ls_ref[g], -1, keepdims=True)
        for r in range(R):
            o_ref[:, (g * R + r) * d:(g * R + r + 1) * d] = o[r * tq:(r + 1) * tq].astype(o_ref.dtype)


def _dsa_attn(main, tail, B, S, tq=128, tkc=256):
    T = B * S
    nq = S // tq
    G, H, d = C_KV_HEADS, C_HEADS, C_HEAD_DIM
    R = H // G
    c_q, c_kv, c_iq = H * d, G * d, IDX_HEADS * IDX_DIM
    assert S % tq == 0 and S % tkc == 0 and tkc % tq == 0
    assert main.shape == (T, c_q + 2 * c_kv + c_iq) and tail.shape == (T, LANES)
    assert c_q % c_kv == 0 and (c_q + 2 * c_kv) % c_iq == 0
    n_keep = min(DSA_TOPK_MAX, S // 4)
    return pl.pallas_call(
        functools.partial(_dsa_kernel, n_keep=n_keep, tkc=tkc, scale=d ** -0.5),
        grid=(B, nq),
        in_specs=[
            pl.BlockSpec((tq, c_q), lambda b, i: (b * nq + i, 0)),
            pl.BlockSpec((S, c_kv), lambda b, i: (b, c_q // c_kv)),
            pl.BlockSpec((S, c_kv), lambda b, i: (b, c_q // c_kv + 1)),
            pl.BlockSpec((tq, c_iq), lambda b, i: (b * nq + i, (c_q + 2 * c_kv) // c_iq)),
            pl.BlockSpec((tq, LANES), lambda b, i: (b * nq + i, 0)),
            pl.BlockSpec((S, LANES), lambda b, i: (b, 0)),
        ],
        out_specs=pl.BlockSpec((tq, c_q), lambda b, i: (b * nq + i, 0)),
        out_shape=jax.ShapeDtypeStruct((T, c_q), BF16),
        scratch_shapes=[pltpu.VMEM((S, IDX_DIM), BF16), pltpu.VMEM((IDX_HEADS * tq, IDX_DIM), BF16),
                        pltpu.VMEM((S // tkc, tkc, tq), jnp.int32),
                        pltpu.VMEM((1, tq), jnp.int32),
                        pltpu.VMEM((G, S // tkc, R * tq, tkc), F32),
                        pltpu.VMEM((G, R * tq, tkc), F32), pltpu.VMEM((G, R * tq, tkc), F32),
                        pltpu.VMEM((G, R * tq, d), F32)],
        compiler_params=_params("parallel", "arbitrary"),
        name="dsa_attn",
    )(main, main, main, main, tail, tail)


def kernel(x, p, ffn1_w_in, ffn1_w_out, ffn2_w_in, ffn2_w_out, ln_g, ln_b, ple_w_gate, ple_w_proj,
           a_w_in, a_w_out, a_lam_q1, a_lam_k1, a_lam_q2, a_lam_k2, a_subln_g, b_w_in, b_w_out,
           c_w_in, c_w_out):
    B, S, D = x.shape
    depth = p.shape[0]
    T = B * S
    alpha = (2.0 * depth) ** 0.25
    bf = lambda w: w.astype(BF16)
    ffn1_in, ffn1_out, ffn2_in, ffn2_out = bf(ffn1_w_in), bf(ffn1_w_out), bf(ffn2_w_in), bf(ffn2_w_out)
    ple_g, ple_p = bf(ple_w_gate), bf(ple_w_proj)
    a_in, a_out, b_in, b_out, c_out = bf(a_w_in), bf(a_w_out), bf(b_w_in), bf(b_w_out), bf(c_w_out)
    n_main = C_HEADS * C_HEAD_DIM + 2 * C_KV_HEADS * C_HEAD_DIM + IDX_HEADS * IDX_DIM
    c_in = bf(c_w_in)
    c_tail = c_w_in[:, :, n_main:]
    c_tail = bf(jnp.pad(c_tail, ((0, 0), (0, 0), (0, LANES - c_tail.shape[2]))))
    lng = ln_g.reshape(-1, 1, D).astype(F32)
    lnb = ln_b.reshape(-1, 1, D).astype(F32)
    n_ln = ln_g.shape[1]
    p = p.reshape(depth, T, -1)
    x = x.reshape(T, D)
    for i in range(depth):
        m, j = i % N_MIXERS, i // N_MIXERS
        x = _ffn(x, ffn1_in, ffn1_out, i, lng, lnb, i * n_ln, alpha)
        if m == 0:
            d = A_HEAD_DIM
            H = D // (2 * d)
            lambda_init = 0.8 - 0.6 * math.exp(-0.3 * i)
            qkv = _proj(x, a_in, j, a_in.shape[2], BF16)
            lam_p = jnp.stack([a_lam_q1[j], a_lam_k1[j], a_lam_q2[j], a_lam_k2[j]]).astype(F32)
            o = _diff_attn(qkv, lam_p, a_subln_g[j].reshape(1, 2 * d).astype(F32), B, S, H, d, lambda_init)
            w_out = a_out
        elif m == 1:
            qkv = _proj(x, b_in, j, b_in.shape[2], BF16)
            o = _moba_attn(qkv, B, S, B_HEADS, B_HEAD_DIM, MOBA_BLOCK)
            w_out = b_out
        else:
            main = _proj(x, c_in, j, n_main, BF16)
            tail = _proj(x, c_tail, j, LANES, F32)
            o = _dsa_attn(main, tail, B, S)
            w_out = c_out
        x = _out_ln(x, o, w_out, j, lng, lnb, i * n_ln + 1, alpha)
        x = _ffn(x, ffn2_in, ffn2_out, i, lng, lnb, i * n_ln + 2, alpha)
        x = _ple(x, p, ple_g, ple_p, i, lng, lnb, i * n_ln + 3, alpha)
    return x.reshape(B, S, D)
```

```python
import functools
import math

import numpy as np
import jax
import jax.numpy as jnp
from jax import lax
from jax.experimental import pallas as pl
from jax.experimental.pallas import tpu as pltpu

F32 = jnp.float32
BF16 = jnp.bfloat16

N_MIXERS = 3
A_HEAD_DIM = 128
B_HEADS = 16
B_HEAD_DIM = 128
MOBA_BLOCK = 256
MOBA_TOPK = 3
C_HEADS = 16
C_KV_HEADS = 4
C_HEAD_DIM = 128
IDX_HEADS = 16
IDX_DIM = 64
DSA_TOPK_MAX = 256
LN_EPS = 1e-5
RMS_EPS = 1e-6

LANES = 128
NEG = -1e30
INT_MIN = -(2 ** 31)
VMEM_LIMIT = 60 * 1024 * 1024
ROW_SPLIT = 128


def _alibi_slopes(n):
    return 2.0 ** (-8.0 * np.arange(1, n + 1, dtype=np.float32) / n)


def _params(*sem):
    return pltpu.CompilerParams(dimension_semantics=sem, vmem_limit_bytes=VMEM_LIMIT)


def _dot(a, b):
    return jnp.dot(a, b, preferred_element_type=F32)


def _dot_nt(a, b):
    return lax.dot_general(a, b, (((1,), (1,)), ((), ())), preferred_element_type=F32)


def _layer_norm(z, g, b):
    mu = jnp.mean(z, -1, keepdims=True)
    zc = z - mu
    var = jnp.mean(zc * zc, -1, keepdims=True)
    return zc * lax.rsqrt(var + LN_EPS) * g + b


def _ffn_kernel(x_ref, wg_ref, wu_ref, wo_ref, g_ref, b_ref, o_ref, xb_ref, *, alpha):
    j = pl.program_id(1)

    @pl.when(j == 0)
    def _():
        xb_ref[...] = x_ref[...].astype(BF16)
        o_ref[...] = jnp.zeros_like(o_ref)

    xb = xb_ref[...]
    gate = _dot(xb, wg_ref[...].astype(BF16))
    up = _dot(xb, wu_ref[...].astype(BF16))
    h = (gate * jax.nn.sigmoid(gate) * up).astype(BF16)
    o_ref[...] += _dot(h, wo_ref[...].astype(BF16))

    @pl.when(j == pl.num_programs(1) - 1)
    def _():
        for r in range(0, o_ref.shape[0], ROW_SPLIT):
            rows = slice(r, r + ROW_SPLIT)
            z = alpha * x_ref[rows, :] + 0.5 * o_ref[rows, :]
            o_ref[rows, :] = _layer_norm(z, g_ref[...], b_ref[...])


def _ln_specs(ln, D, ngrid):
    if ngrid == 1:
        return pl.BlockSpec((None, 1, D), lambda i: (ln, 0, 0))
    return pl.BlockSpec((None, 1, D), lambda i, j: (ln, 0, 0))


def _ffn(x, w_in, w_out, layer, ln_g, ln_b, ln, alpha, tm=1024, tf=256):
    T, D = x.shape
    F = w_out.shape[1]
    nf = F // tf
    assert T % tm == 0 and F % tf == 0 and w_in.shape[1:] == (D, 2 * F)
    return pl.pallas_call(
        functools.partial(_ffn_kernel, alpha=alpha),
        grid=(T // tm, nf),
        in_specs=[
            pl.BlockSpec((tm, D), lambda i, j: (i, 0)),
            pl.BlockSpec((None, D, tf), lambda i, j: (layer, 0, j)),
            pl.BlockSpec((None, D, tf), lambda i, j: (layer, 0, j + nf)),
            pl.BlockSpec((None, tf, D), lambda i, j: (layer, j, 0)),
            _ln_specs(ln, D, 2),
            _ln_specs(ln, D, 2),
        ],
        out_specs=pl.BlockSpec((tm, D), lambda i, j: (i, 0)),
        out_shape=jax.ShapeDtypeStruct((T, D), F32),
        scratch_shapes=[pltpu.VMEM((tm, D), BF16)],
        compiler_params=_params("parallel", "arbitrary"),
        name="ffn",
    )(x, w_in, w_in, w_out, ln_g, ln_b)


def _proj_kernel(x_ref, w_ref, o_ref, xb_ref):
    @pl.when(pl.program_id(1) == 0)
    def _():
        xb_ref[...] = x_ref[...].astype(BF16)

    o_ref[...] = _dot(xb_ref[...], w_ref[...]).astype(o_ref.dtype)


def _proj(x, w, layer, n_cols, out_dtype, tm=1024, tn=1024):
    T, D = x.shape
    tm, tn = min(tm, T), min(tn, n_cols)
    assert T % tm == 0 and n_cols % tn == 0 and w.shape[1] == D and w.shape[2] >= n_cols
    return pl.pallas_call(
        _proj_kernel,
        grid=(T // tm, n_cols // tn),
        in_specs=[
            pl.BlockSpec((tm, D), lambda i, j: (i, 0)),
            pl.BlockSpec((None, D, tn), lambda i, j: (layer, 0, j)),
        ],
        out_specs=pl.BlockSpec((tm, tn), lambda i, j: (i, j)),
        out_shape=jax.ShapeDtypeStruct((T, n_cols), out_dtype),
        scratch_shapes=[pltpu.VMEM((tm, D), BF16)],
        compiler_params=_params("parallel", "arbitrary"),
        name="proj",
    )(x, w)


def _out_ln_kernel(x_ref, o_ref, w_ref, g_ref, b_ref, y_ref, *, alpha):
    for r in range(0, x_ref.shape[0], ROW_SPLIT):
        rows = slice(r, r + ROW_SPLIT)
        z = alpha * x_ref[rows, :] + _dot(o_ref[rows, :], w_ref[...])
        y_ref[rows, :] = _layer_norm(z, g_ref[...], b_ref[...])


def _out_ln(x, o, w, layer, ln_g, ln_b, ln, alpha, tm=512):
    T, D = x.shape
    K = o.shape[1]
    assert T % tm == 0 and tm % ROW_SPLIT == 0 and w.shape[1:] == (K, D)
    return pl.pallas_call(
        functools.partial(_out_ln_kernel, alpha=alpha),
        grid=(T // tm,),
        in_specs=[
            pl.BlockSpec((tm, D), lambda i: (i, 0)),
            pl.BlockSpec((tm, K), lambda i: (i, 0)),
            pl.BlockSpec((None, K, D), lambda i: (layer, 0, 0)),
            _ln_specs(ln, D, 1),
            _ln_specs(ln, D, 1),
        ],
        out_specs=pl.BlockSpec((tm, D), lambda i: (i, 0)),
        out_shape=jax.ShapeDtypeStruct((T, D), F32),
        compiler_params=_params("parallel"),
        name="out_ln",
    )(x, o, w, ln_g, ln_b)


def _ple_kernel(x_ref, p_ref, wg_ref, wp_ref, g_ref, b_ref, y_ref, *, alpha):
    for r in range(0, x_ref.shape[0], ROW_SPLIT):
        rows = slice(r, r + ROW_SPLIT)
        x = x_ref[rows, :]
        gate = jax.nn.sigmoid(_dot(x.astype(BF16), wg_ref[...]))
        emb = _dot(p_ref[rows, :].astype(BF16), wp_ref[...])
        y_ref[rows, :] = _layer_norm(alpha * x + gate * emb, g_ref[...], b_ref[...])


def _ple(x, p, wg, wp, layer, ln_g, ln_b, ln, alpha, tm=512):
    T, D = x.shape
    P = p.shape[2]
    assert T % tm == 0 and tm % ROW_SPLIT == 0
    return pl.pallas_call(
        functools.partial(_ple_kernel, alpha=alpha),
        grid=(T // tm,),
        in_specs=[
            pl.BlockSpec((tm, D), lambda i: (i, 0)),
            pl.BlockSpec((None, tm, P), lambda i: (layer, i, 0)),
            pl.BlockSpec((None, D, D), lambda i: (layer, 0, 0)),
            pl.BlockSpec((None, P, D), lambda i: (layer, 0, 0)),
            _ln_specs(ln, D, 1),
            _ln_specs(ln, D, 1),
        ],
        out_specs=pl.BlockSpec((tm, D), lambda i: (i, 0)),
        out_shape=jax.ShapeDtypeStruct((T, D), F32),
        compiler_params=_params("parallel"),
        name="ple",
    )(x, p, wg, wp, ln_g, ln_b)


def _row_max_bcast(mx_ref):
    m = jnp.max(mx_ref[...], -1, keepdims=True)
    mx_ref[...] = jnp.broadcast_to(m, mx_ref.shape)


def _causal_sweep(fn, qi):
    def body(j, carry):
        fn(2 * j, 2, False)
        return carry

    lax.fori_loop(0, qi // 2, body, 0)

    @pl.when(qi % 2 == 1)
    def _():
        fn(qi - 1, 2, True)

    @pl.when(qi % 2 == 0)
    def _():
        fn(qi, 1, True)


def _row_minus_col(rows, cols):
    r = lax.broadcasted_iota(jnp.int32, (rows, cols), 0)
    c = lax.broadcasted_iota(jnp.int32, (rows, cols), 1)
    return r - c


DIFF_HEADS_PER_STEP = 2


def _diff_attn_kernel(slopes_ref, lam_ref, subg_ref, q_ref, k_ref, v_ref, o_ref,
                      s_ref, bias_ref, mx_ref, ls_ref, acc_ref, *, d, scale, lambda_init):
    hp = DIFF_HEADS_PER_STEP
    h0 = pl.program_id(1) * hp
    qi = pl.program_id(2)
    tq = q_ref.shape[0]
    units = range(2 * hp)
    slopes = [slopes_ref[h0 + hh] for hh in range(hp)]
    q = [q_ref[:, u * d:(u + 1) * d] for u in units]
    mx_ref[...] = jnp.full_like(mx_ref, NEG)
    ls_ref[...] = jnp.zeros_like(ls_ref)
    acc_ref[...] = jnp.zeros_like(acc_ref)

    rel = _row_minus_col(tq, 2 * tq).astype(F32)
    for hh in range(hp):
        bias_ref[hh] = -slopes[hh] * rel

    def span_scores(c0, width, diagonal):
        rows = pl.ds(pl.multiple_of(c0 * tq, tq), width * tq)
        for hh in range(hp):
            bias = bias_ref[hh, :, :width * tq] - slopes[hh] * ((qi - c0) * tq).astype(F32)
            for u in (2 * hh, 2 * hh + 1):
                s = _dot_nt(q[u], k_ref[rows, u * d:(u + 1) * d]) * scale + bias
                if diagonal:
                    s = jnp.where(_row_minus_col(tq, width * tq) + (qi - c0) * tq >= 0, s, NEG)
                mx = mx_ref[u]
                for w in range(width):
                    sw = s[:, w * tq:(w + 1) * tq]
                    s_ref[u, c0 + w] = sw
                    mx = jnp.maximum(mx, sw)
                mx_ref[u] = mx

    def span_accumulate(c0, width, diagonal):
        del diagonal
        rows = pl.ds(pl.multiple_of(c0 * tq, tq), width * tq)
        for u in units:
            hh = u // 2
            ps = [jnp.exp(s_ref[u, c0 + w] - mx_ref[u]) for w in range(width)]
            ls_ref[u] += functools.reduce(lambda a, b: a + b, ps)
            p = ps[0] if width == 1 else jnp.concatenate(ps, axis=1)
            acc_ref[u] += _dot(p.astype(BF16), v_ref[rows, hh * 2 * d:(hh + 1) * 2 * d])

    _causal_sweep(span_scores, qi)
    for u in units:
        _row_max_bcast(mx_ref.at[u])
    _causal_sweep(span_accumulate, qi)

    lam_p = lam_ref[...]
    lam = (jnp.exp(jnp.sum(lam_p[0:1] * lam_p[1:2], -1, keepdims=True))
           - jnp.exp(jnp.sum(lam_p[2:3] * lam_p[3:4], -1, keepdims=True)) + lambda_init)
    for hh in range(hp):
        l0 = jnp.sum(ls_ref[2 * hh], -1, keepdims=True)
        l1 = jnp.sum(ls_ref[2 * hh + 1], -1, keepdims=True)
        o = acc_ref[2 * hh] / l0 - lam * (acc_ref[2 * hh + 1] / l1)
        o = o * lax.rsqrt(jnp.mean(o * o, -1, keepdims=True) + RMS_EPS) * subg_ref[...] * (1.0 - lambda_init)
        o_ref[:, hh * 2 * d:(hh + 1) * 2 * d] = o.astype(o_ref.dtype)


def _diff_attn(qkv, lam_p, subln_g, B, S, H, d, lambda_init, tq=256):
    T = B * S
    nq = S // tq
    hp = DIFF_HEADS_PER_STEP
    w = hp * 2 * d
    assert S % tq == 0 and H % hp == 0 and qkv.shape == (T, 3 * H * 2 * d)
    slopes = jnp.asarray(_alibi_slopes(H), F32)
    return pl.pallas_call(
        functools.partial(_diff_attn_kernel, d=d, scale=d ** -0.5, lambda_init=lambda_init),
        grid=(B, H // hp, nq),
        in_specs=[
            pl.BlockSpec(memory_space=pltpu.SMEM),
            pl.BlockSpec((4, d), lambda b, h, i: (0, 0)),
            pl.BlockSpec((1, 2 * d), lambda b, h, i: (0, 0)),
            pl.BlockSpec((tq, w), lambda b, h, i: (b * nq + i, h)),
            pl.BlockSpec((S, w), lambda b, h, i: (b, H // hp + h)),
            pl.BlockSpec((S, w), lambda b, h, i: (b, 2 * (H // hp) + h)),
        ],
        out_specs=pl.BlockSpec((tq, w), lambda b, h, i: (b * nq + i, h)),
        out_shape=jax.ShapeDtypeStruct((T, H * 2 * d), BF16),
        scratch_shapes=[pltpu.VMEM((2 * hp, nq, tq, tq), F32), pltpu.VMEM((hp, tq, 2 * tq), F32),
                        pltpu.VMEM((2 * hp, tq, tq), F32), pltpu.VMEM((2 * hp, tq, tq), F32),
                        pltpu.VMEM((2 * hp, tq, 2 * d), F32)],
        compiler_params=_params("parallel", "parallel", "arbitrary"),
        name="diff_attn",
    )(slopes, lam_p, subln_g, qkv, qkv, qkv)


MOBA_HEADS_PER_STEP = 2


def _moba_kernel(slopes_ref, q_ref, k_ref, v_ref, o_ref, km_ref, mb_ref, s_ref, bias_ref,
                 mx_ref, ls_ref, acc_ref, *, nb, n_sel, d, scale):
    hp = MOBA_HEADS_PER_STEP
    h0 = pl.program_id(1) * hp
    own = pl.program_id(2)
    bs = q_ref.shape[0]
    nbp = -(-nb // 8) * 8
    heads = range(hp)
    cols = [slice(hh * d, (hh + 1) * d) for hh in heads]
    slopes = [slopes_ref[h0 + hh] for hh in heads]

    @pl.when(own == 0)
    def _():
        km_ref[...] = jnp.zeros_like(km_ref)
        for hh in heads:
            for n in range(nb):
                km_ref[hh, n:n + 1, :] = jnp.mean(
                    k_ref[n * bs:(n + 1) * bs, cols[hh]].astype(F32), 0, keepdims=True)

    q = [q_ref[:, cols[hh]] for hh in heads]
    rel_i = _row_minus_col(bs, bs)
    block = lax.broadcasted_iota(jnp.int32, (nbp, bs), 0)
    for hh in heads:
        km = km_ref[hh]
        km_hi = km.astype(BF16)
        km_lo = (km - km_hi.astype(F32)).astype(BF16)
        gate = (_dot_nt(km_hi, q[hh]) + _dot_nt(km_lo, q[hh]))[0:nbp, :]
        rank = jnp.zeros((nbp, bs), jnp.int32)
        for m in range(nb - 1):
            gm = gate[m:m + 1, :]
            beats = (gm > gate) | ((gm == gate) & (block > m))
            rank = rank + beats.astype(jnp.int32) * (own > m).astype(jnp.int32)
        sel = (rank < n_sel) & (block < own)
        off = -slopes[hh] * ((own - block) * bs).astype(F32)
        mb_t = jnp.where(sel, off, NEG)
        mb = jnp.transpose(jnp.concatenate([mb_t, jnp.zeros((LANES - nbp, bs), F32)], axis=0))
        for n in range(nb - 1):
            mb_ref[hh, n] = jnp.broadcast_to(mb[:, n:n + 1], (bs, LANES))
        bias_ref[hh] = -slopes[hh] * rel_i.astype(F32)

    mx_ref[...] = jnp.full_like(mx_ref, NEG)
    ls_ref[...] = jnp.zeros_like(ls_ref)
    acc_ref[...] = jnp.zeros_like(acc_ref)

    def span_scores(c0, width, diagonal):
        rows = pl.ds(pl.multiple_of(c0 * bs, bs), width * bs)
        for hh in heads:
            qk = _dot_nt(q[hh], k_ref[rows, cols[hh]]) * scale
            mx = mx_ref[hh]
            for w in range(width):
                sw = qk[:, w * bs:(w + 1) * bs] + bias_ref[hh]
                if diagonal and w == width - 1:
                    sw = jnp.where(rel_i >= 0, sw, NEG)
                else:
                    sw = sw + jnp.concatenate([mb_ref[hh, c0 + w]] * (bs // LANES), axis=1)
                s_ref[hh, c0 + w] = sw
                mx = jnp.maximum(mx, sw)
            mx_ref[hh] = mx

    def span_accumulate(c0, width, diagonal):
        del diagonal
        rows = pl.ds(pl.multiple_of(c0 * bs, bs), width * bs)
        for hh in heads:
            ps = [jnp.exp(s_ref[hh, c0 + w] - mx_ref[hh]) for w in range(width)]
            ls_ref[hh] += functools.reduce(lambda a, b: a + b, ps)
            p = ps[0] if width == 1 else jnp.concatenate(ps, axis=1)
            acc_ref[hh] += _dot(p.astype(BF16), v_ref[rows, cols[hh]])

    _causal_sweep(span_scores, own)
    for hh in heads:
        _row_max_bcast(mx_ref.at[hh])
    _causal_sweep(span_accumulate, own)

    for hh in heads:
        l = jnp.sum(ls_ref[hh], -1, keepdims=True)
        o_ref[:, cols[hh]] = (acc_ref[hh] / l).astype(o_ref.dtype)


def _moba_attn(qkv, B, S, H, d, bs):
    T = B * S
    nb = S // bs
    hp = MOBA_HEADS_PER_STEP
    assert S % bs == 0 and nb <= LANES and H % hp == 0 and qkv.shape == (T, 3 * H * d)
    n_sel = min(MOBA_TOPK, nb - 1)
    slopes = jnp.asarray(_alibi_slopes(H), F32)
    return pl.pallas_call(
        functools.partial(_moba_kernel, nb=nb, n_sel=n_sel, d=d, scale=d ** -0.5),
        grid=(B, H // hp, nb),
        in_specs=[
            pl.BlockSpec(memory_space=pltpu.SMEM),
            pl.BlockSpec((bs, hp * d), lambda b, h, i: (b * nb + i, h)),
            pl.BlockSpec((S, hp * d), lambda b, h, i: (b, H // hp + h)),
            pl.BlockSpec((S, hp * d), lambda b, h, i: (b, 2 * (H // hp) + h)),
        ],
        out_specs=pl.BlockSpec((bs, hp * d), lambda b, h, i: (b * nb + i, h)),
        out_shape=jax.ShapeDtypeStruct((T, H * d), BF16),
        scratch_shapes=[pltpu.VMEM((hp, LANES, d), F32), pltpu.VMEM((hp, nb, bs, LANES), F32),
                        pltpu.VMEM((hp, nb, bs, bs), F32), pltpu.VMEM((hp, bs, bs), F32),
                        pltpu.VMEM((hp, bs, bs), F32), pltpu.VMEM((hp, bs, bs), F32),
                        pltpu.VMEM((hp, bs, d), F32)],
        compiler_params=_params("parallel", "parallel", "arbitrary"),
        name="moba_attn",
    )(slopes, qkv, qkv, qkv)


def _dsa_kernel(q_ref, k_ref, v_ref, qi_ref, tailq_ref, tailk_ref, o_ref,
                kib_ref, qis_ref, key_ref, thr_ref, s_ref, mx_ref, ls_ref, acc_ref, *, n_keep, tkc, scale):
    i = pl.program_id(1)
    tq = q_ref.shape[0]
    G, R, d = C_KV_HEADS, C_HEADS // C_KV_HEADS, C_HEAD_DIM
    qstart = i * tq
    nkc = (qstart + tq + tkc - 1) // tkc

    @pl.when(i == 0)
    def _():
        kib_ref[...] = tailk_ref[:, 0:IDX_DIM].astype(BF16)

    w_t = (jnp.transpose(tailq_ref[...])[IDX_DIM:IDX_DIM + IDX_HEADS, :]
           * (IDX_HEADS ** -0.5 * IDX_DIM ** -0.5))
    key_minus_query = _row_minus_col(tkc, tq)

    for hh in range(IDX_HEADS):
        qis_ref[hh * tq:(hh + 1) * tq, :] = qi_ref[:, hh * IDX_DIM:(hh + 1) * IDX_DIM]

    def score_chunk(c, carry):
        kc = kib_ref[pl.ds(pl.multiple_of(c * tkc, tkc), tkc), :]
        dots = _dot_nt(kc, qis_ref[...])
        acc = jnp.zeros((tkc, tq), F32)
        for hh in range(IDX_HEADS):
            acc = acc + w_t[hh:hh + 1, :] * jnp.maximum(dots[:, hh * tq:(hh + 1) * tq], 0.0)
        bits = lax.bitcast_convert_type(acc, jnp.int32)
        key = bits ^ ((bits >> 31) & jnp.int32(0x7FFFFFFF))
        causal = key_minus_query + (c * tkc - qstart) <= 0
        key_ref[c] = jnp.where(causal, key, jnp.int32(INT_MIN))
        return carry

    lax.fori_loop(0, nkc, score_chunk, 0)

    def count(pred):
        def body(c, part):
            hit = pred(key_ref[c]).astype(jnp.int32)
            return part + jnp.sum(hit.reshape(tkc // 8, 8, tq), axis=0)
        part = lax.fori_loop(0, nkc, body, jnp.zeros((8, tq), jnp.int32))
        return jnp.sum(part, axis=0, keepdims=True)

    thr_ref[...] = jnp.full_like(thr_ref, INT_MIN + 1)

    @pl.when(qstart + tq > n_keep)
    def _():
        zero = jnp.zeros((1, tq), jnp.int32)
        thr0 = jnp.where(count(lambda k: k >= zero) >= n_keep, jnp.int32(0), jnp.int32(INT_MIN))

        def bit_step(b, thr):
            cand = thr | jnp.left_shift(jnp.int32(1), 30 - b)
            return jnp.where(count(lambda k: k >= cand) >= n_keep, cand, thr)

        thr = jnp.maximum(lax.fori_loop(0, 31, bit_step, thr0), jnp.int32(INT_MIN + 1))
        thr_ref[...] = thr

        @pl.when(jnp.max(count(lambda k: k >= thr)) > n_keep)
        def _():
            room = (n_keep - count(lambda k: k > thr)).astype(F32)
            lower_tri = jnp.where(_row_minus_col(tkc, tkc) >= 0, 1.0, 0.0).astype(BF16)

            def drop_excess(c, seen):
                key = key_ref[c]
                tie = key == thr
                tie_f = jnp.where(tie, 1.0, 0.0)
                rank = seen + _dot(lower_tri, tie_f.astype(BF16))
                key_ref[c] = jnp.where(tie & (rank > room), jnp.int32(INT_MIN), key)
                return seen + jnp.sum(tie_f, axis=0, keepdims=True)

            lax.fori_loop(0, nkc, drop_excess, jnp.zeros((1, tq), F32))

    thr = thr_ref[...]
    mx_ref[...] = jnp.full_like(mx_ref, NEG)
    ls_ref[...] = jnp.zeros_like(ls_ref)
    acc_ref[...] = jnp.zeros_like(acc_ref)
    slopes = _alibi_slopes(C_HEADS).reshape(G, R)
    rel = _row_minus_col(tq, tkc).astype(F32)

    def score_pass(c, carry):
        rows = pl.ds(pl.multiple_of(c * tkc, tkc), tkc)
        mask = jnp.transpose(jnp.where(key_ref[c] >= thr, 0.0, NEG))
        dist = rel + (qstart - c * tkc).astype(F32)
        for g in range(G):
            qs = jnp.concatenate(
                [q_ref[:, (g * R + r) * d:(g * R + r + 1) * d] for r in range(R)], axis=0)
            bias = jnp.concatenate([mask - float(slopes[g, r]) * dist for r in range(R)], axis=0)
            s = _dot_nt(qs, k_ref[rows, g * d:(g + 1) * d]) * scale + bias
            s_ref[g, c] = s
            mx_ref[g] = jnp.maximum(mx_ref[g], s)
        return carry

    def acc_pass(c, carry):
        rows = pl.ds(pl.multiple_of(c * tkc, tkc), tkc)
        for g in range(G):
            p = jnp.exp(s_ref[g, c] - mx_ref[g])
            ls_ref[g] += p
            acc_ref[g] += _dot(p.astype(BF16), v_ref[rows, g * d:(g + 1) * d])
        return carry

    lax.fori_loop(0, nkc, score_pass, 0)
    for g in range(G):
        _row_max_bcast(mx_ref.at[g])
    lax.fori_loop(0, nkc, acc_pass, 0)

    for g in range(G):
        o = acc_ref[g] / jnp.sum(ls_ref[g], -1, keepdims=True)
        for r in range(R):
            o_ref[:, (g * R + r) * d:(g * R + r + 1) * d] = o[r * tq:(r + 1) * tq].astype(o_ref.dtype)


def _dsa_attn(main, tail, B, S, tq=128, tkc=256):
    T = B * S
    nq = S // tq
    G, H, d = C_KV_HEADS, C_HEADS, C_HEAD_DIM
    R = H // G
    c_q, c_kv, c_iq = H * d, G * d, IDX_HEADS * IDX_DIM
    assert S % tq == 0 and S % tkc == 0 and tkc % tq == 0
    assert main.shape == (T, c_q + 2 * c_kv + c_iq) and tail.shape == (T, LANES)
    assert c_q % c_kv == 0 and (c_q + 2 * c_kv) % c_iq == 0
    n_keep = min(DSA_TOPK_MAX, S // 4)
    return pl.pallas_call(
        functools.partial(_dsa_kernel, n_keep=n_keep, tkc=tkc, scale=d ** -0.5),
        grid=(B, nq),
        in_specs=[
            pl.BlockSpec((tq, c_q), lambda b, i: (b * nq + i, 0)),
            pl.BlockSpec((S, c_kv), lambda b, i: (b, c_q // c_kv)),
            pl.BlockSpec((S, c_kv), lambda b, i: (b, c_q // c_kv + 1)),
            pl.BlockSpec((tq, c_iq), lambda b, i: (b * nq + i, (c_q + 2 * c_kv) // c_iq)),
            pl.BlockSpec((tq, LANES), lambda b, i: (b * nq + i, 0)),
            pl.BlockSpec((S, LANES), lambda b, i: (b, 0)),
        ],
        out_specs=pl.BlockSpec((tq, c_q), lambda b, i: (b * nq + i, 0)),
        out_shape=jax.ShapeDtypeStruct((T, c_q), BF16),
        scratch_shapes=[pltpu.VMEM((S, IDX_DIM), BF16), pltpu.VMEM((IDX_HEADS * tq, IDX_DIM), BF16),
                        pltpu.VMEM((S // tkc, tkc, tq), jnp.int32),
                        pltpu.VMEM((1, tq), jnp.int32),
                        pltpu.VMEM((G, S // tkc, R * tq, tkc), F32),
                        pltpu.VMEM((G, R * tq, tkc), F32), pltpu.VMEM((G, R * tq, tkc), F32),
                        pltpu.VMEM((G, R * tq, d), F32)],
        compiler_params=_params("parallel", "arbitrary"),
        name="dsa_attn",
    )(main, main, main, main, tail, tail)


def kernel(x, p, ffn1_w_in, ffn1_w_out, ffn2_w_in, ffn2_w_out, ln_g, ln_b, ple_w_gate, ple_w_proj,
           a_w_in, a_w_out, a_lam_q1, a_lam_k1, a_lam_q2, a_lam_k2, a_subln_g, b_w_in, b_w_out,
           c_w_in, c_w_out):
    B, S, D = x.shape
    depth = p.shape[0]
    T = B * S
    alpha = (2.0 * depth) ** 0.25
    bf = lambda w: w.astype(BF16)
    ple_g, ple_p = bf(ple_w_gate), bf(ple_w_proj)
    a_in, a_out, b_in, b_out, c_out = bf(a_w_in), bf(a_w_out), bf(b_w_in), bf(b_w_out), bf(c_w_out)
    n_main = C_HEADS * C_HEAD_DIM + 2 * C_KV_HEADS * C_HEAD_DIM + IDX_HEADS * IDX_DIM
    c_in = bf(c_w_in)
    c_tail = c_w_in[:, :, n_main:]
    c_tail = bf(jnp.pad(c_tail, ((0, 0), (0, 0), (0, LANES - c_tail.shape[2]))))
    lng = ln_g.reshape(-1, 1, D).astype(F32)
    lnb = ln_b.reshape(-1, 1, D).astype(F32)
    n_ln = ln_g.shape[1]
    p = p.reshape(depth, T, -1)
    x = x.reshape(T, D)
    for i in range(depth):
        m, j = i % N_MIXERS, i // N_MIXERS
        x = _ffn(x, ffn1_w_in, ffn1_w_out, i, lng, lnb, i * n_ln, alpha)
        if m == 0:
            d = A_HEAD_DIM
            H = D // (2 * d)
            lambda_init = 0.8 - 0.6 * math.exp(-0.3 * i)
            qkv = _proj(x, a_in, j, a_in.shape[2], BF16)
            lam_p = jnp.stack([a_lam_q1[j], a_lam_k1[j], a_lam_q2[j], a_lam_k2[j]]).astype(F32)
            o = _diff_attn(qkv, lam_p, a_subln_g[j].reshape(1, 2 * d).astype(F32), B, S, H, d, lambda_init)
            w_out = a_out
        elif m == 1:
            qkv = _proj(x, b_in, j, b_in.shape[2], BF16)
            o = _moba_attn(qkv, B, S, B_HEADS, B_HEAD_DIM, MOBA_BLOCK)
            w_out = b_out
        else:
            main = _proj(x, c_in, j, n_main, BF16)
            tail = _proj(x, c_tail, j, LANES, F32)
            o = _dsa_attn(main, tail, B, S)
            w_out = c_out
        x = _out_ln(x, o, w_out, j, lng, lnb, i * n_ln + 1, alpha)
        x = _ffn(x, ffn2_w_in, ffn2_w_out, i, lng, lnb, i * n_ln + 2, alpha)
        x = _ple(x, p, ple_g, ple_p, i, lng, lnb, i * n_ln + 3, alpha)
    return x.reshape(B, S, D)
```

```python
import functools
import math

import numpy as np
import jax
import jax.numpy as jnp
from jax import lax
from jax.experimental import pallas as pl
from jax.experimental.pallas import tpu as pltpu

F32 = jnp.float32
BF16 = jnp.bfloat16

N_MIXERS = 3
A_HEAD_DIM = 128
B_HEADS = 16
B_HEAD_DIM = 128
MOBA_BLOCK = 256
MOBA_TOPK = 3
C_HEADS = 16
C_KV_HEADS = 4
C_HEAD_DIM = 128
IDX_HEADS = 16
IDX_DIM = 64
DSA_TOPK_MAX = 256
LN_EPS = 1e-5
RMS_EPS = 1e-6

LANES = 128
NEG = -1e30
INT_MIN = -(2 ** 31)
VMEM_LIMIT = 60 * 1024 * 1024
ROW_SPLIT = 128


def _alibi_slopes(n):
    return 2.0 ** (-8.0 * np.arange(1, n + 1, dtype=np.float32) / n)


def _params(*sem):
    return pltpu.CompilerParams(dimension_semantics=sem, vmem_limit_bytes=VMEM_LIMIT)


def _dot(a, b):
    return jnp.dot(a, b, preferred_element_type=F32)


def _dot_nt(a, b):
    return lax.dot_general(a, b, (((1,), (1,)), ((), ())), preferred_element_type=F32)


def _layer_norm(z, g, b):
    mu = jnp.mean(z, -1, keepdims=True)
    zc = z - mu
    var = jnp.mean(zc * zc, -1, keepdims=True)
    return zc * lax.rsqrt(var + LN_EPS) * g + b


def _ffn_kernel(x_ref, wg_ref, wu_ref, wo_ref, g_ref, b_ref, o_ref, xb_ref, *, alpha):
    j = pl.program_id(1)

    @pl.when(j == 0)
    def _():
        xb_ref[...] = x_ref[...].astype(BF16)
        o_ref[...] = jnp.zeros_like(o_ref)

    xb = xb_ref[...]
    gate = _dot(xb, wg_ref[...].astype(BF16))
    up = _dot(xb, wu_ref[...].astype(BF16))
    h = (gate * jax.nn.sigmoid(gate) * up).astype(BF16)
    o_ref[...] += _dot(h, wo_ref[...].astype(BF16))

    @pl.when(j == pl.num_programs(1) - 1)
    def _():
        for r in range(0, o_ref.shape[0], ROW_SPLIT):
            rows = slice(r, r + ROW_SPLIT)
            z = alpha * x_ref[rows, :] + 0.5 * o_ref[rows, :]
            o_ref[rows, :] = _layer_norm(z, g_ref[...], b_ref[...])


def _ln_specs(ln, D, ngrid):
    if ngrid == 1:
        return pl.BlockSpec((None, 1, D), lambda i: (ln, 0, 0))
    return pl.BlockSpec((None, 1, D), lambda i, j: (ln, 0, 0))


def _ffn(x, w_in, w_out, layer, ln_g, ln_b, ln, alpha, tm=1024, tf=256):
    T, D = x.shape
    F = w_out.shape[1]
    nf = F // tf
    assert T % tm == 0 and F % tf == 0 and w_in.shape[1:] == (D, 2 * F)
    return pl.pallas_call(
        functools.partial(_ffn_kernel, alpha=alpha),
        grid=(T // tm, nf),
        in_specs=[
            pl.BlockSpec((tm, D), lambda i, j: (i, 0)),
            pl.BlockSpec((None, D, tf), lambda i, j: (layer, 0, j)),
            pl.BlockSpec((None, D, tf), lambda i, j: (layer, 0, j + nf)),
            pl.BlockSpec((None, tf, D), lambda i, j: (layer, j, 0)),
            _ln_specs(ln, D, 2),
            _ln_specs(ln, D, 2),
        ],
        out_specs=pl.BlockSpec((tm, D), lambda i, j: (i, 0)),
        out_shape=jax.ShapeDtypeStruct((T, D), F32),
        scratch_shapes=[pltpu.VMEM((tm, D), BF16)],
        compiler_params=_params("parallel", "arbitrary"),
        name="ffn",
    )(x, w_in, w_in, w_out, ln_g, ln_b)


def _proj_kernel(x_ref, w_ref, o_ref, xb_ref):
    @pl.when(pl.program_id(1) == 0)
    def _():
        xb_ref[...] = x_ref[...].astype(BF16)

    o_ref[...] = _dot(xb_ref[...], w_ref[...]).astype(o_ref.dtype)


def _proj(x, w, layer, n_cols, out_dtype, tm=1024, tn=1024):
    T, D = x.shape
    tm, tn = min(tm, T), min(tn, n_cols)
    assert T % tm == 0 and n_cols % tn == 0 and w.shape[1] == D and w.shape[2] >= n_cols
    return pl.pallas_call(
        _proj_kernel,
        grid=(T // tm, n_cols // tn),
        in_specs=[
            pl.BlockSpec((tm, D), lambda i, j: (i, 0)),
            pl.BlockSpec((None, D, tn), lambda i, j: (layer, 0, j)),
        ],
        out_specs=pl.BlockSpec((tm, tn), lambda i, j: (i, j)),
        out_shape=jax.ShapeDtypeStruct((T, n_cols), out_dtype),
        scratch_shapes=[pltpu.VMEM((tm, D), BF16)],
        compiler_params=_params("parallel", "arbitrary"),
        name="proj",
    )(x, w)


def _out_ln_kernel(x_ref, o_ref, w_ref, g_ref, b_ref, y_ref, *, alpha):
    for r in range(0, x_ref.shape[0], ROW_SPLIT):
        rows = slice(r, r + ROW_SPLIT)
        z = alpha * x_ref[rows, :] + _dot(o_ref[rows, :], w_ref[...])
        y_ref[rows, :] = _layer_norm(z, g_ref[...], b_ref[...])


def _out_ln(x, o, w, layer, ln_g, ln_b, ln, alpha, tm=512):
    T, D = x.shape
    K = o.shape[1]
    assert T % tm == 0 and tm % ROW_SPLIT == 0 and w.shape[1:] == (K, D)
    return pl.pallas_call(
        functools.partial(_out_ln_kernel, alpha=alpha),
        grid=(T // tm,),
        in_specs=[
            pl.BlockSpec((tm, D), lambda i: (i, 0)),
            pl.BlockSpec((tm, K), lambda i: (i, 0)),
            pl.BlockSpec((None, K, D), lambda i: (layer, 0, 0)),
            _ln_specs(ln, D, 1),
            _ln_specs(ln, D, 1),
        ],
        out_specs=pl.BlockSpec((tm, D), lambda i: (i, 0)),
        out_shape=jax.ShapeDtypeStruct((T, D), F32),
        compiler_params=_params("parallel"),
        name="out_ln",
    )(x, o, w, ln_g, ln_b)


def _ple_kernel(x_ref, p_ref, wg_ref, wp_ref, g_ref, b_ref, y_ref, *, alpha):
    for r in range(0, x_ref.shape[0], ROW_SPLIT):
        rows = slice(r, r + ROW_SPLIT)
        x = x_ref[rows, :]
        gate = jax.nn.sigmoid(_dot(x.astype(BF16), wg_ref[...]))
        emb = _dot(p_ref[rows, :].astype(BF16), wp_ref[...])
        y_ref[rows, :] = _layer_norm(alpha * x + gate * emb, g_ref[...], b_ref[...])


def _ple(x, p, wg, wp, layer, ln_g, ln_b, ln, alpha, tm=512):
    T, D = x.shape
    P = p.shape[2]
    assert T % tm == 0 and tm % ROW_SPLIT == 0
    return pl.pallas_call(
        functools.partial(_ple_kernel, alpha=alpha),
        grid=(T // tm,),
        in_specs=[
            pl.BlockSpec((tm, D), lambda i: (i, 0)),
            pl.BlockSpec((None, tm, P), lambda i: (layer, i, 0)),
            pl.BlockSpec((None, D, D), lambda i: (layer, 0, 0)),
            pl.BlockSpec((None, P, D), lambda i: (layer, 0, 0)),
            _ln_specs(ln, D, 1),
            _ln_specs(ln, D, 1),
        ],
        out_specs=pl.BlockSpec((tm, D), lambda i: (i, 0)),
        out_shape=jax.ShapeDtypeStruct((T, D), F32),
        compiler_params=_params("parallel"),
        name="ple",
    )(x, p, wg, wp, ln_g, ln_b)


def _row_max_bcast(mx_ref):
    m = jnp.max(mx_ref[...], -1, keepdims=True)
    mx_ref[...] = jnp.broadcast_to(m, mx_ref.shape)


def _causal_sweep(fn, qi):
    @pl.when(qi % 2 == 1)
    def _():
        fn(qi - 1, 2, True)

    @pl.when(qi % 2 == 0)
    def _():
        fn(qi, 1, True)

    def body(j, carry):
        fn(2 * j, 2, False)
        return carry

    lax.fori_loop(0, qi // 2, body, 0)


def _row_minus_col(rows, cols):
    r = lax.broadcasted_iota(jnp.int32, (rows, cols), 0)
    c = lax.broadcasted_iota(jnp.int32, (rows, cols), 1)
    return r - c


DIFF_HEADS_PER_STEP = 4


def _diff_attn_kernel(slopes_ref, lam_ref, subg_ref, q_ref, k_ref, v_ref, o_ref,
                      s_ref, bias_ref, mx_ref, ls_ref, acc_ref, *, d, scale, lambda_init):
    hp = DIFF_HEADS_PER_STEP
    h0 = pl.program_id(1) * hp
    qi = pl.program_id(2)
    tq = q_ref.shape[0]
    units = range(2 * hp)
    slopes = [slopes_ref[h0 + hh] for hh in range(hp)]
    q = [q_ref[:, u * d:(u + 1) * d] for u in units]

    @pl.when(qi == 0)
    def _():
        rel = _row_minus_col(tq, 2 * tq).astype(F32)
        for hh in range(hp):
            bias_ref[hh] = -slopes[hh] * rel

    def span_scores(c0, width, diagonal):
        rows = pl.ds(pl.multiple_of(c0 * tq, tq), width * tq)
        for hh in range(hp):
            bias = bias_ref[hh, :, :width * tq] - slopes[hh] * ((qi - c0) * tq).astype(F32)
            for u in (2 * hh, 2 * hh + 1):
                s = _dot_nt(q[u], k_ref[rows, u * d:(u + 1) * d]) * scale + bias
                if diagonal:
                    s = jnp.where(_row_minus_col(tq, width * tq) + (qi - c0) * tq >= 0, s, NEG)
                chunks = [s[:, w * tq:(w + 1) * tq] for w in range(width)]
                for w in range(width):
                    s_ref[u, c0 + w] = chunks[w]
                mx = functools.reduce(jnp.maximum, chunks)
                mx_ref[u] = mx if diagonal else jnp.maximum(mx_ref[u], mx)

    def span_accumulate(c0, width, diagonal):
        rows = pl.ds(pl.multiple_of(c0 * tq, tq), width * tq)
        for u in units:
            hh = u // 2
            ps = [jnp.exp(s_ref[u, c0 + w] - mx_ref[u]) for w in range(width)]
            psum = functools.reduce(lambda a, b: a + b, ps)
            p = ps[0] if width == 1 else jnp.concatenate(ps, axis=1)
            pv = _dot(p.astype(BF16), v_ref[rows, hh * 2 * d:(hh + 1) * 2 * d])
            if diagonal:
                ls_ref[u] = psum
                acc_ref[u] = pv
            else:
                ls_ref[u] += psum
                acc_ref[u] += pv

    _causal_sweep(span_scores, qi)
    for u in units:
        _row_max_bcast(mx_ref.at[u])
    _causal_sweep(span_accumulate, qi)

    lam_p = lam_ref[...]
    lam = (jnp.exp(jnp.sum(lam_p[0:1] * lam_p[1:2], -1, keepdims=True))
           - jnp.exp(jnp.sum(lam_p[2:3] * lam_p[3:4], -1, keepdims=True)) + lambda_init)
    for hh in range(hp):
        l0 = jnp.sum(ls_ref[2 * hh], -1, keepdims=True)
        l1 = jnp.sum(ls_ref[2 * hh + 1], -1, keepdims=True)
        o = acc_ref[2 * hh] / l0 - lam * (acc_ref[2 * hh + 1] / l1)
        o = o * lax.rsqrt(jnp.mean(o * o, -1, keepdims=True) + RMS_EPS) * subg_ref[...] * (1.0 - lambda_init)
        o_ref[:, hh * 2 * d:(hh + 1) * 2 * d] = o.astype(o_ref.dtype)


def _diff_attn(qkv, lam_p, subln_g, B, S, H, d, lambda_init, tq=256):
    T = B * S
    nq = S // tq
    hp = DIFF_HEADS_PER_STEP
    w = hp * 2 * d
    assert S % tq == 0 and H % hp == 0 and qkv.shape == (T, 3 * H * 2 * d)
    slopes = jnp.asarray(_alibi_slopes(H), F32)
    return pl.pallas_call(
        functools.partial(_diff_attn_kernel, d=d, scale=d ** -0.5, lambda_init=lambda_init),
        grid=(B, H // hp, nq),
        in_specs=[
            pl.BlockSpec(memory_space=pltpu.SMEM),
            pl.BlockSpec((4, d), lambda b, h, i: (0, 0)),
            pl.BlockSpec((1, 2 * d), lambda b, h, i: (0, 0)),
            pl.BlockSpec((tq, w), lambda b, h, i: (b * nq + i, h)),
            pl.BlockSpec((S, w), lambda b, h, i: (b, H // hp + h)),
            pl.BlockSpec((S, w), lambda b, h, i: (b, 2 * (H // hp) + h)),
        ],
        out_specs=pl.BlockSpec((tq, w), lambda b, h, i: (b * nq + i, h)),
        out_shape=jax.ShapeDtypeStruct((T, H * 2 * d), BF16),
        scratch_shapes=[pltpu.VMEM((2 * hp, nq, tq, tq), F32), pltpu.VMEM((hp, tq, 2 * tq), F32),
                        pltpu.VMEM((2 * hp, tq, tq), F32), pltpu.VMEM((2 * hp, tq, tq), F32),
                        pltpu.VMEM((2 * hp, tq, 2 * d), F32)],
        compiler_params=_params("parallel", "parallel", "arbitrary"),
        name="diff_attn",
    )(slopes, lam_p, subln_g, qkv, qkv, qkv)


MOBA_HEADS_PER_STEP = 4


def _moba_kernel(slopes_ref, q_ref, k_ref, v_ref, o_ref, km_ref, mb_ref, s_ref, bias_ref,
                 mx_ref, ls_ref, acc_ref, *, nb, n_sel, d, scale):
    hp = MOBA_HEADS_PER_STEP
    h0 = pl.program_id(1) * hp
    own = pl.program_id(2)
    bs = q_ref.shape[0]
    nbp = -(-nb // 8) * 8
    heads = range(hp)
    cols = [slice(hh * d, (hh + 1) * d) for hh in heads]
    slopes = [slopes_ref[h0 + hh] for hh in heads]

    @pl.when(own == 0)
    def _():
        km_ref[...] = jnp.zeros_like(km_ref)
        for hh in heads:
            for n in range(nb):
                km_ref[hh, n:n + 1, :] = jnp.mean(
                    k_ref[n * bs:(n + 1) * bs, cols[hh]].astype(F32), 0, keepdims=True)

    q = [q_ref[:, cols[hh]] for hh in heads]
    rel_i = _row_minus_col(bs, bs)
    block = lax.broadcasted_iota(jnp.int32, (nbp, bs), 0)
    for hh in heads:
        km = km_ref[hh]
        km_hi = km.astype(BF16)
        km_lo = (km - km_hi.astype(F32)).astype(BF16)
        gate = (_dot_nt(km_hi, q[hh]) + _dot_nt(km_lo, q[hh]))[0:nbp, :]
        rank = jnp.zeros((nbp, bs), jnp.int32)
        for m in range(nb - 1):
            gm = gate[m:m + 1, :]
            beats = (gm > gate) | ((gm == gate) & (block > m))
            rank = rank + beats.astype(jnp.int32) * (own > m).astype(jnp.int32)
        sel = (rank < n_sel) & (block < own)
        off = -slopes[hh] * ((own - block) * bs).astype(F32)
        mb_t = jnp.where(sel, off, NEG)
        mb = jnp.transpose(jnp.concatenate([mb_t, jnp.zeros((LANES - nbp, bs), F32)], axis=0))
        for n in range(nb - 1):
            mb_ref[hh, n] = jnp.broadcast_to(mb[:, n:n + 1], (bs, LANES))

    @pl.when(own == 0)
    def _():
        for hh in heads:
            bias_ref[hh] = -slopes[hh] * rel_i.astype(F32)

    def span_scores(c0, width, diagonal):
        rows = pl.ds(pl.multiple_of(c0 * bs, bs), width * bs)
        for hh in heads:
            qk = _dot_nt(q[hh], k_ref[rows, cols[hh]]) * scale
            chunks = []
            for w in range(width):
                sw = qk[:, w * bs:(w + 1) * bs] + bias_ref[hh]
                if diagonal and w == width - 1:
                    sw = jnp.where(rel_i >= 0, sw, NEG)
                else:
                    sw = sw + jnp.concatenate([mb_ref[hh, c0 + w]] * (bs // LANES), axis=1)
                s_ref[hh, c0 + w] = sw
                chunks.append(sw)
            mx = functools.reduce(jnp.maximum, chunks)
            mx_ref[hh] = mx if diagonal else jnp.maximum(mx_ref[hh], mx)

    def span_accumulate(c0, width, diagonal):
        rows = pl.ds(pl.multiple_of(c0 * bs, bs), width * bs)
        for hh in heads:
            ps = [jnp.exp(s_ref[hh, c0 + w] - mx_ref[hh]) for w in range(width)]
            psum = functools.reduce(lambda a, b: a + b, ps)
            p = ps[0] if width == 1 else jnp.concatenate(ps, axis=1)
            pv = _dot(p.astype(BF16), v_ref[rows, cols[hh]])
            if diagonal:
                ls_ref[hh] = psum
                acc_ref[hh] = pv
            else:
                ls_ref[hh] += psum
                acc_ref[hh] += pv

    _causal_sweep(span_scores, own)
    for hh in heads:
        _row_max_bcast(mx_ref.at[hh])
    _causal_sweep(span_accumulate, own)

    for hh in heads:
        l = jnp.sum(ls_ref[hh], -1, keepdims=True)
        o_ref[:, cols[hh]] = (acc_ref[hh] / l).astype(o_ref.dtype)


def _moba_attn(qkv, B, S, H, d, bs):
    T = B * S
    nb = S // bs
    hp = MOBA_HEADS_PER_STEP
    assert S % bs == 0 and nb <= LANES and H % hp == 0 and qkv.shape == (T, 3 * H * d)
    n_sel = min(MOBA_TOPK, nb - 1)
    slopes = jnp.asarray(_alibi_slopes(H), F32)
    return pl.pallas_call(
        functools.partial(_moba_kernel, nb=nb, n_sel=n_sel, d=d, scale=d ** -0.5),
        grid=(B, H // hp, nb),
        in_specs=[
            pl.BlockSpec(memory_space=pltpu.SMEM),
            pl.BlockSpec((bs, hp * d), lambda b, h, i: (b * nb + i, h)),
            pl.BlockSpec((S, hp * d), lambda b, h, i: (b, H // hp + h)),
            pl.BlockSpec((S, hp * d), lambda b, h, i: (b, 2 * (H // hp) + h)),
        ],
        out_specs=pl.BlockSpec((bs, hp * d), lambda b, h, i: (b * nb + i, h)),
        out_shape=jax.ShapeDtypeStruct((T, H * d), BF16),
        scratch_shapes=[pltpu.VMEM((hp, LANES, d), F32), pltpu.VMEM((hp, nb, bs, LANES), F32),
                        pltpu.VMEM((hp, nb, bs, bs), F32), pltpu.VMEM((hp, bs, bs), F32),
                        pltpu.VMEM((hp, bs, bs), F32), pltpu.VMEM((hp, bs, bs), F32),
                        pltpu.VMEM((hp, bs, d), F32)],
        compiler_params=_params("parallel", "parallel", "arbitrary"),
        name="moba_attn",
    )(slopes, qkv, qkv, qkv)


def _dsa_kernel(q_ref, k_ref, v_ref, qi_ref, tailq_ref, tailk_ref, o_ref,
                kib_ref, qis_ref, key_ref, thr_ref, s_ref, mx_ref, ls_ref, acc_ref, *, n_keep, tkc, scale):
    i = pl.program_id(1)
    tq = q_ref.shape[0]
    G, R, d = C_KV_HEADS, C_HEADS // C_KV_HEADS, C_HEAD_DIM
    qstart = i * tq
    nkc = (qstart + tq + tkc - 1) // tkc

    @pl.when(i == 0)
    def _():
        kib_ref[...] = tailk_ref[:, 0:IDX_DIM].astype(BF16)

    w_t = (jnp.transpose(tailq_ref[...])[IDX_DIM:IDX_DIM + IDX_HEADS, :]
           * (IDX_HEADS ** -0.5 * IDX_DIM ** -0.5))
    key_minus_query = _row_minus_col(tkc, tq)

    for hh in range(IDX_HEADS):
        qis_ref[hh * tq:(hh + 1) * tq, :] = qi_ref[:, hh * IDX_DIM:(hh + 1) * IDX_DIM]

    def score_chunk(c, carry):
        kc = kib_ref[pl.ds(pl.multiple_of(c * tkc, tkc), tkc), :]
        dots = _dot_nt(kc, qis_ref[...])
        acc = jnp.zeros((tkc, tq), F32)
        for hh in range(IDX_HEADS):
            acc = acc + w_t[hh:hh + 1, :] * jnp.maximum(dots[:, hh * tq:(hh + 1) * tq], 0.0)
        bits = lax.bitcast_convert_type(acc, jnp.int32)
        key = bits ^ ((bits >> 31) & jnp.int32(0x7FFFFFFF))
        causal = key_minus_query + (c * tkc - qstart) <= 0
        key_ref[c] = jnp.where(causal, key, jnp.int32(INT_MIN))
        return carry

    lax.fori_loop(0, nkc, score_chunk, 0)

    def count(pred):
        def body(c, part):
            hit = pred(key_ref[c]).astype(jnp.int32)
            return part + jnp.sum(hit.reshape(tkc // 8, 8, tq), axis=0)
        part = lax.fori_loop(0, nkc, body, jnp.zeros((8, tq), jnp.int32))
        return jnp.sum(part, axis=0, keepdims=True)

    thr_ref[...] = jnp.full_like(thr_ref, INT_MIN + 1)

    @pl.when(qstart + tq > n_keep)
    def _():
        zero = jnp.zeros((1, tq), jnp.int32)
        thr0 = jnp.where(count(lambda k: k >= zero) >= n_keep, jnp.int32(0), jnp.int32(INT_MIN))

        def bit_step(b, thr):
            cand = thr | jnp.left_shift(jnp.int32(1), 30 - b)
            return jnp.where(count(lambda k: k >= cand) >= n_keep, cand, thr)

        thr = jnp.maximum(lax.fori_loop(0, 31, bit_step, thr0), jnp.int32(INT_MIN + 1))
        thr_ref[...] = thr

        @pl.when(jnp.max(count(lambda k: k >= thr)) > n_keep)
        def _():
            room = (n_keep - count(lambda k: k > thr)).astype(F32)
            lower_tri = jnp.where(_row_minus_col(tkc, tkc) >= 0, 1.0, 0.0).astype(BF16)

            def drop_excess(c, seen):
                key = key_ref[c]
                tie = key == thr
                tie_f = jnp.where(tie, 1.0, 0.0)
                rank = seen + _dot(lower_tri, tie_f.astype(BF16))
                key_ref[c] = jnp.where(tie & (rank > room), jnp.int32(INT_MIN), key)
                return seen + jnp.sum(tie_f, axis=0, keepdims=True)

            lax.fori_loop(0, nkc, drop_excess, jnp.zeros((1, tq), F32))

    thr = thr_ref[...]
    mx_ref[...] = jnp.full_like(mx_ref, NEG)
    ls_ref[...] = jnp.zeros_like(ls_ref)
    acc_ref[...] = jnp.zeros_like(acc_ref)
    slopes = _alibi_slopes(C_HEADS).reshape(G, R)
    rel = _row_minus_col(tq, tkc).astype(F32)

    def score_pass(c, carry):
        rows = pl.ds(pl.multiple_of(c * tkc, tkc), tkc)
        mask = jnp.transpose(jnp.where(key_ref[c] >= thr, 0.0, NEG))
        dist = rel + (qstart - c * tkc).astype(F32)
        for g in range(G):
            qs = jnp.concatenate(
                [q_ref[:, (g * R + r) * d:(g * R + r + 1) * d] for r in range(R)], axis=0)
            bias = jnp.concatenate([mask - float(slopes[g, r]) * dist for r in range(R)], axis=0)
            s = _dot_nt(qs, k_ref[rows, g * d:(g + 1) * d]) * scale + bias
            s_ref[g, c] = s
            mx_ref[g] = jnp.maximum(mx_ref[g], s)
        return carry

    def acc_pass(c, carry):
        rows = pl.ds(pl.multiple_of(c * tkc, tkc), tkc)
        for g in range(G):
            p = jnp.exp(s_ref[g, c] - mx_ref[g])
            ls_ref[g] += p
            acc_ref[g] += _dot(p.astype(BF16), v_ref[rows, g * d:(g + 1) * d])
        return carry

    lax.fori_loop(0, nkc, score_pass, 0)
    for g in range(G):
        _row_max_bcast(mx_ref.at[g])
    lax.fori_loop(0, nkc, acc_pass, 0)

    for g in range(G):
        o = acc_ref[g] / jnp.sum(ls_ref[g], -1, keepdims=True)
        for r in range(R):
            o_ref[:, (g * R + r) * d:(g * R + r + 1) * d] = o[r * tq:(r + 1) * tq].astype(o_ref.dtype)


def _dsa_attn(main, tail, B, S, tq=128, tkc=256):
    T = B * S
    nq = S // tq
    G, H, d = C_KV_HEADS, C_HEADS, C_HEAD_DIM
    R = H // G
    c_q, c_kv, c_iq = H * d, G * d, IDX_HEADS * IDX_DIM
    assert S % tq == 0 and S % tkc == 0 and tkc % tq == 0
    assert main.shape == (T, c_q + 2 * c_kv + c_iq) and tail.shape == (T, LANES)
    assert c_q % c_kv == 0 and (c_q + 2 * c_kv) % c_iq == 0
    n_keep = min(DSA_TOPK_MAX, S // 4)
    return pl.pallas_call(
        functools.partial(_dsa_kernel, n_keep=n_keep, tkc=tkc, scale=d ** -0.5),
        grid=(B, nq),
        in_specs=[
            pl.BlockSpec((tq, c_q), lambda b, i: (b * nq + i, 0)),
            pl.BlockSpec((S, c_kv), lambda b, i: (b, c_q // c_kv)),
            pl.BlockSpec((S, c_kv), lambda b, i: (b, c_q // c_kv + 1)),
            pl.BlockSpec((tq, c_iq), lambda b, i: (b * nq + i, (c_q + 2 * c_kv) // c_iq)),
            pl.BlockSpec((tq, LANES), lambda b, i: (b * nq + i, 0)),
            pl.BlockSpec((S, LANES), lambda b, i: (b, 0)),
        ],
        out_specs=pl.BlockSpec((tq, c_q), lambda b, i: (b * nq + i, 0)),
        out_shape=jax.ShapeDtypeStruct((T, c_q), BF16),
        scratch_shapes=[pltpu.VMEM((S, IDX_DIM), BF16), pltpu.VMEM((IDX_HEADS * tq, IDX_DIM), BF16),
                        pltpu.VMEM((S // tkc, tkc, tq), jnp.int32),
                        pltpu.VMEM((1, tq), jnp.int32),
                        pltpu.VMEM((G, S // tkc, R * tq, tkc), F32),
                        pltpu.VMEM((G, R * tq, tkc), F32), pltpu.VMEM((G, R * tq, tkc), F32),
                        pltpu.VMEM((G, R * tq, d), F32)],
        compiler_params=_params("parallel", "arbitrary"),
        name="dsa_attn",
    )(main, main, main, main, tail, tail)


def kernel(x, p, ffn1_w_in, ffn1_w_out, ffn2_w_in, ffn2_w_out, ln_g, ln_b, ple_w_gate, ple_w_proj,
           a_w_in, a_w_out, a_lam_q1, a_lam_k1, a_lam_q2, a_lam_k2, a_subln_g, b_w_in, b_w_out,
           c_w_in, c_w_out):
    B, S, D = x.shape
    depth = p.shape[0]
    T = B * S
    alpha = (2.0 * depth) ** 0.25
    bf = lambda w: w.astype(BF16)
    ple_g, ple_p = bf(ple_w_gate), bf(ple_w_proj)
    a_in, a_out, b_in, b_out, c_out = bf(a_w_in), bf(a_w_out), bf(b_w_in), bf(b_w_out), bf(c_w_out)
    n_main = C_HEADS * C_HEAD_DIM + 2 * C_KV_HEADS * C_HEAD_DIM + IDX_HEADS * IDX_DIM
    c_in = bf(c_w_in)
    c_tail = c_w_in[:, :, n_main:]
    c_tail = bf(jnp.pad(c_tail, ((0, 0), (0, 0), (0, LANES - c_tail.shape[2]))))
    lng = ln_g.reshape(-1, 1, D).astype(F32)
    lnb = ln_b.reshape(-1, 1, D).astype(F32)
    n_ln = ln_g.shape[1]
    p = p.reshape(depth, T, -1)
    x = x.reshape(T, D)
    for i in range(depth):
        m, j = i % N_MIXERS, i // N_MIXERS
        x = _ffn(x, ffn1_w_in, ffn1_w_out, i, lng, lnb, i * n_ln, alpha)
        if m == 0:
            d = A_HEAD_DIM
            H = D // (2 * d)
            lambda_init = 0.8 - 0.6 * math.exp(-0.3 * i)
            qkv = _proj(x, a_in, j, a_in.shape[2], BF16)
            lam_p = jnp.stack([a_lam_q1[j], a_lam_k1[j], a_lam_q2[j], a_lam_k2[j]]).astype(F32)
            o = _diff_attn(qkv, lam_p, a_subln_g[j].reshape(1, 2 * d).astype(F32), B, S, H, d, lambda_init)
            w_out = a_out
        elif m == 1:
            qkv = _proj(x, b_in, j, b_in.shape[2], BF16)
            o = _moba_attn(qkv, B, S, B_HEADS, B_HEAD_DIM, MOBA_BLOCK)
            w_out = b_out
        else:
            main = _proj(x, c_in, j, n_main, BF16)
            tail = _proj(x, c_tail, j, LANES, F32)
            o = _dsa_attn(main, tail, B, S)
            w_out = c_out
        x = _out_ln(x, o, w_out, j, lng, lnb, i * n_ln + 1, alpha)
        x = _ffn(x, ffn2_w_in, ffn2_w_out, i, lng, lnb, i * n_ln + 2, alpha)
        x = _ple(x, p, ple_g, ple_p, i, lng, lnb, i * n_ln + 3, alpha)
    return x.reshape(B, S, D)
```

```python
import functools
import math

import numpy as np
import jax
import jax.numpy as jnp
from jax import lax
from jax.experimental import pallas as pl
from jax.experimental.pallas import tpu as pltpu

F32 = jnp.float32
BF16 = jnp.bfloat16

N_MIXERS = 3
A_HEAD_DIM = 128
B_HEADS = 16
B_HEAD_DIM = 128
MOBA_BLOCK = 256
MOBA_TOPK = 3
C_HEADS = 16
C_KV_HEADS = 4
C_HEAD_DIM = 128
IDX_HEADS = 16
IDX_DIM = 64
DSA_TOPK_MAX = 256
LN_EPS = 1e-5
RMS_EPS = 1e-6

LANES = 128
NEG = -1e30
INT_MIN = -(2 ** 31)
LOG2E = math.log2(math.e)
VMEM_LIMIT = 60 * 1024 * 1024
ROW_SPLIT = 128


def _alibi_slopes(n):
    return 2.0 ** (-8.0 * np.arange(1, n + 1, dtype=np.float32) / n)


def _params(*sem):
    return pltpu.CompilerParams(dimension_semantics=sem, vmem_limit_bytes=VMEM_LIMIT)


def _dot(a, b):
    return jnp.dot(a, b, preferred_element_type=F32)


def _dot_nt(a, b):
    return lax.dot_general(a, b, (((1,), (1,)), ((), ())), preferred_element_type=F32)


def _layer_norm(z, g, b):
    mu = jnp.mean(z, -1, keepdims=True)
    zc = z - mu
    var = jnp.mean(zc * zc, -1, keepdims=True)
    return zc * lax.rsqrt(var + LN_EPS) * g + b


def _ffn_kernel(x_ref, wg_ref, wu_ref, wo_ref, g_ref, b_ref, o_ref, xb_ref, *, alpha):
    j = pl.program_id(1)

    @pl.when(j == 0)
    def _():
        xb_ref[...] = x_ref[...].astype(BF16)
        o_ref[...] = jnp.zeros_like(o_ref)

    xb = xb_ref[...]
    gate = _dot(xb, wg_ref[...].astype(BF16))
    up = _dot(xb, wu_ref[...].astype(BF16))
    h = (gate * jax.nn.sigmoid(gate) * up).astype(BF16)
    o_ref[...] += _dot(h, wo_ref[...].astype(BF16))

    @pl.when(j == pl.num_programs(1) - 1)
    def _():
        for r in range(0, o_ref.shape[0], ROW_SPLIT):
            rows = slice(r, r + ROW_SPLIT)
            z = alpha * x_ref[rows, :] + 0.5 * o_ref[rows, :]
            o_ref[rows, :] = _layer_norm(z, g_ref[...], b_ref[...])


def _ln_specs(ln, D, ngrid):
    if ngrid == 1:
        return pl.BlockSpec((None, 1, D), lambda i: (ln, 0, 0))
    return pl.BlockSpec((None, 1, D), lambda i, j: (ln, 0, 0))


def _ffn(x, w_in, w_out, layer, ln_g, ln_b, ln, alpha, tm=1024, tf=256):
    T, D = x.shape
    F = w_out.shape[1]
    nf = F // tf
    assert T % tm == 0 and F % tf == 0 and w_in.shape[1:] == (D, 2 * F)
    return pl.pallas_call(
        functools.partial(_ffn_kernel, alpha=alpha),
        grid=(T // tm, nf),
        in_specs=[
            pl.BlockSpec((tm, D), lambda i, j: (i, 0)),
            pl.BlockSpec((None, D, tf), lambda i, j: (layer, 0, j)),
            pl.BlockSpec((None, D, tf), lambda i, j: (layer, 0, j + nf)),
            pl.BlockSpec((None, tf, D), lambda i, j: (layer, j, 0)),
            _ln_specs(ln, D, 2),
            _ln_specs(ln, D, 2),
        ],
        out_specs=pl.BlockSpec((tm, D), lambda i, j: (i, 0)),
        out_shape=jax.ShapeDtypeStruct((T, D), F32),
        scratch_shapes=[pltpu.VMEM((tm, D), BF16)],
        compiler_params=_params("parallel", "arbitrary"),
        name="ffn",
    )(x, w_in, w_in, w_out, ln_g, ln_b)


def _proj_kernel(x_ref, w_ref, o_ref, xb_ref):
    @pl.when(pl.program_id(1) == 0)
    def _():
        xb_ref[...] = x_ref[...].astype(BF16)

    o_ref[...] = _dot(xb_ref[...], w_ref[...]).astype(o_ref.dtype)


def _proj(x, w, layer, n_cols, out_dtype, tm=1024, tn=1024):
    T, D = x.shape
    tm, tn = min(tm, T), min(tn, n_cols)
    assert T % tm == 0 and n_cols % tn == 0 and w.shape[1] == D and w.shape[2] >= n_cols
    return pl.pallas_call(
        _proj_kernel,
        grid=(T // tm, n_cols // tn),
        in_specs=[
            pl.BlockSpec((tm, D), lambda i, j: (i, 0)),
            pl.BlockSpec((None, D, tn), lambda i, j: (layer, 0, j)),
        ],
        out_specs=pl.BlockSpec((tm, tn), lambda i, j: (i, j)),
        out_shape=jax.ShapeDtypeStruct((T, n_cols), out_dtype),
        scratch_shapes=[pltpu.VMEM((tm, D), BF16)],
        compiler_params=_params("parallel", "arbitrary"),
        name="proj",
    )(x, w)


def _out_ln_kernel(x_ref, o_ref, w_ref, g_ref, b_ref, y_ref, *, alpha):
    for r in range(0, x_ref.shape[0], ROW_SPLIT):
        rows = slice(r, r + ROW_SPLIT)
        z = alpha * x_ref[rows, :] + _dot(o_ref[rows, :], w_ref[...])
        y_ref[rows, :] = _layer_norm(z, g_ref[...], b_ref[...])


def _out_ln(x, o, w, layer, ln_g, ln_b, ln, alpha, tm=512):
    T, D = x.shape
    K = o.shape[1]
    assert T % tm == 0 and tm % ROW_SPLIT == 0 and w.shape[1:] == (K, D)
    return pl.pallas_call(
        functools.partial(_out_ln_kernel, alpha=alpha),
        grid=(T // tm,),
        in_specs=[
            pl.BlockSpec((tm, D), lambda i: (i, 0)),
            pl.BlockSpec((tm, K), lambda i: (i, 0)),
            pl.BlockSpec((None, K, D), lambda i: (layer, 0, 0)),
            _ln_specs(ln, D, 1),
            _ln_specs(ln, D, 1),
        ],
        out_specs=pl.BlockSpec((tm, D), lambda i: (i, 0)),
        out_shape=jax.ShapeDtypeStruct((T, D), F32),
        compiler_params=_params("parallel"),
        name="out_ln",
    )(x, o, w, ln_g, ln_b)


def _ple_kernel(x_ref, p_ref, wg_ref, wp_ref, g_ref, b_ref, y_ref, *, alpha):
    for r in range(0, x_ref.shape[0], ROW_SPLIT):
        rows = slice(r, r + ROW_SPLIT)
        x = x_ref[rows, :]
        gate = jax.nn.sigmoid(_dot(x.astype(BF16), wg_ref[...]))
        emb = _dot(p_ref[rows, :].astype(BF16), wp_ref[...])
        y_ref[rows, :] = _layer_norm(alpha * x + gate * emb, g_ref[...], b_ref[...])


def _ple(x, p, wg, wp, layer, ln_g, ln_b, ln, alpha, tm=512):
    T, D = x.shape
    P = p.shape[2]
    assert T % tm == 0 and tm % ROW_SPLIT == 0
    return pl.pallas_call(
        functools.partial(_ple_kernel, alpha=alpha),
        grid=(T // tm,),
        in_specs=[
            pl.BlockSpec((tm, D), lambda i: (i, 0)),
            pl.BlockSpec((None, tm, P), lambda i: (layer, i, 0)),
            pl.BlockSpec((None, D, D), lambda i: (layer, 0, 0)),
            pl.BlockSpec((None, P, D), lambda i: (layer, 0, 0)),
            _ln_specs(ln, D, 1),
            _ln_specs(ln, D, 1),
        ],
        out_specs=pl.BlockSpec((tm, D), lambda i: (i, 0)),
        out_shape=jax.ShapeDtypeStruct((T, D), F32),
        compiler_params=_params("parallel"),
        name="ple",
    )(x, p, wg, wp, ln_g, ln_b)


def _row_max_bcast(mx_ref):
    m = jnp.max(mx_ref[...], -1, keepdims=True)
    mx_ref[...] = jnp.broadcast_to(m, mx_ref.shape)


def _causal_sweep(fn, qi):
    @pl.when(qi % 2 == 1)
    def _():
        fn(qi - 1, 2, True)

    @pl.when(qi % 2 == 0)
    def _():
        fn(qi, 1, True)

    def body(j, carry):
        fn(2 * j, 2, False)
        return carry

    lax.fori_loop(0, qi // 2, body, 0)


def _row_minus_col(rows, cols):
    r = lax.broadcasted_iota(jnp.int32, (rows, cols), 0)
    c = lax.broadcasted_iota(jnp.int32, (rows, cols), 1)
    return r - c


DIFF_HEADS_PER_STEP = 4


def _diff_attn_kernel(slopes_ref, lam_ref, subg_ref, q_ref, k_ref, v_ref, o_ref,
                      s_ref, bias_ref, mx_ref, ls_ref, acc_ref, *, d, scale, lambda_init):
    hp = DIFF_HEADS_PER_STEP
    h0 = pl.program_id(1) * hp
    qi = pl.program_id(2)
    tq = q_ref.shape[0]
    units = range(2 * hp)
    slopes = [slopes_ref[h0 + hh] * (1.0 / scale) for hh in range(hp)]
    q = [q_ref[:, u * d:(u + 1) * d] for u in units]

    @pl.when(qi == 0)
    def _():
        rel = _row_minus_col(tq, 2 * tq).astype(F32)
        for hh in range(hp):
            bias_ref[hh] = -slopes[hh] * rel

    def span_scores(c0, width, diagonal):
        rows = pl.ds(pl.multiple_of(c0 * tq, tq), width * tq)
        for hh in range(hp):
            bias = bias_ref[hh, :, :width * tq] - slopes[hh] * ((qi - c0) * tq).astype(F32)
            for u in (2 * hh, 2 * hh + 1):
                s = _dot_nt(q[u], k_ref[rows, u * d:(u + 1) * d]) + bias
                if diagonal:
                    s = jnp.where(_row_minus_col(tq, width * tq) + (qi - c0) * tq >= 0, s, NEG)
                chunks = [s[:, w * tq:(w + 1) * tq] for w in range(width)]
                for w in range(width):
                    s_ref[u, c0 + w] = chunks[w]
                mx = functools.reduce(jnp.maximum, chunks)
                mx_ref[u] = mx if diagonal else jnp.maximum(mx_ref[u], mx)

    def span_accumulate(c0, width, diagonal):
        rows = pl.ds(pl.multiple_of(c0 * tq, tq), width * tq)
        for u in units:
            hh = u // 2
            ps = [jnp.exp2((s_ref[u, c0 + w] - mx_ref[u]) * (scale * LOG2E)) for w in range(width)]
            psum = functools.reduce(lambda a, b: a + b, ps)
            p = ps[0] if width == 1 else jnp.concatenate(ps, axis=1)
            pv = _dot(p.astype(BF16), v_ref[rows, hh * 2 * d:(hh + 1) * 2 * d])
            if diagonal:
                ls_ref[u] = psum
                acc_ref[u] = pv
            else:
                ls_ref[u] += psum
                acc_ref[u] += pv

    _causal_sweep(span_scores, qi)
    for u in units:
        _row_max_bcast(mx_ref.at[u])
    _causal_sweep(span_accumulate, qi)

    lam_p = lam_ref[...]
    lam = (jnp.exp(jnp.sum(lam_p[0:1] * lam_p[1:2], -1, keepdims=True))
           - jnp.exp(jnp.sum(lam_p[2:3] * lam_p[3:4], -1, keepdims=True)) + lambda_init)
    for hh in range(hp):
        l0 = jnp.sum(ls_ref[2 * hh], -1, keepdims=True)
        l1 = jnp.sum(ls_ref[2 * hh + 1], -1, keepdims=True)
        o = acc_ref[2 * hh] / l0 - lam * (acc_ref[2 * hh + 1] / l1)
        o = o * lax.rsqrt(jnp.mean(o * o, -1, keepdims=True) + RMS_EPS) * subg_ref[...] * (1.0 - lambda_init)
        o_ref[:, hh * 2 * d:(hh + 1) * 2 * d] = o.astype(o_ref.dtype)


def _diff_attn(qkv, lam_p, subln_g, B, S, H, d, lambda_init, tq=256):
    T = B * S
    nq = S // tq
    hp = DIFF_HEADS_PER_STEP
    w = hp * 2 * d
    assert S % tq == 0 and H % hp == 0 and qkv.shape == (T, 3 * H * 2 * d)
    slopes = jnp.asarray(_alibi_slopes(H), F32)
    return pl.pallas_call(
        functools.partial(_diff_attn_kernel, d=d, scale=d ** -0.5, lambda_init=lambda_init),
        grid=(B, H // hp, nq),
        in_specs=[
            pl.BlockSpec(memory_space=pltpu.SMEM),
            pl.BlockSpec((4, d), lambda b, h, i: (0, 0)),
            pl.BlockSpec((1, 2 * d), lambda b, h, i: (0, 0)),
            pl.BlockSpec((tq, w), lambda b, h, i: (b * nq + i, h)),
            pl.BlockSpec((S, w), lambda b, h, i: (b, H // hp + h)),
            pl.BlockSpec((S, w), lambda b, h, i: (b, 2 * (H // hp) + h)),
        ],
        out_specs=pl.BlockSpec((tq, w), lambda b, h, i: (b * nq + i, h)),
        out_shape=jax.ShapeDtypeStruct((T, H * 2 * d), BF16),
        scratch_shapes=[pltpu.VMEM((2 * hp, nq, tq, tq), F32), pltpu.VMEM((hp, tq, 2 * tq), F32),
                        pltpu.VMEM((2 * hp, tq, tq), F32), pltpu.VMEM((2 * hp, tq, tq), F32),
                        pltpu.VMEM((2 * hp, tq, 2 * d), F32)],
        compiler_params=_params("parallel", "parallel", "arbitrary"),
        name="diff_attn",
    )(slopes, lam_p, subln_g, qkv, qkv, qkv)


MOBA_HEADS_PER_STEP = 4


def _moba_kernel(slopes_ref, q_ref, k_ref, v_ref, o_ref, km_ref, mb_ref, s_ref, bias_ref,
                 mx_ref, ls_ref, acc_ref, *, nb, n_sel, d, scale):
    hp = MOBA_HEADS_PER_STEP
    h0 = pl.program_id(1) * hp
    own = pl.program_id(2)
    bs = q_ref.shape[0]
    nbp = -(-nb // 8) * 8
    heads = range(hp)
    cols = [slice(hh * d, (hh + 1) * d) for hh in heads]
    slopes = [slopes_ref[h0 + hh] * (1.0 / scale) for hh in heads]

    @pl.when(own == 0)
    def _():
        km_ref[...] = jnp.zeros_like(km_ref)
        for hh in heads:
            for n in range(nb):
                km_ref[hh, n:n + 1, :] = jnp.mean(
                    k_ref[n * bs:(n + 1) * bs, cols[hh]].astype(F32), 0, keepdims=True)

    q = [q_ref[:, cols[hh]] for hh in heads]
    rel_i = _row_minus_col(bs, bs)
    block = lax.broadcasted_iota(jnp.int32, (nbp, bs), 0)
    for hh in heads:
        km = km_ref[hh]
        km_hi = km.astype(BF16)
        km_lo = (km - km_hi.astype(F32)).astype(BF16)
        gate = (_dot_nt(km_hi, q[hh]) + _dot_nt(km_lo, q[hh]))[0:nbp, :]
        rank = jnp.zeros((nbp, bs), jnp.int32)
        for m in range(nb - 1):
            gm = gate[m:m + 1, :]
            beats = (gm > gate) | ((gm == gate) & (block > m))
            rank = rank + beats.astype(jnp.int32) * (own > m).astype(jnp.int32)
        sel = (rank < n_sel) & (block < own)
        off = -slopes[hh] * ((own - block) * bs).astype(F32)
        mb_t = jnp.where(sel, off, NEG)
        mb = jnp.transpose(jnp.concatenate([mb_t, jnp.zeros((LANES - nbp, bs), F32)], axis=0))
        for n in range(nb - 1):
            mb_ref[hh, n] = jnp.broadcast_to(mb[:, n:n + 1], (bs, LANES))

    @pl.when(own == 0)
    def _():
        for hh in heads:
            bias_ref[hh] = -slopes[hh] * rel_i.astype(F32)

    def span_scores(c0, width, diagonal):
        rows = pl.ds(pl.multiple_of(c0 * bs, bs), width * bs)
        for hh in heads:
            qk = _dot_nt(q[hh], k_ref[rows, cols[hh]])
            chunks = []
            for w in range(width):
                sw = qk[:, w * bs:(w + 1) * bs] + bias_ref[hh]
                if diagonal and w == width - 1:
                    sw = jnp.where(rel_i >= 0, sw, NEG)
                else:
                    sw = sw + jnp.concatenate([mb_ref[hh, c0 + w]] * (bs // LANES), axis=1)
                s_ref[hh, c0 + w] = sw
                chunks.append(sw)
            mx = functools.reduce(jnp.maximum, chunks)
            mx_ref[hh] = mx if diagonal else jnp.maximum(mx_ref[hh], mx)

    def span_accumulate(c0, width, diagonal):
        rows = pl.ds(pl.multiple_of(c0 * bs, bs), width * bs)
        for hh in heads:
            ps = [jnp.exp2((s_ref[hh, c0 + w] - mx_ref[hh]) * (scale * LOG2E)) for w in range(width)]
            psum = functools.reduce(lambda a, b: a + b, ps)
            p = ps[0] if width == 1 else jnp.concatenate(ps, axis=1)
            pv = _dot(p.astype(BF16), v_ref[rows, cols[hh]])
            if diagonal:
                ls_ref[hh] = psum
                acc_ref[hh] = pv
            else:
                ls_ref[hh] += psum
                acc_ref[hh] += pv

    _causal_sweep(span_scores, own)
    for hh in heads:
        _row_max_bcast(mx_ref.at[hh])
    _causal_sweep(span_accumulate, own)

    for hh in heads:
        l = jnp.sum(ls_ref[hh], -1, keepdims=True)
        o_ref[:, cols[hh]] = (acc_ref[hh] / l).astype(o_ref.dtype)


def _moba_attn(qkv, B, S, H, d, bs):
    T = B * S
    nb = S // bs
    hp = MOBA_HEADS_PER_STEP
    assert S % bs == 0 and nb <= LANES and H % hp == 0 and qkv.shape == (T, 3 * H * d)
    n_sel = min(MOBA_TOPK, nb - 1)
    slopes = jnp.asarray(_alibi_slopes(H), F32)
    return pl.pallas_call(
        functools.partial(_moba_kernel, nb=nb, n_sel=n_sel, d=d, scale=d ** -0.5),
        grid=(B, H // hp, nb),
        in_specs=[
            pl.BlockSpec(memory_space=pltpu.SMEM),
            pl.BlockSpec((bs, hp * d), lambda b, h, i: (b * nb + i, h)),
            pl.BlockSpec((S, hp * d), lambda b, h, i: (b, H // hp + h)),
            pl.BlockSpec((S, hp * d), lambda b, h, i: (b, 2 * (H // hp) + h)),
        ],
        out_specs=pl.BlockSpec((bs, hp * d), lambda b, h, i: (b * nb + i, h)),
        out_shape=jax.ShapeDtypeStruct((T, H * d), BF16),
        scratch_shapes=[pltpu.VMEM((hp, LANES, d), F32), pltpu.VMEM((hp, nb, bs, LANES), F32),
                        pltpu.VMEM((hp, nb, bs, bs), F32), pltpu.VMEM((hp, bs, bs), F32),
                        pltpu.VMEM((hp, bs, bs), F32), pltpu.VMEM((hp, bs, bs), F32),
                        pltpu.VMEM((hp, bs, d), F32)],
        compiler_params=_params("parallel", "parallel", "arbitrary"),
        name="moba_attn",
    )(slopes, qkv, qkv, qkv)


def _dsa_kernel(q_ref, k_ref, v_ref, qi_ref, tailq_ref, tailk_ref, o_ref,
                kib_ref, qis_ref, key_ref, thr_ref, s_ref, mx_ref, ls_ref, acc_ref, *, n_keep, tkc, scale):
    i = pl.program_id(1)
    tq = q_ref.shape[0]
    G, R, d = C_KV_HEADS, C_HEADS // C_KV_HEADS, C_HEAD_DIM
    qstart = i * tq
    nkc = (qstart + tq + tkc - 1) // tkc

    @pl.when(i == 0)
    def _():
        kib_ref[...] = tailk_ref[:, 0:IDX_DIM].astype(BF16)

    w_t = (jnp.transpose(tailq_ref[...])[IDX_DIM:IDX_DIM + IDX_HEADS, :]
           * (IDX_HEADS ** -0.5 * IDX_DIM ** -0.5))
    key_minus_query = _row_minus_col(tkc, tq)

    for hh in range(IDX_HEADS):
        qis_ref[hh * tq:(hh + 1) * tq, :] = qi_ref[:, hh * IDX_DIM:(hh + 1) * IDX_DIM]

    def score_chunk(c, carry):
        kc = kib_ref[pl.ds(pl.multiple_of(c * tkc, tkc), tkc), :]
        dots = _dot_nt(kc, qis_ref[...])
        acc = jnp.zeros((tkc, tq), F32)
        for hh in range(IDX_HEADS):
            acc = acc + w_t[hh:hh + 1, :] * jnp.maximum(dots[:, hh * tq:(hh + 1) * tq], 0.0)
        bits = lax.bitcast_convert_type(acc, jnp.int32)
        key = bits ^ ((bits >> 31) & jnp.int32(0x7FFFFFFF))
        causal = key_minus_query + (c * tkc - qstart) <= 0
        key_ref[c] = jnp.where(causal, key, jnp.int32(INT_MIN))
        return carry

    lax.fori_loop(0, nkc, score_chunk, 0)

    def count(pred):
        def body(c, part):
            hit = pred(key_ref[c]).astype(jnp.int32)
            return part + jnp.sum(hit.reshape(tkc // 8, 8, tq), axis=0)
        part = lax.fori_loop(0, nkc, body, jnp.zeros((8, tq), jnp.int32))
        return jnp.sum(part, axis=0, keepdims=True)

    thr_ref[...] = jnp.full_like(thr_ref, INT_MIN + 1)

    @pl.when(qstart + tq > n_keep)
    def _():
        zero = jnp.zeros((1, tq), jnp.int32)
        thr0 = jnp.where(count(lambda k: k >= zero) >= n_keep, jnp.int32(0), jnp.int32(INT_MIN))

        def bit_step(b, thr):
            cand = thr | jnp.left_shift(jnp.int32(1), 30 - b)
            return jnp.where(count(lambda k: k >= cand) >= n_keep, cand, thr)

        thr = jnp.maximum(lax.fori_loop(0, 31, bit_step, thr0), jnp.int32(INT_MIN + 1))
        thr_ref[...] = thr

        @pl.when(jnp.max(count(lambda k: k >= thr)) > n_keep)
        def _():
            room = (n_keep - count(lambda k: k > thr)).astype(F32)
            lower_tri = jnp.where(_row_minus_col(tkc, tkc) >= 0, 1.0, 0.0).astype(BF16)

            def drop_excess(c, seen):
                key = key_ref[c]
                tie = key == thr
                tie_f = jnp.where(tie, 1.0, 0.0)
                rank = seen + _dot(lower_tri, tie_f.astype(BF16))
                key_ref[c] = jnp.where(tie & (rank > room), jnp.int32(INT_MIN), key)
                return seen + jnp.sum(tie_f, axis=0, keepdims=True)

            lax.fori_loop(0, nkc, drop_excess, jnp.zeros((1, tq), F32))

    thr = thr_ref[...]
    slopes = _alibi_slopes(C_HEADS).reshape(G, R)
    rel = _row_minus_col(tq, tkc).astype(F32)

    def score_pass(c, first):
        rows = pl.ds(pl.multiple_of(c * tkc, tkc), tkc)
        mask = jnp.transpose(jnp.where(key_ref[c] >= thr, 0.0, NEG))
        dist = rel + (qstart - c * tkc).astype(F32)
        for g in range(G):
            qs = jnp.concatenate(
                [q_ref[:, (g * R + r) * d:(g * R + r + 1) * d] for r in range(R)], axis=0)
            bias = jnp.concatenate([mask - float(slopes[g, r] / scale) * dist for r in range(R)], axis=0)
            s = _dot_nt(qs, k_ref[rows, g * d:(g + 1) * d]) + bias
            s_ref[g, c] = s
            mx_ref[g] = s if first else jnp.maximum(mx_ref[g], s)

    def acc_pass(c, first):
        rows = pl.ds(pl.multiple_of(c * tkc, tkc), tkc)
        for g in range(G):
            p = jnp.exp2((s_ref[g, c] - mx_ref[g]) * (scale * LOG2E))
            pv = _dot(p.astype(BF16), v_ref[rows, g * d:(g + 1) * d])
            if first:
                ls_ref[g] = p
                acc_ref[g] = pv
            else:
                ls_ref[g] += p
                acc_ref[g] += pv

    def sweep(fn):
        fn(0, True)

        def body(c, carry):
            fn(c, False)
            return carry

        lax.fori_loop(1, nkc, body, 0)

    sweep(score_pass)
    for g in range(G):
        _row_max_bcast(mx_ref.at[g])
    sweep(acc_pass)

    for g in range(G):
        o = acc_ref[g] / jnp.sum(ls_ref[g], -1, keepdims=True)
        for r in range(R):
            o_ref[:, (g * R + r) * d:(g * R + r + 1) * d] = o[r * tq:(r + 1) * tq].astype(o_ref.dtype)


def _dsa_attn(main, tail, B, S, tq=128, tkc=256):
    T = B * S
    nq = S // tq
    G, H, d = C_KV_HEADS, C_HEADS, C_HEAD_DIM
    R = H // G
    c_q, c_kv, c_iq = H * d, G * d, IDX_HEADS * IDX_DIM
    assert S % tq == 0 and S % tkc == 0 and tkc % tq == 0
    assert main.shape == (T, c_q + 2 * c_kv + c_iq) and tail.shape == (T, LANES)
    assert c_q % c_kv == 0 and (c_q + 2 * c_kv) % c_iq == 0
    n_keep = min(DSA_TOPK_MAX, S // 4)
    return pl.pallas_call(
        functools.partial(_dsa_kernel, n_keep=n_keep, tkc=tkc, scale=d ** -0.5),
        grid=(B, nq),
        in_specs=[
            pl.BlockSpec((tq, c_q), lambda b, i: (b * nq + i, 0)),
            pl.BlockSpec((S, c_kv), lambda b, i: (b, c_q // c_kv)),
            pl.BlockSpec((S, c_kv), lambda b, i: (b, c_q // c_kv + 1)),
            pl.BlockSpec((tq, c_iq), lambda b, i: (b * nq + i, (c_q + 2 * c_kv) // c_iq)),
            pl.BlockSpec((tq, LANES), lambda b, i: (b * nq + i, 0)),
            pl.BlockSpec((S, LANES), lambda b, i: (b, 0)),
        ],
        out_specs=pl.BlockSpec((tq, c_q), lambda b, i: (b * nq + i, 0)),
        out_shape=jax.ShapeDtypeStruct((T, c_q), BF16),
        scratch_shapes=[pltpu.VMEM((S, IDX_DIM), BF16), pltpu.VMEM((IDX_HEADS * tq, IDX_DIM), BF16),
                        pltpu.VMEM((S // tkc, tkc, tq), jnp.int32),
                        pltpu.VMEM((1, tq), jnp.int32),
                        pltpu.VMEM((G, S // tkc, R * tq, tkc), F32),
                        pltpu.VMEM((G, R * tq, tkc), F32), pltpu.VMEM((G, R * tq, tkc), F32),
                        pltpu.VMEM((G, R * tq, d), F32)],
        compiler_params=_params("parallel", "arbitrary"),
        name="dsa_attn",
    )(main, main, main, main, tail, tail)


def kernel(x, p, ffn1_w_in, ffn1_w_out, ffn2_w_in, ffn2_w_out, ln_g, ln_b, ple_w_gate, ple_w_proj,
           a_w_in, a_w_out, a_lam_q1, a_lam_k1, a_lam_q2, a_lam_k2, a_subln_g, b_w_in, b_w_out,
           c_w_in, c_w_out):
    B, S, D = x.shape
    depth = p.shape[0]
    T = B * S
    alpha = (2.0 * depth) ** 0.25
    bf = lambda w: w.astype(BF16)
    ple_g, ple_p = bf(ple_w_gate), bf(ple_w_proj)
    a_in, a_out, b_in, b_out, c_out = bf(a_w_in), bf(a_w_out), bf(b_w_in), bf(b_w_out), bf(c_w_out)
    n_main = C_HEADS * C_HEAD_DIM + 2 * C_KV_HEADS * C_HEAD_DIM + IDX_HEADS * IDX_DIM
    c_in = bf(c_w_in)
    c_tail = c_w_in[:, :, n_main:]
    c_tail = bf(jnp.pad(c_tail, ((0, 0), (0, 0), (0, LANES - c_tail.shape[2]))))
    lng = ln_g.reshape(-1, 1, D).astype(F32)
    lnb = ln_b.reshape(-1, 1, D).astype(F32)
    n_ln = ln_g.shape[1]
    p = p.reshape(depth, T, -1)
    x = x.reshape(T, D)
    for i in range(depth):
        m, j = i % N_MIXERS, i // N_MIXERS
        x = _ffn(x, ffn1_w_in, ffn1_w_out, i, lng, lnb, i * n_ln, alpha)
        if m == 0:
            d = A_HEAD_DIM
            H = D // (2 * d)
            lambda_init = 0.8 - 0.6 * math.exp(-0.3 * i)
            qkv = _proj(x, a_in, j, a_in.shape[2], BF16)
            lam_p = jnp.stack([a_lam_q1[j], a_lam_k1[j], a_lam_q2[j], a_lam_k2[j]]).astype(F32)
            o = _diff_attn(qkv, lam_p, a_subln_g[j].reshape(1, 2 * d).astype(F32), B, S, H, d, lambda_init)
            w_out = a_out
        elif m == 1:
            qkv = _proj(x, b_in, j, b_in.shape[2], BF16)
            o = _moba_attn(qkv, B, S, B_HEADS, B_HEAD_DIM, MOBA_BLOCK)
            w_out = b_out
        else:
            main = _proj(x, c_in, j, n_main, BF16)
            tail = _proj(x, c_tail, j, LANES, F32)
            o = _dsa_attn(main, tail, B, S)
            w_out = c_out
        x = _out_ln(x, o, w_out, j, lng, lnb, i * n_ln + 1, alpha)
        x = _ffn(x, ffn2_w_in, ffn2_w_out, i, lng, lnb, i * n_ln + 2, alpha)
        x = _ple(x, p, ple_g, ple_p, i, lng, lnb, i * n_ln + 3, alpha)
    return x.reshape(B, S, D)
```

```python
import functools
import math

import numpy as np
import jax
import jax.numpy as jnp
from jax import lax
from jax.experimental import pallas as pl
from jax.experimental.pallas import tpu as pltpu

F32 = jnp.float32
BF16 = jnp.bfloat16

N_MIXERS = 3
A_HEAD_DIM = 128
B_HEADS = 16
B_HEAD_DIM = 128
MOBA_BLOCK = 256
MOBA_TOPK = 3
C_HEADS = 16
C_KV_HEADS = 4
C_HEAD_DIM = 128
IDX_HEADS = 16
IDX_DIM = 64
DSA_TOPK_MAX = 256
LN_EPS = 1e-5
RMS_EPS = 1e-6

LANES = 128
NEG = -1e30
INT_MIN = -(2 ** 31)
LOG2E = math.log2(math.e)
VMEM_LIMIT = 60 * 1024 * 1024
ROW_SPLIT = 128


def _alibi_slopes(n):
    return 2.0 ** (-8.0 * np.arange(1, n + 1, dtype=np.float32) / n)


def _params(*sem):
    return pltpu.CompilerParams(dimension_semantics=sem, vmem_limit_bytes=VMEM_LIMIT)


def _dot(a, b):
    return jnp.dot(a, b, preferred_element_type=F32)


def _dot_nt(a, b):
    return lax.dot_general(a, b, (((1,), (1,)), ((), ())), preferred_element_type=F32)


def _layer_norm(z, g, b):
    mu = jnp.mean(z, -1, keepdims=True)
    zc = z - mu
    var = jnp.mean(zc * zc, -1, keepdims=True)
    return zc * lax.rsqrt(var + LN_EPS) * g + b


def _ffn_kernel(x_ref, wg_ref, wu_ref, wo_ref, g_ref, b_ref, o_ref, xb_ref, *, alpha):
    j = pl.program_id(1)

    @pl.when(j == 0)
    def _():
        xb_ref[...] = x_ref[...].astype(BF16)
        o_ref[...] = jnp.zeros_like(o_ref)

    xb = xb_ref[...]
    gate = _dot(xb, wg_ref[...].astype(BF16))
    up = _dot(xb, wu_ref[...].astype(BF16))
    h = (gate * jax.nn.sigmoid(gate) * up).astype(BF16)
    o_ref[...] += _dot(h, wo_ref[...].astype(BF16))

    @pl.when(j == pl.num_programs(1) - 1)
    def _():
        for r in range(0, o_ref.shape[0], ROW_SPLIT):
            rows = slice(r, r + ROW_SPLIT)
            z = alpha * x_ref[rows, :] + 0.5 * o_ref[rows, :]
            o_ref[rows, :] = _layer_norm(z, g_ref[...], b_ref[...])


def _ln_specs(ln, D, ngrid):
    if ngrid == 1:
        return pl.BlockSpec((None, 1, D), lambda i: (ln, 0, 0))
    return pl.BlockSpec((None, 1, D), lambda i, j: (ln, 0, 0))


def _ffn(x, w_in, w_out, layer, ln_g, ln_b, ln, alpha, tm=1024, tf=256):
    T, D = x.shape
    F = w_out.shape[1]
    nf = F // tf
    assert T % tm == 0 and F % tf == 0 and w_in.shape[1:] == (D, 2 * F)
    return pl.pallas_call(
        functools.partial(_ffn_kernel, alpha=alpha),
        grid=(T // tm, nf),
        in_specs=[
            pl.BlockSpec((tm, D), lambda i, j: (i, 0)),
            pl.BlockSpec((None, D, tf), lambda i, j: (layer, 0, j)),
            pl.BlockSpec((None, D, tf), lambda i, j: (layer, 0, j + nf)),
            pl.BlockSpec((None, tf, D), lambda i, j: (layer, j, 0)),
            _ln_specs(ln, D, 2),
            _ln_specs(ln, D, 2),
        ],
        out_specs=pl.BlockSpec((tm, D), lambda i, j: (i, 0)),
        out_shape=jax.ShapeDtypeStruct((T, D), F32),
        scratch_shapes=[pltpu.VMEM((tm, D), BF16)],
        compiler_params=_params("parallel", "arbitrary"),
        name="ffn",
    )(x, w_in, w_in, w_out, ln_g, ln_b)


def _proj_kernel(x_ref, w_ref, o_ref, xb_ref):
    @pl.when(pl.program_id(1) == 0)
    def _():
        xb_ref[...] = x_ref[...].astype(BF16)

    o_ref[...] = _dot(xb_ref[...], w_ref[...]).astype(o_ref.dtype)


def _proj(x, w, layer, n_cols, out_dtype, tm=1024, tn=1024):
    T, D = x.shape
    tm, tn = min(tm, T), min(tn, n_cols)
    assert T % tm == 0 and n_cols % tn == 0 and w.shape[1] == D and w.shape[2] >= n_cols
    return pl.pallas_call(
        _proj_kernel,
        grid=(T // tm, n_cols // tn),
        in_specs=[
            pl.BlockSpec((tm, D), lambda i, j: (i, 0)),
            pl.BlockSpec((None, D, tn), lambda i, j: (layer, 0, j)),
        ],
        out_specs=pl.BlockSpec((tm, tn), lambda i, j: (i, j)),
        out_shape=jax.ShapeDtypeStruct((T, n_cols), out_dtype),
        scratch_shapes=[pltpu.VMEM((tm, D), BF16)],
        compiler_params=_params("parallel", "arbitrary"),
        name="proj",
    )(x, w)


def _out_ln_kernel(x_ref, o_ref, w_ref, g_ref, b_ref, y_ref, *, alpha):
    for r in range(0, x_ref.shape[0], ROW_SPLIT):
        rows = slice(r, r + ROW_SPLIT)
        z = alpha * x_ref[rows, :] + _dot(o_ref[rows, :], w_ref[...])
        y_ref[rows, :] = _layer_norm(z, g_ref[...], b_ref[...])


def _out_ln(x, o, w, layer, ln_g, ln_b, ln, alpha, tm=512):
    T, D = x.shape
    K = o.shape[1]
    assert T % tm == 0 and tm % ROW_SPLIT == 0 and w.shape[1:] == (K, D)
    return pl.pallas_call(
        functools.partial(_out_ln_kernel, alpha=alpha),
        grid=(T // tm,),
        in_specs=[
            pl.BlockSpec((tm, D), lambda i: (i, 0)),
            pl.BlockSpec((tm, K), lambda i: (i, 0)),
            pl.BlockSpec((None, K, D), lambda i: (layer, 0, 0)),
            _ln_specs(ln, D, 1),
            _ln_specs(ln, D, 1),
        ],
        out_specs=pl.BlockSpec((tm, D), lambda i: (i, 0)),
        out_shape=jax.ShapeDtypeStruct((T, D), F32),
        compiler_params=_params("parallel"),
        name="out_ln",
    )(x, o, w, ln_g, ln_b)


def _ple_kernel(x_ref, p_ref, wg_ref, wp_ref, g_ref, b_ref, y_ref, *, alpha):
    for r in range(0, x_ref.shape[0], ROW_SPLIT):
        rows = slice(r, r + ROW_SPLIT)
        x = x_ref[rows, :]
        gate = jax.nn.sigmoid(_dot(x.astype(BF16), wg_ref[...]))
        emb = _dot(p_ref[rows, :].astype(BF16), wp_ref[...])
        y_ref[rows, :] = _layer_norm(alpha * x + gate * emb, g_ref[...], b_ref[...])


def _ple(x, p, wg, wp, layer, ln_g, ln_b, ln, alpha, tm=512):
    T, D = x.shape
    P = p.shape[2]
    assert T % tm == 0 and tm % ROW_SPLIT == 0
    return pl.pallas_call(
        functools.partial(_ple_kernel, alpha=alpha),
        grid=(T // tm,),
        in_specs=[
            pl.BlockSpec((tm, D), lambda i: (i, 0)),
            pl.BlockSpec((None, tm, P), lambda i: (layer, i, 0)),
            pl.BlockSpec((None, D, D), lambda i: (layer, 0, 0)),
            pl.BlockSpec((None, P, D), lambda i: (layer, 0, 0)),
            _ln_specs(ln, D, 1),
            _ln_specs(ln, D, 1),
        ],
        out_specs=pl.BlockSpec((tm, D), lambda i: (i, 0)),
        out_shape=jax.ShapeDtypeStruct((T, D), F32),
        compiler_params=_params("parallel"),
        name="ple",
    )(x, p, wg, wp, ln_g, ln_b)


def _row_max_bcast(mx_ref):
    m = jnp.max(mx_ref[...], -1, keepdims=True)
    mx_ref[...] = jnp.broadcast_to(m, mx_ref.shape)


def _causal_sweep(fn, qi):
    @pl.when(qi % 2 == 1)
    def _():
        fn(qi - 1, 2, True)

    @pl.when(qi % 2 == 0)
    def _():
        fn(qi, 1, True)

    def body(j, carry):
        fn(2 * j, 2, False)
        return carry

    lax.fori_loop(0, qi // 2, body, 0)


def _row_minus_col(rows, cols):
    r = lax.broadcasted_iota(jnp.int32, (rows, cols), 0)
    c = lax.broadcasted_iota(jnp.int32, (rows, cols), 1)
    return r - c


DIFF_HEADS_PER_STEP = 4


def _diff_attn_kernel(slopes_ref, lam_ref, subg_ref, q_ref, k_ref, v_ref, o_ref,
                      s_ref, bias_ref, mx_ref, ls_ref, acc_ref, *, d, scale, lambda_init):
    hp = DIFF_HEADS_PER_STEP
    h0 = pl.program_id(1) * hp
    qi = pl.program_id(2)
    tq = q_ref.shape[0]
    units = range(2 * hp)
    slopes = [slopes_ref[h0 + hh] * (1.0 / scale) for hh in range(hp)]
    q = [q_ref[:, u * d:(u + 1) * d] for u in units]

    @pl.when(qi == 0)
    def _():
        rel = _row_minus_col(tq, 2 * tq).astype(F32)
        for hh in range(hp):
            bias_ref[hh] = -slopes[hh] * rel

    def span_scores(c0, width, diagonal):
        rows = pl.ds(pl.multiple_of(c0 * tq, tq), width * tq)
        for hh in range(hp):
            bias = bias_ref[hh, :, :width * tq] - slopes[hh] * ((qi - c0) * tq).astype(F32)
            for u in (2 * hh, 2 * hh + 1):
                s = _dot_nt(q[u], k_ref[rows, u * d:(u + 1) * d]) + bias
                if diagonal:
                    s = jnp.where(_row_minus_col(tq, width * tq) + (qi - c0) * tq >= 0, s, NEG)
                chunks = [s[:, w * tq:(w + 1) * tq] for w in range(width)]
                for w in range(width):
                    s_ref[u, c0 + w] = chunks[w]
                mx = functools.reduce(jnp.maximum, chunks)
                mx_ref[u] = mx if diagonal else jnp.maximum(mx_ref[u], mx)

    def span_accumulate(c0, width, diagonal):
        rows = pl.ds(pl.multiple_of(c0 * tq, tq), width * tq)
        for u in units:
            hh = u // 2
            ps = [jnp.exp2((s_ref[u, c0 + w] - mx_ref[u]) * (scale * LOG2E)) for w in range(width)]
            psum = functools.reduce(lambda a, b: a + b, ps)
            p = ps[0] if width == 1 else jnp.concatenate(ps, axis=1)
            pv = _dot(p.astype(BF16), v_ref[rows, hh * 2 * d:(hh + 1) * 2 * d])
            if diagonal:
                ls_ref[u] = psum
                acc_ref[u] = pv
            else:
                ls_ref[u] += psum
                acc_ref[u] += pv

    _causal_sweep(span_scores, qi)
    for u in units:
        _row_max_bcast(mx_ref.at[u])
    _causal_sweep(span_accumulate, qi)

    lam_p = lam_ref[...]
    lam = (jnp.exp(jnp.sum(lam_p[0:1] * lam_p[1:2], -1, keepdims=True))
           - jnp.exp(jnp.sum(lam_p[2:3] * lam_p[3:4], -1, keepdims=True)) + lambda_init)
    for hh in range(hp):
        l0 = jnp.sum(ls_ref[2 * hh], -1, keepdims=True)
        l1 = jnp.sum(ls_ref[2 * hh + 1], -1, keepdims=True)
        o = acc_ref[2 * hh] / l0 - lam * (acc_ref[2 * hh + 1] / l1)
        o = o * lax.rsqrt(jnp.mean(o * o, -1, keepdims=True) + RMS_EPS) * subg_ref[...] * (1.0 - lambda_init)
        o_ref[:, hh * 2 * d:(hh + 1) * 2 * d] = o.astype(o_ref.dtype)


def _diff_attn(qkv, lam_p, subln_g, B, S, H, d, lambda_init, tq=256):
    T = B * S
    nq = S // tq
    hp = DIFF_HEADS_PER_STEP
    w = hp * 2 * d
    assert S % tq == 0 and H % hp == 0 and qkv.shape == (T, 3 * H * 2 * d)
    slopes = jnp.asarray(_alibi_slopes(H), F32)
    return pl.pallas_call(
        functools.partial(_diff_attn_kernel, d=d, scale=d ** -0.5, lambda_init=lambda_init),
        grid=(B, H // hp, nq),
        in_specs=[
            pl.BlockSpec(memory_space=pltpu.SMEM),
            pl.BlockSpec((4, d), lambda b, h, i: (0, 0)),
            pl.BlockSpec((1, 2 * d), lambda b, h, i: (0, 0)),
            pl.BlockSpec((tq, w), lambda b, h, i: (b * nq + i, h)),
            pl.BlockSpec((S, w), lambda b, h, i: (b, H // hp + h)),
            pl.BlockSpec((S, w), lambda b, h, i: (b, 2 * (H // hp) + h)),
        ],
        out_specs=pl.BlockSpec((tq, w), lambda b, h, i: (b * nq + i, h)),
        out_shape=jax.ShapeDtypeStruct((T, H * 2 * d), BF16),
        scratch_shapes=[pltpu.VMEM((2 * hp, nq, tq, tq), F32), pltpu.VMEM((hp, tq, 2 * tq), F32),
                        pltpu.VMEM((2 * hp, tq, tq), F32), pltpu.VMEM((2 * hp, tq, tq), F32),
                        pltpu.VMEM((2 * hp, tq, 2 * d), F32)],
        compiler_params=_params("parallel", "parallel", "arbitrary"),
        name="diff_attn",
    )(slopes, lam_p, subln_g, qkv, qkv, qkv)


MOBA_HEADS_PER_STEP = 8


def _moba_kernel(slopes_ref, q_ref, k_ref, v_ref, o_ref, km_ref, mb_ref, s_ref, bias_ref,
                 mx_ref, ls_ref, acc_ref, *, nb, n_sel, d, scale):
    hp = MOBA_HEADS_PER_STEP
    h0 = pl.program_id(1) * hp
    own = pl.program_id(2)
    bs = q_ref.shape[0]
    nbp = -(-nb // 8) * 8
    heads = range(hp)
    cols = [slice(hh * d, (hh + 1) * d) for hh in heads]
    slopes = [slopes_ref[h0 + hh] * (1.0 / scale) for hh in heads]

    @pl.when(own == 0)
    def _():
        km_ref[...] = jnp.zeros_like(km_ref)
        for hh in heads:
            for n in range(nb):
                km_ref[hh, n:n + 1, :] = jnp.mean(
                    k_ref[n * bs:(n + 1) * bs, cols[hh]].astype(F32), 0, keepdims=True)

    q = [q_ref[:, cols[hh]] for hh in heads]
    rel_i = _row_minus_col(bs, bs)
    block = lax.broadcasted_iota(jnp.int32, (nbp, bs), 0)
    for hh in heads:
        km = km_ref[hh]
        km_hi = km.astype(BF16)
        km_lo = (km - km_hi.astype(F32)).astype(BF16)
        gate = (_dot_nt(km_hi, q[hh]) + _dot_nt(km_lo, q[hh]))[0:nbp, :]
        rank = jnp.zeros((nbp, bs), jnp.int32)
        for m in range(nb - 1):
            gm = gate[m:m + 1, :]
            beats = (gm > gate) | ((gm == gate) & (block > m))
            rank = rank + beats.astype(jnp.int32) * (own > m).astype(jnp.int32)
        sel = (rank < n_sel) & (block < own)
        off = -slopes[hh] * ((own - block) * bs).astype(F32)
        mb_t = jnp.where(sel, off, NEG)
        mb = jnp.transpose(jnp.concatenate([mb_t, jnp.zeros((LANES - nbp, bs), F32)], axis=0))
        for n in range(nb - 1):
            mb_ref[hh, n] = jnp.broadcast_to(mb[:, n:n + 1], (bs, LANES))

    @pl.when(own == 0)
    def _():
        for hh in heads:
            bias_ref[hh] = -slopes[hh] * rel_i.astype(F32)

    def span_scores(c0, width, diagonal):
        rows = pl.ds(pl.multiple_of(c0 * bs, bs), width * bs)
        for hh in heads:
            qk = _dot_nt(q[hh], k_ref[rows, cols[hh]])
            chunks = []
            for w in range(width):
                sw = qk[:, w * bs:(w + 1) * bs] + bias_ref[hh]
                if diagonal and w == width - 1:
                    sw = jnp.where(rel_i >= 0, sw, NEG)
                else:
                    sw = sw + jnp.concatenate([mb_ref[hh, c0 + w]] * (bs // LANES), axis=1)
                s_ref[hh, c0 + w] = sw
                chunks.append(sw)
            mx = functools.reduce(jnp.maximum, chunks)
            mx_ref[hh] = mx if diagonal else jnp.maximum(mx_ref[hh], mx)

    def span_accumulate(c0, width, diagonal):
        rows = pl.ds(pl.multiple_of(c0 * bs, bs), width * bs)
        for hh in heads:
            ps = [jnp.exp2((s_ref[hh, c0 + w] - mx_ref[hh]) * (scale * LOG2E)) for w in range(width)]
            psum = functools.reduce(lambda a, b: a + b, ps)
            p = ps[0] if width == 1 else jnp.concatenate(ps, axis=1)
            pv = _dot(p.astype(BF16), v_ref[rows, cols[hh]])
            if diagonal:
                ls_ref[hh] = psum
                acc_ref[hh] = pv
            else:
                ls_ref[hh] += psum
                acc_ref[hh] += pv

    _causal_sweep(span_scores, own)
    for hh in heads:
        _row_max_bcast(mx_ref.at[hh])
    _causal_sweep(span_accumulate, own)

    for hh in heads:
        l = jnp.sum(ls_ref[hh], -1, keepdims=True)
        o_ref[:, cols[hh]] = (acc_ref[hh] / l).astype(o_ref.dtype)


def _moba_attn(qkv, B, S, H, d, bs):
    T = B * S
    nb = S // bs
    hp = MOBA_HEADS_PER_STEP
    assert S % bs == 0 and nb <= LANES and H % hp == 0 and qkv.shape == (T, 3 * H * d)
    n_sel = min(MOBA_TOPK, nb - 1)
    slopes = jnp.asarray(_alibi_slopes(H), F32)
    return pl.pallas_call(
        functools.partial(_moba_kernel, nb=nb, n_sel=n_sel, d=d, scale=d ** -0.5),
        grid=(B, H // hp, nb),
        in_specs=[
            pl.BlockSpec(memory_space=pltpu.SMEM),
            pl.BlockSpec((bs, hp * d), lambda b, h, i: (b * nb + i, h)),
            pl.BlockSpec((S, hp * d), lambda b, h, i: (b, H // hp + h)),
            pl.BlockSpec((S, hp * d), lambda b, h, i: (b, 2 * (H // hp) + h)),
        ],
        out_specs=pl.BlockSpec((bs, hp * d), lambda b, h, i: (b * nb + i, h)),
        out_shape=jax.ShapeDtypeStruct((T, H * d), BF16),
        scratch_shapes=[pltpu.VMEM((hp, LANES, d), F32), pltpu.VMEM((hp, nb, bs, LANES), F32),
                        pltpu.VMEM((hp, nb, bs, bs), F32), pltpu.VMEM((hp, bs, bs), F32),
                        pltpu.VMEM((hp, bs, bs), F32), pltpu.VMEM((hp, bs, bs), F32),
                        pltpu.VMEM((hp, bs, d), F32)],
        compiler_params=_params("parallel", "parallel", "arbitrary"),
        name="moba_attn",
    )(slopes, qkv, qkv, qkv)


def _dsa_kernel(q_ref, k_ref, v_ref, qi_ref, tailq_ref, tailk_ref, o_ref,
                kib_ref, qis_ref, key_ref, thr_ref, s_ref, mx_ref, ls_ref, acc_ref, *, n_keep, tkc, scale):
    i = pl.program_id(1)
    tq = q_ref.shape[0]
    G, R, d = C_KV_HEADS, C_HEADS // C_KV_HEADS, C_HEAD_DIM
    qstart = i * tq
    nkc = (qstart + tq + tkc - 1) // tkc

    @pl.when(i == 0)
    def _():
        kib_ref[...] = tailk_ref[:, 0:IDX_DIM].astype(BF16)

    w_t = (jnp.transpose(tailq_ref[...])[IDX_DIM:IDX_DIM + IDX_HEADS, :]
           * (IDX_HEADS ** -0.5 * IDX_DIM ** -0.5))
    key_minus_query = _row_minus_col(tkc, tq)

    for hh in range(IDX_HEADS):
        qis_ref[hh * tq:(hh + 1) * tq, :] = qi_ref[:, hh * IDX_DIM:(hh + 1) * IDX_DIM]

    def score_chunk(c, carry):
        kc = kib_ref[pl.ds(pl.multiple_of(c * tkc, tkc), tkc), :]
        dots = _dot_nt(kc, qis_ref[...])
        acc = jnp.zeros((tkc, tq), F32)
        for hh in range(IDX_HEADS):
            acc = acc + w_t[hh:hh + 1, :] * jnp.maximum(dots[:, hh * tq:(hh + 1) * tq], 0.0)
        bits = lax.bitcast_convert_type(acc, jnp.int32)
        key = bits ^ ((bits >> 31) & jnp.int32(0x7FFFFFFF))
        causal = key_minus_query + (c * tkc - qstart) <= 0
        key_ref[c] = jnp.where(causal, key, jnp.int32(INT_MIN))
        return carry

    lax.fori_loop(0, nkc, score_chunk, 0)

    def count(pred):
        def body(c, part):
            hit = pred(key_ref[c]).astype(jnp.int32)
            return part + jnp.sum(hit.reshape(tkc // 8, 8, tq), axis=0)
        part = lax.fori_loop(0, nkc, body, jnp.zeros((8, tq), jnp.int32))
        return jnp.sum(part, axis=0, keepdims=True)

    thr_ref[...] = jnp.full_like(thr_ref, INT_MIN + 1)

    @pl.when(qstart + tq > n_keep)
    def _():
        zero = jnp.zeros((1, tq), jnp.int32)
        thr0 = jnp.where(count(lambda k: k >= zero) >= n_keep, jnp.int32(0), jnp.int32(INT_MIN))

        def bit_step(b, thr):
            cand = thr | jnp.left_shift(jnp.int32(1), 30 - b)
            return jnp.where(count(lambda k: k >= cand) >= n_keep, cand, thr)

        thr = jnp.maximum(lax.fori_loop(0, 31, bit_step, thr0), jnp.int32(INT_MIN + 1))
        thr_ref[...] = thr

        @pl.when(jnp.max(count(lambda k: k >= thr)) > n_keep)
        def _():
            room = (n_keep - count(lambda k: k > thr)).astype(F32)
            lower_tri = jnp.where(_row_minus_col(tkc, tkc) >= 0, 1.0, 0.0).astype(BF16)

            def drop_excess(c, seen):
                key = key_ref[c]
                tie = key == thr
                tie_f = jnp.where(tie, 1.0, 0.0)
                rank = seen + _dot(lower_tri, tie_f.astype(BF16))
                key_ref[c] = jnp.where(tie & (rank > room), jnp.int32(INT_MIN), key)
                return seen + jnp.sum(tie_f, axis=0, keepdims=True)

            lax.fori_loop(0, nkc, drop_excess, jnp.zeros((1, tq), F32))

    thr = thr_ref[...]
    slopes = _alibi_slopes(C_HEADS).reshape(G, R)
    rel = _row_minus_col(tq, tkc).astype(F32)

    def score_pass(c, first):
        rows = pl.ds(pl.multiple_of(c * tkc, tkc), tkc)
        mask = jnp.transpose(jnp.where(key_ref[c] >= thr, 0.0, NEG))
        dist = rel + (qstart - c * tkc).astype(F32)
        for g in range(G):
            qs = jnp.concatenate(
                [q_ref[:, (g * R + r) * d:(g * R + r + 1) * d] for r in range(R)], axis=0)
            bias = jnp.concatenate([mask - float(slopes[g, r] / scale) * dist for r in range(R)], axis=0)
            s = _dot_nt(qs, k_ref[rows, g * d:(g + 1) * d]) + bias
            s_ref[g, c] = s
            mx_ref[g] = s if first else jnp.maximum(mx_ref[g], s)

    def acc_pass(c, first):
        rows = pl.ds(pl.multiple_of(c * tkc, tkc), tkc)
        for g in range(G):
            p = jnp.exp2((s_ref[g, c] - mx_ref[g]) * (scale * LOG2E))
            pv = _dot(p.astype(BF16), v_ref[rows, g * d:(g + 1) * d])
            if first:
                ls_ref[g] = p
                acc_ref[g] = pv
            else:
                ls_ref[g] += p
                acc_ref[g] += pv

    def sweep(fn):
        fn(0, True)

        def body(c, carry):
            fn(c, False)
            return carry

        lax.fori_loop(1, nkc, body, 0)

    sweep(score_pass)
    for g in range(G):
        _row_max_bcast(mx_ref.at[g])
    sweep(acc_pass)

    for g in range(G):
        o = acc_ref[g] / jnp.sum(ls_ref[g], -1, keepdims=True)
        for r in range(R):
            o_ref[:, (g * R + r) * d:(g * R + r + 1) * d] = o[r * tq:(r + 1) * tq].astype(o_ref.dtype)


def _dsa_attn(main, tail, B, S, tq=128, tkc=256):
    T = B * S
    nq = S // tq
    G, H, d = C_KV_HEADS, C_HEADS, C_HEAD_DIM
    R = H // G
    c_q, c_kv, c_iq = H * d, G * d, IDX_HEADS * IDX_DIM
    assert S % tq == 0 and S % tkc == 0 and tkc % tq == 0
    assert main.shape == (T, c_q + 2 * c_kv + c_iq) and tail.shape == (T, LANES)
    assert c_q % c_kv == 0 and (c_q + 2 * c_kv) % c_iq == 0
    n_keep = min(DSA_TOPK_MAX, S // 4)
    return pl.pallas_call(
        functools.partial(_dsa_kernel, n_keep=n_keep, tkc=tkc, scale=d ** -0.5),
        grid=(B, nq),
        in_specs=[
            pl.BlockSpec((tq, c_q), lambda b, i: (b * nq + i, 0)),
            pl.BlockSpec((S, c_kv), lambda b, i: (b, c_q // c_kv)),
            pl.BlockSpec((S, c_kv), lambda b, i: (b, c_q // c_kv + 1)),
            pl.BlockSpec((tq, c_iq), lambda b, i: (b * nq + i, (c_q + 2 * c_kv) // c_iq)),
            pl.BlockSpec((tq, LANES), lambda b, i: (b * nq + i, 0)),
            pl.BlockSpec((S, LANES), lambda b, i: (b, 0)),
        ],
        out_specs=pl.BlockSpec((tq, c_q), lambda b, i: (b * nq + i, 0)),
        out_shape=jax.ShapeDtypeStruct((T, c_q), BF16),
        scratch_shapes=[pltpu.VMEM((S, IDX_DIM), BF16), pltpu.VMEM((IDX_HEADS * tq, IDX_DIM), BF16),
                        pltpu.VMEM((S // tkc, tkc, tq), jnp.int32),
                        pltpu.VMEM((1, tq), jnp.int32),
                        pltpu.VMEM((G, S // tkc, R * tq, tkc), F32),
                        pltpu.VMEM((G, R * tq, tkc), F32), pltpu.VMEM((G, R * tq, tkc), F32),
                        pltpu.VMEM((G, R * tq, d), F32)],
        compiler_params=_params("parallel", "arbitrary"),
        name="dsa_attn",
    )(main, main, main, main, tail, tail)


def kernel(x, p, ffn1_w_in, ffn1_w_out, ffn2_w_in, ffn2_w_out, ln_g, ln_b, ple_w_gate, ple_w_proj,
           a_w_in, a_w_out, a_lam_q1, a_lam_k1, a_lam_q2, a_lam_k2, a_subln_g, b_w_in, b_w_out,
           c_w_in, c_w_out):
    B, S, D = x.shape
    depth = p.shape[0]
    T = B * S
    alpha = (2.0 * depth) ** 0.25
    bf = lambda w: w.astype(BF16)
    ple_g, ple_p = bf(ple_w_gate), bf(ple_w_proj)
    a_in, a_out, b_in, b_out, c_out = bf(a_w_in), bf(a_w_out), bf(b_w_in), bf(b_w_out), bf(c_w_out)
    n_main = C_HEADS * C_HEAD_DIM + 2 * C_KV_HEADS * C_HEAD_DIM + IDX_HEADS * IDX_DIM
    c_in = bf(c_w_in)
    c_tail = c_w_in[:, :, n_main:]
    c_tail = bf(jnp.pad(c_tail, ((0, 0), (0, 0), (0, LANES - c_tail.shape[2]))))
    lng = ln_g.reshape(-1, 1, D).astype(F32)
    lnb = ln_b.reshape(-1, 1, D).astype(F32)
    n_ln = ln_g.shape[1]
    p = p.reshape(depth, T, -1)
    x = x.reshape(T, D)
    for i in range(depth):
        m, j = i % N_MIXERS, i // N_MIXERS
        x = _ffn(x, ffn1_w_in, ffn1_w_out, i, lng, lnb, i * n_ln, alpha)
        if m == 0:
            d = A_HEAD_DIM
            H = D // (2 * d)
            lambda_init = 0.8 - 0.6 * math.exp(-0.3 * i)
            qkv = _proj(x, a_in, j, a_in.shape[2], BF16)
            lam_p = jnp.stack([a_lam_q1[j], a_lam_k1[j], a_lam_q2[j], a_lam_k2[j]]).astype(F32)
            o = _diff_attn(qkv, lam_p, a_subln_g[j].reshape(1, 2 * d).astype(F32), B, S, H, d, lambda_init)
            w_out = a_out
        elif m == 1:
            qkv = _proj(x, b_in, j, b_in.shape[2], BF16)
            o = _moba_attn(qkv, B, S, B_HEADS, B_HEAD_DIM, MOBA_BLOCK)
            w_out = b_out
        else:
            main = _proj(x, c_in, j, n_main, BF16)
            tail = _proj(x, c_tail, j, LANES, F32)
            o = _dsa_attn(main, tail, B, S)
            w_out = c_out
        x = _out_ln(x, o, w_out, j, lng, lnb, i * n_ln + 1, alpha)
        x = _ffn(x, ffn2_w_in, ffn2_w_out, i, lng, lnb, i * n_ln + 2, alpha)
        x = _ple(x, p, ple_g, ple_p, i, lng, lnb, i * n_ln + 3, alpha)
    return x.reshape(B, S, D)
```

```python
import functools
import math

import numpy as np
import jax
import jax.numpy as jnp
from jax import lax
from jax.experimental import pallas as pl
from jax.experimental.pallas import tpu as pltpu

F32 = jnp.float32
BF16 = jnp.bfloat16

N_MIXERS = 3
A_HEAD_DIM = 128
B_HEADS = 16
B_HEAD_DIM = 128
MOBA_BLOCK = 256
MOBA_TOPK = 3
C_HEADS = 16
C_KV_HEADS = 4
C_HEAD_DIM = 128
IDX_HEADS = 16
IDX_DIM = 64
DSA_TOPK_MAX = 256
LN_EPS = 1e-5
RMS_EPS = 1e-6

LANES = 128
SUBLANES = 8
NEG = -1e30
INT_MIN = -(2 ** 31)
LOG2E = math.log2(math.e)
VMEM_LIMIT = 60 * 1024 * 1024
ROW_SPLIT = 128


def _alibi_slopes(n):
    return 2.0 ** (-8.0 * np.arange(1, n + 1, dtype=np.float32) / n)


def _params(*sem):
    return pltpu.CompilerParams(dimension_semantics=sem, vmem_limit_bytes=VMEM_LIMIT)


def _dot(a, b):
    return jnp.dot(a, b, preferred_element_type=F32)


def _dot_nt(a, b):
    return lax.dot_general(a, b, (((1,), (1,)), ((), ())), preferred_element_type=F32)


def _layer_norm(z, g, b):
    mu = jnp.mean(z, -1, keepdims=True)
    zc = z - mu
    var = jnp.mean(zc * zc, -1, keepdims=True)
    return zc * lax.rsqrt(var + LN_EPS) * g + b


def _ffn_kernel(x_ref, wg_ref, wu_ref, wo_ref, g_ref, b_ref, o_ref, xb_ref, *, alpha):
    j = pl.program_id(1)

    @pl.when(j == 0)
    def _():
        xb_ref[...] = x_ref[...].astype(BF16)
        o_ref[...] = jnp.zeros_like(o_ref)

    xb = xb_ref[...]
    gate = _dot(xb, wg_ref[...].astype(BF16))
    up = _dot(xb, wu_ref[...].astype(BF16))
    h = (gate * jax.nn.sigmoid(gate) * up).astype(BF16)
    o_ref[...] += _dot(h, wo_ref[...].astype(BF16))

    @pl.when(j == pl.num_programs(1) - 1)
    def _():
        for r in range(0, o_ref.shape[0], ROW_SPLIT):
            rows = slice(r, r + ROW_SPLIT)
            z = alpha * x_ref[rows, :] + 0.5 * o_ref[rows, :]
            o_ref[rows, :] = _layer_norm(z, g_ref[...], b_ref[...])


def _ln_specs(ln, D, ngrid):
    if ngrid == 1:
        return pl.BlockSpec((None, 1, D), lambda i: (ln, 0, 0))
    return pl.BlockSpec((None, 1, D), lambda i, j: (ln, 0, 0))


def _ffn(x, w_in, w_out, layer, ln_g, ln_b, ln, alpha, tm=1024, tf=256):
    T, D = x.shape
    F = w_out.shape[1]
    nf = F // tf
    assert T % tm == 0 and F % tf == 0 and w_in.shape[1:] == (D, 2 * F)
    return pl.pallas_call(
        functools.partial(_ffn_kernel, alpha=alpha),
        grid=(T // tm, nf),
        in_specs=[
            pl.BlockSpec((tm, D), lambda i, j: (i, 0)),
            pl.BlockSpec((None, D, tf), lambda i, j: (layer, 0, j)),
            pl.BlockSpec((None, D, tf), lambda i, j: (layer, 0, j + nf)),
            pl.BlockSpec((None, tf, D), lambda i, j: (layer, j, 0)),
            _ln_specs(ln, D, 2),
            _ln_specs(ln, D, 2),
        ],
        out_specs=pl.BlockSpec((tm, D), lambda i, j: (i, 0)),
        out_shape=jax.ShapeDtypeStruct((T, D), F32),
        scratch_shapes=[pltpu.VMEM((tm, D), BF16)],
        compiler_params=_params("parallel", "arbitrary"),
        name="ffn",
    )(x, w_in, w_in, w_out, ln_g, ln_b)


def _proj_kernel(x_ref, w_ref, o_ref, xb_ref):
    @pl.when(pl.program_id(1) == 0)
    def _():
        xb_ref[...] = x_ref[...].astype(BF16)

    o_ref[...] = _dot(xb_ref[...], w_ref[...]).astype(o_ref.dtype)


def _proj(x, w, layer, n_cols, out_dtype, tm=1024, tn=1024):
    T, D = x.shape
    tm, tn = min(tm, T), min(tn, n_cols)
    assert T % tm == 0 and n_cols % tn == 0 and w.shape[1] == D and w.shape[2] >= n_cols
    return pl.pallas_call(
        _proj_kernel,
        grid=(T // tm, n_cols // tn),
        in_specs=[
            pl.BlockSpec((tm, D), lambda i, j: (i, 0)),
            pl.BlockSpec((None, D, tn), lambda i, j: (layer, 0, j)),
        ],
        out_specs=pl.BlockSpec((tm, tn), lambda i, j: (i, j)),
        out_shape=jax.ShapeDtypeStruct((T, n_cols), out_dtype),
        scratch_shapes=[pltpu.VMEM((tm, D), BF16)],
        compiler_params=_params("parallel", "arbitrary"),
        name="proj",
    )(x, w)


def _out_ln_kernel(x_ref, o_ref, w_ref, g_ref, b_ref, y_ref, *, alpha):
    for r in range(0, x_ref.shape[0], ROW_SPLIT):
        rows = slice(r, r + ROW_SPLIT)
        z = alpha * x_ref[rows, :] + _dot(o_ref[rows, :], w_ref[...])
        y_ref[rows, :] = _layer_norm(z, g_ref[...], b_ref[...])


def _out_ln(x, o, w, layer, ln_g, ln_b, ln, alpha, tm=512):
    T, D = x.shape
    K = o.shape[1]
    assert T % tm == 0 and tm % ROW_SPLIT == 0 and w.shape[1:] == (K, D)
    return pl.pallas_call(
        functools.partial(_out_ln_kernel, alpha=alpha),
        grid=(T // tm,),
        in_specs=[
            pl.BlockSpec((tm, D), lambda i: (i, 0)),
            pl.BlockSpec((tm, K), lambda i: (i, 0)),
            pl.BlockSpec((None, K, D), lambda i: (layer, 0, 0)),
            _ln_specs(ln, D, 1),
            _ln_specs(ln, D, 1),
        ],
        out_specs=pl.BlockSpec((tm, D), lambda i: (i, 0)),
        out_shape=jax.ShapeDtypeStruct((T, D), F32),
        compiler_params=_params("parallel"),
        name="out_ln",
    )(x, o, w, ln_g, ln_b)


def _ple_kernel(x_ref, p_ref, wg_ref, wp_ref, g_ref, b_ref, y_ref, *, alpha):
    for r in range(0, x_ref.shape[0], ROW_SPLIT):
        rows = slice(r, r + ROW_SPLIT)
        x = x_ref[rows, :]
        gate = jax.nn.sigmoid(_dot(x.astype(BF16), wg_ref[...]))
        emb = _dot(p_ref[rows, :].astype(BF16), wp_ref[...])
        y_ref[rows, :] = _layer_norm(alpha * x + gate * emb, g_ref[...], b_ref[...])


def _ple(x, p, wg, wp, layer, ln_g, ln_b, ln, alpha, tm=512):
    T, D = x.shape
    P = p.shape[2]
    assert T % tm == 0 and tm % ROW_SPLIT == 0
    return pl.pallas_call(
        functools.partial(_ple_kernel, alpha=alpha),
        grid=(T // tm,),
        in_specs=[
            pl.BlockSpec((tm, D), lambda i: (i, 0)),
            pl.BlockSpec((None, tm, P), lambda i: (layer, i, 0)),
            pl.BlockSpec((None, D, D), lambda i: (layer, 0, 0)),
            pl.BlockSpec((None, P, D), lambda i: (layer, 0, 0)),
            _ln_specs(ln, D, 1),
            _ln_specs(ln, D, 1),
        ],
        out_specs=pl.BlockSpec((tm, D), lambda i: (i, 0)),
        out_shape=jax.ShapeDtypeStruct((T, D), F32),
        compiler_params=_params("parallel"),
        name="ple",
    )(x, p, wg, wp, ln_g, ln_b)


def _row_max_bcast(mx_ref):
    m = jnp.max(mx_ref[...], -1, keepdims=True)
    mx_ref[...] = jnp.broadcast_to(m, mx_ref.shape)


def _causal_sweep(fn, qi):
    @pl.when(qi % 2 == 1)
    def _():
        fn(qi - 1, 2, True)

    @pl.when(qi % 2 == 0)
    def _():
        fn(qi, 1, True)

    def body(j, carry):
        fn(2 * j, 2, False)
        return carry

    lax.fori_loop(0, qi // 2, body, 0)


def _row_minus_col(rows, cols):
    r = lax.broadcasted_iota(jnp.int32, (rows, cols), 0)
    c = lax.broadcasted_iota(jnp.int32, (rows, cols), 1)
    return r - c


DIFF_HEADS_PER_STEP = 2


def _diff_attn_kernel(slopes_ref, lam_ref, subg_ref, q_ref, k_ref, v_ref, o_ref,
                      s_ref, bias_ref, mx_ref, ls_ref, acc_ref, *, d, scale, lambda_init):
    hp = DIFF_HEADS_PER_STEP
    h0 = pl.program_id(1) * hp
    qi = pl.program_id(2)
    tq = q_ref.shape[0]
    units = range(2 * hp)
    slopes = [slopes_ref[h0 + hh] * (1.0 / scale) for hh in range(hp)]
    q = [q_ref[:, u * d:(u + 1) * d] for u in units]

    @pl.when(qi == 0)
    def _():
        rel = _row_minus_col(tq, 2 * tq).astype(F32)
        for hh in range(hp):
            bias_ref[hh] = -slopes[hh] * rel

    def span_scores(c0, width, diagonal):
        rows = pl.ds(pl.multiple_of(c0 * tq, tq), width * tq)
        for hh in range(hp):
            bias = bias_ref[hh, :, :width * tq] - slopes[hh] * ((qi - c0) * tq).astype(F32)
            for u in (2 * hh, 2 * hh + 1):
                s = _dot_nt(q[u], k_ref[rows, u * d:(u + 1) * d]) + bias
                if diagonal:
                    s = jnp.where(_row_minus_col(tq, width * tq) + (qi - c0) * tq >= 0, s, NEG)
                chunks = [s[:, w * tq:(w + 1) * tq] for w in range(width)]
                for w in range(width):
                    s_ref[u, c0 + w] = chunks[w]
                mx = functools.reduce(jnp.maximum, chunks)
                mx_ref[u] = mx if diagonal else jnp.maximum(mx_ref[u], mx)

    def span_accumulate(c0, width, diagonal):
        rows = pl.ds(pl.multiple_of(c0 * tq, tq), width * tq)
        for u in units:
            hh = u // 2
            ps = [jnp.exp2((s_ref[u, c0 + w] - mx_ref[u]) * (scale * LOG2E)) for w in range(width)]
            psum = functools.reduce(lambda a, b: a + b, ps)
            p = ps[0] if width == 1 else jnp.concatenate(ps, axis=1)
            pv = _dot(p.astype(BF16), v_ref[rows, hh * 2 * d:(hh + 1) * 2 * d])
            if diagonal:
                ls_ref[u] = psum
                acc_ref[u] = pv
            else:
                ls_ref[u] += psum
                acc_ref[u] += pv

    _causal_sweep(span_scores, qi)
    for u in units:
        _row_max_bcast(mx_ref.at[u])
    _causal_sweep(span_accumulate, qi)

    lam_p = lam_ref[...]
    lam = (jnp.exp(jnp.sum(lam_p[0:1] * lam_p[1:2], -1, keepdims=True))
           - jnp.exp(jnp.sum(lam_p[2:3] * lam_p[3:4], -1, keepdims=True)) + lambda_init)
    for hh in range(hp):
        l0 = jnp.sum(ls_ref[2 * hh], -1, keepdims=True)
        l1 = jnp.sum(ls_ref[2 * hh + 1], -1, keepdims=True)
        o = acc_ref[2 * hh] / l0 - lam * (acc_ref[2 * hh + 1] / l1)
        o = o * lax.rsqrt(jnp.mean(o * o, -1, keepdims=True) + RMS_EPS) * subg_ref[...] * (1.0 - lambda_init)
        o_ref[:, hh * 2 * d:(hh + 1) * 2 * d] = o.astype(o_ref.dtype)


def _diff_attn_static_kernel(slopes_ref, lam_ref, subg_ref, q_ref, k_ref, v_ref, o_ref,
                             s_ref, bias_ref, mx_ref, ls_ref, acc_ref, *, d, tq, scale, lambda_init):
    hp = DIFF_HEADS_PER_STEP
    h0 = pl.program_id(1) * hp
    nq = q_ref.shape[0] // tq
    units = range(2 * hp)
    slopes = [slopes_ref[h0 + hh] * (1.0 / scale) for hh in range(hp)]

    rel = _row_minus_col(tq, 2 * tq).astype(F32)
    for hh in range(hp):
        bias_ref[hh] = -slopes[hh] * rel
    lam_p = lam_ref[...]
    lam = (jnp.exp(jnp.sum(lam_p[0:1] * lam_p[1:2], -1, keepdims=True))
           - jnp.exp(jnp.sum(lam_p[2:3] * lam_p[3:4], -1, keepdims=True)) + lambda_init)

    for qi in range(nq):
        par = qi % 2
        q = [q_ref[qi * tq:(qi + 1) * tq, u * d:(u + 1) * d] for u in units]
        spans = [(qi - 1, 2, True)] if qi % 2 else [(qi, 1, True)]
        spans += [(2 * j, 2, False) for j in range(qi // 2)]

        for c0, width, diagonal in spans:
            for hh in range(hp):
                bias = bias_ref[hh, :, :width * tq] - slopes[hh] * float((qi - c0) * tq)
                for u in (2 * hh, 2 * hh + 1):
                    s = _dot_nt(q[u], k_ref[c0 * tq:(c0 + width) * tq, u * d:(u + 1) * d]) + bias
                    if diagonal:
                        s = jnp.where(_row_minus_col(tq, width * tq) + (qi - c0) * tq >= 0, s, NEG)
                    chunks = [s[:, w * tq:(w + 1) * tq] for w in range(width)]
                    for w in range(width):
                        s_ref[par, u, c0 + w] = chunks[w]
                    mx = functools.reduce(jnp.maximum, chunks)
                    mx_ref[par, u] = mx if diagonal else jnp.maximum(mx_ref[par, u], mx)

        for u in units:
            _row_max_bcast(mx_ref.at[par, u])

        for c0, width, diagonal in spans:
            for u in units:
                hh = u // 2
                ps = [jnp.exp2((s_ref[par, u, c0 + w] - mx_ref[par, u]) * (scale * LOG2E))
                      for w in range(width)]
                psum = functools.reduce(lambda a, b: a + b, ps)
                p = ps[0] if width == 1 else jnp.concatenate(ps, axis=1)
                pv = _dot(p.astype(BF16), v_ref[c0 * tq:(c0 + width) * tq, hh * 2 * d:(hh + 1) * 2 * d])
                if diagonal:
                    ls_ref[par, u] = psum
                    acc_ref[par, u] = pv
                else:
                    ls_ref[par, u] += psum
                    acc_ref[par, u] += pv

        for hh in range(hp):
            l0 = jnp.sum(ls_ref[par, 2 * hh], -1, keepdims=True)
            l1 = jnp.sum(ls_ref[par, 2 * hh + 1], -1, keepdims=True)
            o = acc_ref[par, 2 * hh] / l0 - lam * (acc_ref[par, 2 * hh + 1] / l1)
            o = o * lax.rsqrt(jnp.mean(o * o, -1, keepdims=True) + RMS_EPS) * subg_ref[...] * (1.0 - lambda_init)
            o_ref[qi * tq:(qi + 1) * tq, hh * 2 * d:(hh + 1) * 2 * d] = o.astype(o_ref.dtype)


def _diff_attn(qkv, lam_p, subln_g, B, S, H, d, lambda_init, tq=256):
    T = B * S
    nq = S // tq
    hp = DIFF_HEADS_PER_STEP
    w = hp * 2 * d
    assert S % tq == 0 and H % hp == 0 and qkv.shape == (T, 3 * H * 2 * d)
    slopes = jnp.asarray(_alibi_slopes(H), F32)
    return pl.pallas_call(
        functools.partial(_diff_attn_static_kernel, d=d, tq=tq, scale=d ** -0.5, lambda_init=lambda_init),
        grid=(B, H // hp),
        in_specs=[
            pl.BlockSpec(memory_space=pltpu.SMEM),
            pl.BlockSpec((4, d), lambda b, h: (0, 0)),
            pl.BlockSpec((1, 2 * d), lambda b, h: (0, 0)),
            pl.BlockSpec((S, w), lambda b, h: (b, h)),
            pl.BlockSpec((S, w), lambda b, h: (b, H // hp + h)),
            pl.BlockSpec((S, w), lambda b, h: (b, 2 * (H // hp) + h)),
        ],
        out_specs=pl.BlockSpec((S, w), lambda b, h: (b, h)),
        out_shape=jax.ShapeDtypeStruct((T, H * 2 * d), BF16),
        scratch_shapes=[pltpu.VMEM((2, 2 * hp, nq, tq, tq), F32), pltpu.VMEM((hp, tq, 2 * tq), F32),
                        pltpu.VMEM((2, 2 * hp, tq, tq), F32), pltpu.VMEM((2, 2 * hp, tq, tq), F32),
                        pltpu.VMEM((2, 2 * hp, tq, 2 * d), F32)],
        compiler_params=_params("parallel", "parallel"),
        name="diff_attn",
    )(slopes, lam_p, subln_g, qkv, qkv, qkv)


def _diff_attn_looped(qkv, lam_p, subln_g, B, S, H, d, lambda_init, tq=256):
    T = B * S
    nq = S // tq
    hp = DIFF_HEADS_PER_STEP
    w = hp * 2 * d
    assert S % tq == 0 and H % hp == 0 and qkv.shape == (T, 3 * H * 2 * d)
    slopes = jnp.asarray(_alibi_slopes(H), F32)
    return pl.pallas_call(
        functools.partial(_diff_attn_kernel, d=d, scale=d ** -0.5, lambda_init=lambda_init),
        grid=(B, H // hp, nq),
        in_specs=[
            pl.BlockSpec(memory_space=pltpu.SMEM),
            pl.BlockSpec((4, d), lambda b, h, i: (0, 0)),
            pl.BlockSpec((1, 2 * d), lambda b, h, i: (0, 0)),
            pl.BlockSpec((tq, w), lambda b, h, i: (b * nq + i, h)),
            pl.BlockSpec((S, w), lambda b, h, i: (b, H // hp + h)),
            pl.BlockSpec((S, w), lambda b, h, i: (b, 2 * (H // hp) + h)),
        ],
        out_specs=pl.BlockSpec((tq, w), lambda b, h, i: (b * nq + i, h)),
        out_shape=jax.ShapeDtypeStruct((T, H * 2 * d), BF16),
        scratch_shapes=[pltpu.VMEM((2 * hp, nq, tq, tq), F32), pltpu.VMEM((hp, tq, 2 * tq), F32),
                        pltpu.VMEM((2 * hp, tq, tq), F32), pltpu.VMEM((2 * hp, tq, tq), F32),
                        pltpu.VMEM((2 * hp, tq, 2 * d), F32)],
        compiler_params=_params("parallel", "parallel", "arbitrary"),
        name="diff_attn",
    )(slopes, lam_p, subln_g, qkv, qkv, qkv)


MOBA_HEADS_PER_STEP = 8


def _moba_kernel(slopes_ref, q_ref, k_ref, v_ref, o_ref, km_ref, mb_ref, s_ref, bias_ref,
                 mx_ref, ls_ref, acc_ref, *, nb, n_sel, d, scale):
    hp = MOBA_HEADS_PER_STEP
    h0 = pl.program_id(1) * hp
    own = pl.program_id(2)
    bs = q_ref.shape[0]
    nbp = -(-nb // SUBLANES) * SUBLANES
    heads = range(hp)
    cols = [slice(hh * d, (hh + 1) * d) for hh in heads]
    slopes = [slopes_ref[h0 + hh] * (1.0 / scale) for hh in heads]

    @pl.when(own == 0)
    def _():
        km_ref[...] = jnp.zeros_like(km_ref)
        for hh in heads:
            for n in range(nb):
                km_ref[hh, n:n + 1, :] = jnp.mean(
                    k_ref[n * bs:(n + 1) * bs, cols[hh]].astype(F32), 0, keepdims=True)

    q = [q_ref[:, cols[hh]] for hh in heads]
    rel_i = _row_minus_col(bs, bs)
    block = lax.broadcasted_iota(jnp.int32, (nbp, bs), 0)
    for hh in heads:
        km = km_ref[hh]
        km_hi = km.astype(BF16)
        km_lo = (km - km_hi.astype(F32)).astype(BF16)
        gate = (_dot_nt(km_hi, q[hh]) + _dot_nt(km_lo, q[hh]))[0:nbp, :]
        rank = jnp.zeros((nbp, bs), jnp.int32)
        for m in range(nb - 1):
            gm = gate[m:m + 1, :]
            beats = (gm > gate) | ((gm == gate) & (block > m))
            rank = rank + beats.astype(jnp.int32) * (own > m).astype(jnp.int32)
        sel = (rank < n_sel) & (block < own)
        off = -slopes[hh] * ((own - block) * bs).astype(F32)
        mb_t = jnp.where(sel, off, NEG)
        mb = jnp.transpose(jnp.concatenate([mb_t, jnp.zeros((LANES - nbp, bs), F32)], axis=0))
        for n in range(nb - 1):
            mb_ref[hh, n] = jnp.broadcast_to(mb[:, n:n + 1], (bs, LANES))

    @pl.when(own == 0)
    def _():
        for hh in heads:
            bias_ref[hh] = -slopes[hh] * rel_i.astype(F32)

    def span_scores(c0, width, diagonal):
        rows = pl.ds(pl.multiple_of(c0 * bs, bs), width * bs)
        for hh in heads:
            qk = _dot_nt(q[hh], k_ref[rows, cols[hh]])
            chunks = []
            for w in range(width):
                sw = qk[:, w * bs:(w + 1) * bs] + bias_ref[hh]
                if diagonal and w == width - 1:
                    sw = jnp.where(rel_i >= 0, sw, NEG)
                else:
                    sw = sw + jnp.concatenate([mb_ref[hh, c0 + w]] * (bs // LANES), axis=1)
                s_ref[hh, c0 + w] = sw
                chunks.append(sw)
            mx = functools.reduce(jnp.maximum, chunks)
            mx_ref[hh] = mx if diagonal else jnp.maximum(mx_ref[hh], mx)

    def span_accumulate(c0, width, diagonal):
        rows = pl.ds(pl.multiple_of(c0 * bs, bs), width * bs)
        for hh in heads:
            ps = [jnp.exp2((s_ref[hh, c0 + w] - mx_ref[hh]) * (scale * LOG2E)) for w in range(width)]
            psum = functools.reduce(lambda a, b: a + b, ps)
            p = ps[0] if width == 1 else jnp.concatenate(ps, axis=1)
            pv = _dot(p.astype(BF16), v_ref[rows, cols[hh]])
            if diagonal:
                ls_ref[hh] = psum
                acc_ref[hh] = pv
            else:
                ls_ref[hh] += psum
                acc_ref[hh] += pv

    _causal_sweep(span_scores, own)
    for hh in heads:
        _row_max_bcast(mx_ref.at[hh])
    _causal_sweep(span_accumulate, own)

    for hh in heads:
        l = jnp.sum(ls_ref[hh], -1, keepdims=True)
        o_ref[:, cols[hh]] = (acc_ref[hh] / l).astype(o_ref.dtype)


def _moba_attn(qkv, B, S, H, d, bs):
    T = B * S
    nb = S // bs
    hp = MOBA_HEADS_PER_STEP
    assert S % bs == 0 and nb <= LANES and H % hp == 0 and qkv.shape == (T, 3 * H * d)
    n_sel = min(MOBA_TOPK, nb - 1)
    slopes = jnp.asarray(_alibi_slopes(H), F32)
    return pl.pallas_call(
        functools.partial(_moba_kernel, nb=nb, n_sel=n_sel, d=d, scale=d ** -0.5),
        grid=(B, H // hp, nb),
        in_specs=[
            pl.BlockSpec(memory_space=pltpu.SMEM),
            pl.BlockSpec((bs, hp * d), lambda b, h, i: (b * nb + i, h)),
            pl.BlockSpec((S, hp * d), lambda b, h, i: (b, H // hp + h)),
            pl.BlockSpec((S, hp * d), lambda b, h, i: (b, 2 * (H // hp) + h)),
        ],
        out_specs=pl.BlockSpec((bs, hp * d), lambda b, h, i: (b * nb + i, h)),
        out_shape=jax.ShapeDtypeStruct((T, H * d), BF16),
        scratch_shapes=[pltpu.VMEM((hp, LANES, d), F32), pltpu.VMEM((hp, nb, bs, LANES), F32),
                        pltpu.VMEM((hp, nb, bs, bs), F32), pltpu.VMEM((hp, bs, bs), F32),
                        pltpu.VMEM((hp, bs, bs), F32), pltpu.VMEM((hp, bs, bs), F32),
                        pltpu.VMEM((hp, bs, d), F32)],
        compiler_params=_params("parallel", "parallel", "arbitrary"),
        name="moba_attn",
    )(slopes, qkv, qkv, qkv)


def _dsa_kernel(q_ref, k_ref, v_ref, qi_ref, tailq_ref, tailk_ref, o_ref,
                kib_ref, qis_ref, key_ref, thr_ref, s_ref, mx_ref, ls_ref, acc_ref, *, n_keep, tkc, scale):
    i = pl.program_id(1)
    tq = q_ref.shape[0]
    G, R, d = C_KV_HEADS, C_HEADS // C_KV_HEADS, C_HEAD_DIM
    qstart = i * tq
    nkc = (qstart + tq + tkc - 1) // tkc

    @pl.when(i == 0)
    def _():
        kib_ref[...] = tailk_ref[:, 0:IDX_DIM].astype(BF16)

    w_t = (jnp.transpose(tailq_ref[...])[IDX_DIM:IDX_DIM + IDX_HEADS, :]
           * (IDX_HEADS ** -0.5 * IDX_DIM ** -0.5))
    key_minus_query = _row_minus_col(tkc, tq)

    for hh in range(IDX_HEADS):
        qis_ref[hh * tq:(hh + 1) * tq, :] = qi_ref[:, hh * IDX_DIM:(hh + 1) * IDX_DIM]

    def score_chunk(c, carry):
        kc = kib_ref[pl.ds(pl.multiple_of(c * tkc, tkc), tkc), :]
        dots = _dot_nt(kc, qis_ref[...])
        acc = jnp.zeros((tkc, tq), F32)
        for hh in range(IDX_HEADS):
            acc = acc + w_t[hh:hh + 1, :] * jnp.maximum(dots[:, hh * tq:(hh + 1) * tq], 0.0)
        bits = lax.bitcast_convert_type(acc, jnp.int32)
        key = bits ^ ((bits >> 31) & jnp.int32(0x7FFFFFFF))
        causal = key_minus_query + (c * tkc - qstart) <= 0
        key_ref[c] = jnp.where(causal, key, jnp.int32(INT_MIN))
        return carry

    lax.fori_loop(0, nkc, score_chunk, 0)

    def count(pred):
        def body(c, part):
            hit = pred(key_ref[c]).astype(jnp.int32)
            return part + jnp.sum(hit.reshape(tkc // SUBLANES, SUBLANES, tq), axis=0)
        part = lax.fori_loop(0, nkc, body, jnp.zeros((SUBLANES, tq), jnp.int32))
        return jnp.sum(part, axis=0, keepdims=True)

    thr_ref[...] = jnp.full_like(thr_ref, INT_MIN + 1)

    @pl.when(qstart + tq > n_keep)
    def _():
        zero = jnp.zeros((1, tq), jnp.int32)
        thr0 = jnp.where(count(lambda k: k >= zero) >= n_keep, jnp.int32(0), jnp.int32(INT_MIN))

        def bit_step(b, thr):
            cand = thr | jnp.left_shift(jnp.int32(1), 30 - b)
            return jnp.where(count(lambda k: k >= cand) >= n_keep, cand, thr)

        thr = jnp.maximum(lax.fori_loop(0, 31, bit_step, thr0), jnp.int32(INT_MIN + 1))
        thr_ref[...] = thr

        @pl.when(jnp.max(count(lambda k: k >= thr)) > n_keep)
        def _():
            room = (n_keep - count(lambda k: k > thr)).astype(F32)
            lower_tri = jnp.where(_row_minus_col(tkc, tkc) >= 0, 1.0, 0.0).astype(BF16)

            def drop_excess(c, seen):
                key = key_ref[c]
                tie = key == thr
                tie_f = jnp.where(tie, 1.0, 0.0)
                rank = seen + _dot(lower_tri, tie_f.astype(BF16))
                key_ref[c] = jnp.where(tie & (rank > room), jnp.int32(INT_MIN), key)
                return seen + jnp.sum(tie_f, axis=0, keepdims=True)

            lax.fori_loop(0, nkc, drop_excess, jnp.zeros((1, tq), F32))

    thr = thr_ref[...]
    slopes = _alibi_slopes(C_HEADS).reshape(G, R)
    rel = _row_minus_col(tq, tkc).astype(F32)

    def score_pass(c, first):
        rows = pl.ds(pl.multiple_of(c * tkc, tkc), tkc)
        mask = jnp.transpose(jnp.where(key_ref[c] >= thr, 0.0, NEG))
        dist = rel + (qstart - c * tkc).astype(F32)
        for g in range(G):
            qs = jnp.concatenate(
                [q_ref[:, (g * R + r) * d:(g * R + r + 1) * d] for r in range(R)], axis=0)
            bias = jnp.concatenate([mask - float(slopes[g, r] / scale) * dist for r in range(R)], axis=0)
            s = _dot_nt(qs, k_ref[rows, g * d:(g + 1) * d]) + bias
            s_ref[g, c] = s
            mx_ref[g] = s if first else jnp.maximum(mx_ref[g], s)

    def acc_pass(c, first):
        rows = pl.ds(pl.multiple_of(c * tkc, tkc), tkc)
        for g in range(G):
            p = jnp.exp2((s_ref[g, c] - mx_ref[g]) * (scale * LOG2E))
            pv = _dot(p.astype(BF16), v_ref[rows, g * d:(g + 1) * d])
            if first:
                ls_ref[g] = p
                acc_ref[g] = pv
            else:
                ls_ref[g] += p
                acc_ref[g] += pv

    def sweep(fn):
        fn(0, True)

        def body(c, carry):
            fn(c, False)
            return carry

        lax.fori_loop(1, nkc, body, 0)

    sweep(score_pass)
    for g in range(G):
        _row_max_bcast(mx_ref.at[g])
    sweep(acc_pass)

    for g in range(G):
        o = acc_ref[g] / jnp.sum(ls_ref[g], -1, keepdims=True)
        for r in range(R):
            o_ref[:, (g * R + r) * d:(g * R + r + 1) * d] = o[r * tq:(r + 1) * tq].astype(o_ref.dtype)


def _dsa_attn(main, tail, B, S, tq=128, tkc=256):
    T = B * S
    nq = S // tq
    G, H, d = C_KV_HEADS, C_HEADS, C_HEAD_DIM
    R = H // G
    c_q, c_kv, c_iq = H * d, G * d, IDX_HEADS * IDX_DIM
    assert S % tq == 0 and S % tkc == 0 and tkc % tq == 0
    assert main.shape == (T, c_q + 2 * c_kv + c_iq) and tail.shape == (T, LANES)
    assert c_q % c_kv == 0 and (c_q + 2 * c_kv) % c_iq == 0
    n_keep = min(DSA_TOPK_MAX, S // 4)
    return pl.pallas_call(
        functools.partial(_dsa_kernel, n_keep=n_keep, tkc=tkc, scale=d ** -0.5),
        grid=(B, nq),
        in_specs=[
            pl.BlockSpec((tq, c_q), lambda b, i: (b * nq + i, 0)),
            pl.BlockSpec((S, c_kv), lambda b, i: (b, c_q // c_kv)),
            pl.BlockSpec((S, c_kv), lambda b, i: (b, c_q // c_kv + 1)),
            pl.BlockSpec((tq, c_iq), lambda b, i: (b * nq + i, (c_q + 2 * c_kv) // c_iq)),
            pl.BlockSpec((tq, LANES), lambda b, i: (b * nq + i, 0)),
            pl.BlockSpec((S, LANES), lambda b, i: (b, 0)),
        ],
        out_specs=pl.BlockSpec((tq, c_q), lambda b, i: (b * nq + i, 0)),
        out_shape=jax.ShapeDtypeStruct((T, c_q), BF16),
        scratch_shapes=[pltpu.VMEM((S, IDX_DIM), BF16), pltpu.VMEM((IDX_HEADS * tq, IDX_DIM), BF16),
                        pltpu.VMEM((S // tkc, tkc, tq), jnp.int32),
                        pltpu.VMEM((1, tq), jnp.int32),
                        pltpu.VMEM((G, S // tkc, R * tq, tkc), F32),
                        pltpu.VMEM((G, R * tq, tkc), F32), pltpu.VMEM((G, R * tq, tkc), F32),
                        pltpu.VMEM((G, R * tq, d), F32)],
        compiler_params=_params("parallel", "arbitrary"),
        name="dsa_attn",
    )(main, main, main, main, tail, tail)


def kernel(x, p, ffn1_w_in, ffn1_w_out, ffn2_w_in, ffn2_w_out, ln_g, ln_b, ple_w_gate, ple_w_proj,
           a_w_in, a_w_out, a_lam_q1, a_lam_k1, a_lam_q2, a_lam_k2, a_subln_g, b_w_in, b_w_out,
           c_w_in, c_w_out):
    B, S, D = x.shape
    depth = p.shape[0]
    T = B * S
    alpha = (2.0 * depth) ** 0.25
    bf = lambda w: w.astype(BF16)
    ple_g, ple_p = bf(ple_w_gate), bf(ple_w_proj)
    a_in, a_out, b_in, b_out, c_out = bf(a_w_in), bf(a_w_out), bf(b_w_in), bf(b_w_out), bf(c_w_out)
    n_main = C_HEADS * C_HEAD_DIM + 2 * C_KV_HEADS * C_HEAD_DIM + IDX_HEADS * IDX_DIM
    c_in = bf(c_w_in)
    c_tail = c_w_in[:, :, n_main:]
    c_tail = bf(jnp.pad(c_tail, ((0, 0), (0, 0), (0, LANES - c_tail.shape[2]))))
    lng = ln_g.reshape(-1, 1, D).astype(F32)
    lnb = ln_b.reshape(-1, 1, D).astype(F32)
    n_ln = ln_g.shape[1]
    p = p.reshape(depth, T, -1)
    x = x.reshape(T, D)
    for i in range(depth):
        m, j = i % N_MIXERS, i // N_MIXERS
        x = _ffn(x, ffn1_w_in, ffn1_w_out, i, lng, lnb, i * n_ln, alpha)
        if m == 0:
            d = A_HEAD_DIM
            H = D // (2 * d)
            lambda_init = 0.8 - 0.6 * math.exp(-0.3 * i)
            qkv = _proj(x, a_in, j, a_in.shape[2], BF16)
            lam_p = jnp.stack([a_lam_q1[j], a_lam_k1[j], a_lam_q2[j], a_lam_k2[j]]).astype(F32)
            o = _diff_attn(qkv, lam_p, a_subln_g[j].reshape(1, 2 * d).astype(F32), B, S, H, d, lambda_init)
            w_out = a_out
        elif m == 1:
            qkv = _proj(x, b_in, j, b_in.shape[2], BF16)
            o = _moba_attn(qkv, B, S, B_HEADS, B_HEAD_DIM, MOBA_BLOCK)
            w_out = b_out
        else:
            main = _proj(x, c_in, j, n_main, BF16)
            tail = _proj(x, c_tail, j, LANES, F32)
            o = _dsa_attn(main, tail, B, S)
            w_out = c_out
        x = _out_ln(x, o, w_out, j, lng, lnb, i * n_ln + 1, alpha)
        x = _ffn(x, ffn2_w_in, ffn2_w_out, i, lng, lnb, i * n_ln + 2, alpha)
        x = _ple(x, p, ple_g, ple_p, i, lng, lnb, i * n_ln + 3, alpha)
    return x.reshape(B, S, D)
```

```python
import functools
import math

import numpy as np
import jax
import jax.numpy as jnp
from jax import lax
from jax.experimental import pallas as pl
from jax.experimental.pallas import tpu as pltpu

F32 = jnp.float32
BF16 = jnp.bfloat16

N_MIXERS = 3
A_HEAD_DIM = 128
B_HEADS = 16
B_HEAD_DIM = 128
MOBA_BLOCK = 256
MOBA_TOPK = 3
C_HEADS = 16
C_KV_HEADS = 4
C_HEAD_DIM = 128
IDX_HEADS = 16
IDX_DIM = 64
DSA_TOPK_MAX = 256
LN_EPS = 1e-5
RMS_EPS = 1e-6

LANES = 128
SUBLANES = 8
NEG = -1e30
INT_MIN = -(2 ** 31)
LOG2E = math.log2(math.e)
VMEM_LIMIT = 60 * 1024 * 1024
ROW_SPLIT = 128


def _alibi_slopes(n):
    return 2.0 ** (-8.0 * np.arange(1, n + 1, dtype=np.float32) / n)


def _params(*sem):
    return pltpu.CompilerParams(dimension_semantics=sem, vmem_limit_bytes=VMEM_LIMIT)


def _dot(a, b):
    return jnp.dot(a, b, preferred_element_type=F32)


def _dot_nt(a, b):
    return lax.dot_general(a, b, (((1,), (1,)), ((), ())), preferred_element_type=F32)


def _layer_norm(z, g, b):
    mu = jnp.mean(z, -1, keepdims=True)
    zc = z - mu
    var = jnp.mean(zc * zc, -1, keepdims=True)
    return zc * lax.rsqrt(var + LN_EPS) * g + b


def _ffn_kernel(x_ref, wg_ref, wu_ref, wo_ref, g_ref, b_ref, o_ref, xb_ref, *, alpha):
    j = pl.program_id(1)

    @pl.when(j == 0)
    def _():
        xb_ref[...] = x_ref[...].astype(BF16)
        o_ref[...] = jnp.zeros_like(o_ref)

    xb = xb_ref[...]
    gate = _dot(xb, wg_ref[...].astype(BF16))
    up = _dot(xb, wu_ref[...].astype(BF16))
    h = (gate * jax.nn.sigmoid(gate) * up).astype(BF16)
    o_ref[...] += _dot(h, wo_ref[...].astype(BF16))

    @pl.when(j == pl.num_programs(1) - 1)
    def _():
        for r in range(0, o_ref.shape[0], ROW_SPLIT):
            rows = slice(r, r + ROW_SPLIT)
            z = alpha * x_ref[rows, :] + 0.5 * o_ref[rows, :]
            o_ref[rows, :] = _layer_norm(z, g_ref[...], b_ref[...])


def _ln_specs(ln, D, ngrid):
    if ngrid == 1:
        return pl.BlockSpec((None, 1, D), lambda i: (ln, 0, 0))
    return pl.BlockSpec((None, 1, D), lambda i, j: (ln, 0, 0))


def _ffn(x, w_in, w_out, layer, ln_g, ln_b, ln, alpha, tm=1024, tf=256):
    T, D = x.shape
    F = w_out.shape[1]
    nf = F // tf
    assert T % tm == 0 and F % tf == 0 and w_in.shape[1:] == (D, 2 * F)
    return pl.pallas_call(
        functools.partial(_ffn_kernel, alpha=alpha),
        grid=(T // tm, nf),
        in_specs=[
            pl.BlockSpec((tm, D), lambda i, j: (i, 0)),
            pl.BlockSpec((None, D, tf), lambda i, j: (layer, 0, j)),
            pl.BlockSpec((None, D, tf), lambda i, j: (layer, 0, j + nf)),
            pl.BlockSpec((None, tf, D), lambda i, j: (layer, j, 0)),
            _ln_specs(ln, D, 2),
            _ln_specs(ln, D, 2),
        ],
        out_specs=pl.BlockSpec((tm, D), lambda i, j: (i, 0)),
        out_shape=jax.ShapeDtypeStruct((T, D), F32),
        scratch_shapes=[pltpu.VMEM((tm, D), BF16)],
        compiler_params=_params("parallel", "arbitrary"),
        name="ffn",
    )(x, w_in, w_in, w_out, ln_g, ln_b)


def _proj_kernel(x_ref, w_ref, o_ref, xb_ref):
    @pl.when(pl.program_id(1) == 0)
    def _():
        xb_ref[...] = x_ref[...].astype(BF16)

    o_ref[...] = _dot(xb_ref[...], w_ref[...]).astype(o_ref.dtype)


def _proj(x, w, layer, n_cols, out_dtype, tm=1024, tn=1024):
    T, D = x.shape
    tm, tn = min(tm, T), min(tn, n_cols)
    assert T % tm == 0 and n_cols % tn == 0 and w.shape[1] == D and w.shape[2] >= n_cols
    return pl.pallas_call(
        _proj_kernel,
        grid=(T // tm, n_cols // tn),
        in_specs=[
            pl.BlockSpec((tm, D), lambda i, j: (i, 0)),
            pl.BlockSpec((None, D, tn), lambda i, j: (layer, 0, j)),
        ],
        out_specs=pl.BlockSpec((tm, tn), lambda i, j: (i, j)),
        out_shape=jax.ShapeDtypeStruct((T, n_cols), out_dtype),
        scratch_shapes=[pltpu.VMEM((tm, D), BF16)],
        compiler_params=_params("parallel", "arbitrary"),
        name="proj",
    )(x, w)


def _out_ln_kernel(x_ref, o_ref, w_ref, g_ref, b_ref, y_ref, *, alpha):
    for r in range(0, x_ref.shape[0], ROW_SPLIT):
        rows = slice(r, r + ROW_SPLIT)
        z = alpha * x_ref[rows, :] + _dot(o_ref[rows, :], w_ref[...])
        y_ref[rows, :] = _layer_norm(z, g_ref[...], b_ref[...])


def _out_ln(x, o, w, layer, ln_g, ln_b, ln, alpha, tm=512):
    T, D = x.shape
    K = o.shape[1]
    assert T % tm == 0 and tm % ROW_SPLIT == 0 and w.shape[1:] == (K, D)
    return pl.pallas_call(
        functools.partial(_out_ln_kernel, alpha=alpha),
        grid=(T // tm,),
        in_specs=[
            pl.BlockSpec((tm, D), lambda i: (i, 0)),
            pl.BlockSpec((tm, K), lambda i: (i, 0)),
            pl.BlockSpec((None, K, D), lambda i: (layer, 0, 0)),
            _ln_specs(ln, D, 1),
            _ln_specs(ln, D, 1),
        ],
        out_specs=pl.BlockSpec((tm, D), lambda i: (i, 0)),
        out_shape=jax.ShapeDtypeStruct((T, D), F32),
        compiler_params=_params("parallel"),
        name="out_ln",
    )(x, o, w, ln_g, ln_b)


def _ple_kernel(x_ref, p_ref, wg_ref, wp_ref, g_ref, b_ref, y_ref, *, alpha):
    for r in range(0, x_ref.shape[0], ROW_SPLIT):
        rows = slice(r, r + ROW_SPLIT)
        x = x_ref[rows, :]
        gate = jax.nn.sigmoid(_dot(x.astype(BF16), wg_ref[...]))
        emb = _dot(p_ref[rows, :].astype(BF16), wp_ref[...])
        y_ref[rows, :] = _layer_norm(alpha * x + gate * emb, g_ref[...], b_ref[...])


def _ple(x, p, wg, wp, layer, ln_g, ln_b, ln, alpha, tm=512):
    T, D = x.shape
    P = p.shape[2]
    assert T % tm == 0 and tm % ROW_SPLIT == 0
    return pl.pallas_call(
        functools.partial(_ple_kernel, alpha=alpha),
        grid=(T // tm,),
        in_specs=[
            pl.BlockSpec((tm, D), lambda i: (i, 0)),
            pl.BlockSpec((None, tm, P), lambda i: (layer, i, 0)),
            pl.BlockSpec((None, D, D), lambda i: (layer, 0, 0)),
            pl.BlockSpec((None, P, D), lambda i: (layer, 0, 0)),
            _ln_specs(ln, D, 1),
            _ln_specs(ln, D, 1),
        ],
        out_specs=pl.BlockSpec((tm, D), lambda i: (i, 0)),
        out_shape=jax.ShapeDtypeStruct((T, D), F32),
        compiler_params=_params("parallel"),
        name="ple",
    )(x, p, wg, wp, ln_g, ln_b)


def _row_max_bcast(mx_ref):
    m = jnp.max(mx_ref[...], -1, keepdims=True)
    mx_ref[...] = jnp.broadcast_to(m, mx_ref.shape)


def _causal_spans(qi):
    spans = [(qi - 1, 2, True)] if qi % 2 else [(qi, 1, True)]
    return spans + [(2 * j, 2, False) for j in range(qi // 2)]


def _row_minus_col(rows, cols):
    r = lax.broadcasted_iota(jnp.int32, (rows, cols), 0)
    c = lax.broadcasted_iota(jnp.int32, (rows, cols), 1)
    return r - c


DIFF_HEADS_PER_STEP = 2


def _diff_attn_kernel(slopes_ref, lam_ref, subg_ref, q_ref, k_ref, v_ref, o_ref,
                      s_ref, bias_ref, mx_ref, ls_ref, acc_ref, *, d, tq, scale, lambda_init):
    hp = DIFF_HEADS_PER_STEP
    h0 = pl.program_id(1) * hp
    nq = q_ref.shape[0] // tq
    units = range(2 * hp)
    slopes = [slopes_ref[h0 + hh] * (1.0 / scale) for hh in range(hp)]

    rel = _row_minus_col(tq, 2 * tq).astype(F32)
    for hh in range(hp):
        bias_ref[hh] = -slopes[hh] * rel
    lam_p = lam_ref[...]
    lam = (jnp.exp(jnp.sum(lam_p[0:1] * lam_p[1:2], -1, keepdims=True))
           - jnp.exp(jnp.sum(lam_p[2:3] * lam_p[3:4], -1, keepdims=True)) + lambda_init)

    for qi in range(nq):
        par = qi % 2
        q = [q_ref[qi * tq:(qi + 1) * tq, u * d:(u + 1) * d] for u in units]
        spans = _causal_spans(qi)

        for c0, width, diagonal in spans:
            for hh in range(hp):
                bias = bias_ref[hh, :, :width * tq] - slopes[hh] * float((qi - c0) * tq)
                for u in (2 * hh, 2 * hh + 1):
                    s = _dot_nt(q[u], k_ref[c0 * tq:(c0 + width) * tq, u * d:(u + 1) * d]) + bias
                    if diagonal:
                        s = jnp.where(_row_minus_col(tq, width * tq) + (qi - c0) * tq >= 0, s, NEG)
                    chunks = [s[:, w * tq:(w + 1) * tq] for w in range(width)]
                    for w in range(width):
                        s_ref[par, u, c0 + w] = chunks[w]
                    mx = functools.reduce(jnp.maximum, chunks)
                    mx_ref[par, u] = mx if diagonal else jnp.maximum(mx_ref[par, u], mx)

        for u in units:
            _row_max_bcast(mx_ref.at[par, u])

        for c0, width, diagonal in spans:
            for u in units:
                hh = u // 2
                ps = [jnp.exp2((s_ref[par, u, c0 + w] - mx_ref[par, u]) * (scale * LOG2E))
                      for w in range(width)]
                psum = functools.reduce(lambda a, b: a + b, ps)
                p = ps[0] if width == 1 else jnp.concatenate(ps, axis=1)
                pv = _dot(p.astype(BF16), v_ref[c0 * tq:(c0 + width) * tq, hh * 2 * d:(hh + 1) * 2 * d])
                if diagonal:
                    ls_ref[par, u] = psum
                    acc_ref[par, u] = pv
                else:
                    ls_ref[par, u] += psum
                    acc_ref[par, u] += pv

        for hh in range(hp):
            l0 = jnp.sum(ls_ref[par, 2 * hh], -1, keepdims=True)
            l1 = jnp.sum(ls_ref[par, 2 * hh + 1], -1, keepdims=True)
            o = acc_ref[par, 2 * hh] / l0 - lam * (acc_ref[par, 2 * hh + 1] / l1)
            o = o * lax.rsqrt(jnp.mean(o * o, -1, keepdims=True) + RMS_EPS) * subg_ref[...] * (1.0 - lambda_init)
            o_ref[qi * tq:(qi + 1) * tq, hh * 2 * d:(hh + 1) * 2 * d] = o.astype(o_ref.dtype)


def _diff_attn(qkv, lam_p, subln_g, B, S, H, d, lambda_init, tq=256):
    T = B * S
    nq = S // tq
    hp = DIFF_HEADS_PER_STEP
    w = hp * 2 * d
    assert S % tq == 0 and H % hp == 0 and qkv.shape == (T, 3 * H * 2 * d)
    slopes = jnp.asarray(_alibi_slopes(H), F32)
    return pl.pallas_call(
        functools.partial(_diff_attn_kernel, d=d, tq=tq, scale=d ** -0.5, lambda_init=lambda_init),
        grid=(B, H // hp),
        in_specs=[
            pl.BlockSpec(memory_space=pltpu.SMEM),
            pl.BlockSpec((4, d), lambda b, h: (0, 0)),
            pl.BlockSpec((1, 2 * d), lambda b, h: (0, 0)),
            pl.BlockSpec((S, w), lambda b, h: (b, h)),
            pl.BlockSpec((S, w), lambda b, h: (b, H // hp + h)),
            pl.BlockSpec((S, w), lambda b, h: (b, 2 * (H // hp) + h)),
        ],
        out_specs=pl.BlockSpec((S, w), lambda b, h: (b, h)),
        out_shape=jax.ShapeDtypeStruct((T, H * 2 * d), BF16),
        scratch_shapes=[pltpu.VMEM((2, 2 * hp, nq, tq, tq), F32), pltpu.VMEM((hp, tq, 2 * tq), F32),
                        pltpu.VMEM((2, 2 * hp, tq, tq), F32), pltpu.VMEM((2, 2 * hp, tq, tq), F32),
                        pltpu.VMEM((2, 2 * hp, tq, 2 * d), F32)],
        compiler_params=_params("parallel", "parallel"),
        name="diff_attn",
    )(slopes, lam_p, subln_g, qkv, qkv, qkv)


MOBA_HEADS_PER_STEP = 4


def _moba_kernel(slopes_ref, q_ref, k_ref, v_ref, o_ref, km_ref, mb_ref, s_ref, bias_ref,
                 mx_ref, ls_ref, acc_ref, *, nb, n_sel, d, bs, scale):
    hp = MOBA_HEADS_PER_STEP
    h0 = pl.program_id(1) * hp
    nbp = -(-nb // SUBLANES) * SUBLANES
    heads = range(hp)
    cols = [slice(hh * d, (hh + 1) * d) for hh in heads]
    slopes = [slopes_ref[h0 + hh] * (1.0 / scale) for hh in heads]

    km_ref[...] = jnp.zeros_like(km_ref)
    rel_i = _row_minus_col(bs, bs)
    for hh in heads:
        for n in range(nb):
            km_ref[hh, n:n + 1, :] = jnp.mean(
                k_ref[n * bs:(n + 1) * bs, cols[hh]].astype(F32), 0, keepdims=True)
        bias_ref[hh] = -slopes[hh] * rel_i.astype(F32)
    block = lax.broadcasted_iota(jnp.int32, (nbp, bs), 0)

    for own in range(nb):
        par = own % 2
        q = [q_ref[own * bs:(own + 1) * bs, cols[hh]] for hh in heads]
        for hh in heads if own > 0 else ():
            km = km_ref[hh]
            km_hi = km.astype(BF16)
            km_lo = (km - km_hi.astype(F32)).astype(BF16)
            gate = (_dot_nt(km_hi, q[hh]) + _dot_nt(km_lo, q[hh]))[0:nbp, :]
            rank = jnp.zeros((nbp, bs), jnp.int32)
            for m in range(own):
                gm = gate[m:m + 1, :]
                beats = (gm > gate) | ((gm == gate) & (block > m))
                rank = rank + beats.astype(jnp.int32)
            sel = (rank < n_sel) & (block < own)
            off = -slopes[hh] * ((own - block) * bs).astype(F32)
            mb_t = jnp.where(sel, off, NEG)
            mb = jnp.transpose(jnp.concatenate([mb_t, jnp.zeros((LANES - nbp, bs), F32)], axis=0))
            for n in range(own):
                mb_ref[par, hh, n] = jnp.broadcast_to(mb[:, n:n + 1], (bs, LANES))

        spans = _causal_spans(own)

        for c0, width, diagonal in spans:
            for hh in heads:
                qk = _dot_nt(q[hh], k_ref[c0 * bs:(c0 + width) * bs, cols[hh]])
                chunks = []
                for w in range(width):
                    sw = qk[:, w * bs:(w + 1) * bs] + bias_ref[hh]
                    if diagonal and w == width - 1:
                        sw = jnp.where(rel_i >= 0, sw, NEG)
                    else:
                        sw = sw + jnp.concatenate([mb_ref[par, hh, c0 + w]] * (bs // LANES), axis=1)
                    s_ref[par, hh, c0 + w] = sw
                    chunks.append(sw)
                mx = functools.reduce(jnp.maximum, chunks)
                mx_ref[par, hh] = mx if diagonal else jnp.maximum(mx_ref[par, hh], mx)

        for hh in heads:
            _row_max_bcast(mx_ref.at[par, hh])

        for c0, width, diagonal in spans:
            for hh in heads:
                ps = [jnp.exp2((s_ref[par, hh, c0 + w] - mx_ref[par, hh]) * (scale * LOG2E))
                      for w in range(width)]
                psum = functools.reduce(lambda a, b: a + b, ps)
                p = ps[0] if width == 1 else jnp.concatenate(ps, axis=1)
                pv = _dot(p.astype(BF16), v_ref[c0 * bs:(c0 + width) * bs, cols[hh]])
                if diagonal:
                    ls_ref[par, hh] = psum
                    acc_ref[par, hh] = pv
                else:
                    ls_ref[par, hh] += psum
                    acc_ref[par, hh] += pv

        for hh in heads:
            l = jnp.sum(ls_ref[par, hh], -1, keepdims=True)
            o_ref[own * bs:(own + 1) * bs, cols[hh]] = (acc_ref[par, hh] / l).astype(o_ref.dtype)


def _moba_attn(qkv, B, S, H, d, bs):
    T = B * S
    nb = S // bs
    hp = MOBA_HEADS_PER_STEP
    assert S % bs == 0 and nb <= LANES and H % hp == 0 and qkv.shape == (T, 3 * H * d)
    n_sel = min(MOBA_TOPK, nb - 1)
    slopes = jnp.asarray(_alibi_slopes(H), F32)
    return pl.pallas_call(
        functools.partial(_moba_kernel, nb=nb, n_sel=n_sel, d=d, bs=bs, scale=d ** -0.5),
        grid=(B, H // hp),
        in_specs=[
            pl.BlockSpec(memory_space=pltpu.SMEM),
            pl.BlockSpec((S, hp * d), lambda b, h: (b, h)),
            pl.BlockSpec((S, hp * d), lambda b, h: (b, H // hp + h)),
            pl.BlockSpec((S, hp * d), lambda b, h: (b, 2 * (H // hp) + h)),
        ],
        out_specs=pl.BlockSpec((S, hp * d), lambda b, h: (b, h)),
        out_shape=jax.ShapeDtypeStruct((T, H * d), BF16),
        scratch_shapes=[pltpu.VMEM((hp, LANES, d), F32), pltpu.VMEM((2, hp, nb, bs, LANES), F32),
                        pltpu.VMEM((2, hp, nb, bs, bs), F32), pltpu.VMEM((hp, bs, bs), F32),
                        pltpu.VMEM((2, hp, bs, bs), F32), pltpu.VMEM((2, hp, bs, bs), F32),
                        pltpu.VMEM((2, hp, bs, d), F32)],
        compiler_params=_params("parallel", "parallel"),
        name="moba_attn",
    )(slopes, qkv, qkv, qkv)


def _dsa_kernel(q_ref, k_ref, v_ref, qi_ref, tailq_ref, tailk_ref, o_ref,
                kib_ref, qis_ref, key_ref, thr_ref, s_ref, mx_ref, ls_ref, acc_ref, *, n_keep, tkc, scale):
    i = pl.program_id(1)
    tq = q_ref.shape[0]
    G, R, d = C_KV_HEADS, C_HEADS // C_KV_HEADS, C_HEAD_DIM
    qstart = i * tq
    nkc = (qstart + tq + tkc - 1) // tkc

    @pl.when(i == 0)
    def _():
        kib_ref[...] = tailk_ref[:, 0:IDX_DIM].astype(BF16)

    w_t = (jnp.transpose(tailq_ref[...])[IDX_DIM:IDX_DIM + IDX_HEADS, :]
           * (IDX_HEADS ** -0.5 * IDX_DIM ** -0.5))
    key_minus_query = _row_minus_col(tkc, tq)

    for hh in range(IDX_HEADS):
        qis_ref[hh * tq:(hh + 1) * tq, :] = qi_ref[:, hh * IDX_DIM:(hh + 1) * IDX_DIM]

    def score_chunk(c, carry):
        kc = kib_ref[pl.ds(pl.multiple_of(c * tkc, tkc), tkc), :]
        dots = _dot_nt(kc, qis_ref[...])
        acc = jnp.zeros((tkc, tq), F32)
        for hh in range(IDX_HEADS):
            acc = acc + w_t[hh:hh + 1, :] * jnp.maximum(dots[:, hh * tq:(hh + 1) * tq], 0.0)
        bits = lax.bitcast_convert_type(acc, jnp.int32)
        key = bits ^ ((bits >> 31) & jnp.int32(0x7FFFFFFF))
        causal = key_minus_query + (c * tkc - qstart) <= 0
        key_ref[c] = jnp.where(causal, key, jnp.int32(INT_MIN))
        return carry

    lax.fori_loop(0, nkc, score_chunk, 0)

    def count(pred):
        def body(c, part):
            hit = pred(key_ref[c]).astype(jnp.int32)
            return part + jnp.sum(hit.reshape(tkc // SUBLANES, SUBLANES, tq), axis=0)
        part = lax.fori_loop(0, nkc, body, jnp.zeros((SUBLANES, tq), jnp.int32))
        return jnp.sum(part, axis=0, keepdims=True)

    thr_ref[...] = jnp.full_like(thr_ref, INT_MIN + 1)

    @pl.when(qstart + tq > n_keep)
    def _():
        zero = jnp.zeros((1, tq), jnp.int32)
        thr0 = jnp.where(count(lambda k: k >= zero) >= n_keep, jnp.int32(0), jnp.int32(INT_MIN))

        def bit_step(b, thr):
            cand = thr | jnp.left_shift(jnp.int32(1), 30 - b)
            return jnp.where(count(lambda k: k >= cand) >= n_keep, cand, thr)

        thr = jnp.maximum(lax.fori_loop(0, 31, bit_step, thr0), jnp.int32(INT_MIN + 1))
        thr_ref[...] = thr

        @pl.when(jnp.max(count(lambda k: k >= thr)) > n_keep)
        def _():
            room = (n_keep - count(lambda k: k > thr)).astype(F32)
            lower_tri = jnp.where(_row_minus_col(tkc, tkc) >= 0, 1.0, 0.0).astype(BF16)

            def drop_excess(c, seen):
                key = key_ref[c]
                tie = key == thr
                tie_f = jnp.where(tie, 1.0, 0.0)
                rank = seen + _dot(lower_tri, tie_f.astype(BF16))
                key_ref[c] = jnp.where(tie & (rank > room), jnp.int32(INT_MIN), key)
                return seen + jnp.sum(tie_f, axis=0, keepdims=True)

            lax.fori_loop(0, nkc, drop_excess, jnp.zeros((1, tq), F32))

    thr = thr_ref[...]
    slopes = _alibi_slopes(C_HEADS).reshape(G, R)
    rel = _row_minus_col(tq, tkc).astype(F32)

    def score_pass(c, first):
        rows = pl.ds(pl.multiple_of(c * tkc, tkc), tkc)
        mask = jnp.transpose(jnp.where(key_ref[c] >= thr, 0.0, NEG))
        dist = rel + (qstart - c * tkc).astype(F32)
        for g in range(G):
            qs = jnp.concatenate(
                [q_ref[:, (g * R + r) * d:(g * R + r + 1) * d] for r in range(R)], axis=0)
            bias = jnp.concatenate([mask - float(slopes[g, r] / scale) * dist for r in range(R)], axis=0)
            s = _dot_nt(qs, k_ref[rows, g * d:(g + 1) * d]) + bias
            s_ref[g, c] = s
            mx_ref[g] = s if first else jnp.maximum(mx_ref[g], s)

    def acc_pass(c, first):
        rows = pl.ds(pl.multiple_of(c * tkc, tkc), tkc)
        for g in range(G):
            p = jnp.exp2((s_ref[g, c] - mx_ref[g]) * (scale * LOG2E))
            pv = _dot(p.astype(BF16), v_ref[rows, g * d:(g + 1) * d])
            if first:
                ls_ref[g] = p
                acc_ref[g] = pv
            else:
                ls_ref[g] += p
                acc_ref[g] += pv

    def sweep(fn):
        fn(0, True)

        def body(c, carry):
            fn(c, False)
            return carry

        lax.fori_loop(1, nkc, body, 0)

    sweep(score_pass)
    for g in range(G):
        _row_max_bcast(mx_ref.at[g])
    sweep(acc_pass)

    for g in range(G):
        o = acc_ref[g] / jnp.sum(ls_ref[g], -1, keepdims=True)
        for r in range(R):
            o_ref[:, (g * R + r) * d:(g * R + r + 1) * d] = o[r * tq:(r + 1) * tq].astype(o_ref.dtype)


def _dsa_attn(main, tail, B, S, tq=128, tkc=256):
    T = B * S
    nq = S // tq
    G, H, d = C_KV_HEADS, C_HEADS, C_HEAD_DIM
    R = H // G
    c_q, c_kv, c_iq = H * d, G * d, IDX_HEADS * IDX_DIM
    assert S % tq == 0 and S % tkc == 0 and tkc % tq == 0
    assert main.shape == (T, c_q + 2 * c_kv + c_iq) and tail.shape == (T, LANES)
    assert c_q % c_kv == 0 and (c_q + 2 * c_kv) % c_iq == 0
    n_keep = min(DSA_TOPK_MAX, S // 4)
    return pl.pallas_call(
        functools.partial(_dsa_kernel, n_keep=n_keep, tkc=tkc, scale=d ** -0.5),
        grid=(B, nq),
        in_specs=[
            pl.BlockSpec((tq, c_q), lambda b, i: (b * nq + i, 0)),
            pl.BlockSpec((S, c_kv), lambda b, i: (b, c_q // c_kv)),
            pl.BlockSpec((S, c_kv), lambda b, i: (b, c_q // c_kv + 1)),
            pl.BlockSpec((tq, c_iq), lambda b, i: (b * nq + i, (c_q + 2 * c_kv) // c_iq)),
            pl.BlockSpec((tq, LANES), lambda b, i: (b * nq + i, 0)),
            pl.BlockSpec((S, LANES), lambda b, i: (b, 0)),
        ],
        out_specs=pl.BlockSpec((tq, c_q), lambda b, i: (b * nq + i, 0)),
        out_shape=jax.ShapeDtypeStruct((T, c_q), BF16),
        scratch_shapes=[pltpu.VMEM((S, IDX_DIM), BF16), pltpu.VMEM((IDX_HEADS * tq, IDX_DIM), BF16),
                        pltpu.VMEM((S // tkc, tkc, tq), jnp.int32),
                        pltpu.VMEM((1, tq), jnp.int32),
                        pltpu.VMEM((G, S // tkc, R * tq, tkc), F32),
                        pltpu.VMEM((G, R * tq, tkc), F32), pltpu.VMEM((G, R * tq, tkc), F32),
                        pltpu.VMEM((G, R * tq, d), F32)],
        compiler_params=_params("parallel", "arbitrary"),
        name="dsa_attn",
    )(main, main, main, main, tail, tail)


def kernel(x, p, ffn1_w_in, ffn1_w_out, ffn2_w_in, ffn2_w_out, ln_g, ln_b, ple_w_gate, ple_w_proj,
           a_w_in, a_w_out, a_lam_q1, a_lam_k1, a_lam_q2, a_lam_k2, a_subln_g, b_w_in, b_w_out,
           c_w_in, c_w_out):
    B, S, D = x.shape
    depth = p.shape[0]
    T = B * S
    alpha = (2.0 * depth) ** 0.25
    bf = lambda w: w.astype(BF16)
    ple_g, ple_p = bf(ple_w_gate), bf(ple_w_proj)
    a_in, a_out, b_in, b_out, c_out = bf(a_w_in), bf(a_w_out), bf(b_w_in), bf(b_w_out), bf(c_w_out)
    n_main = C_HEADS * C_HEAD_DIM + 2 * C_KV_HEADS * C_HEAD_DIM + IDX_HEADS * IDX_DIM
    c_in = bf(c_w_in)
    c_tail = c_w_in[:, :, n_main:]
    c_tail = bf(jnp.pad(c_tail, ((0, 0), (0, 0), (0, LANES - c_tail.shape[2]))))
    lng = ln_g.reshape(-1, 1, D).astype(F32)
    lnb = ln_b.reshape(-1, 1, D).astype(F32)
    n_ln = ln_g.shape[1]
    p = p.reshape(depth, T, -1)
    x = x.reshape(T, D)
    for i in range(depth):
        m, j = i % N_MIXERS, i // N_MIXERS
        x = _ffn(x, ffn1_w_in, ffn1_w_out, i, lng, lnb, i * n_ln, alpha)
        if m == 0:
            d = A_HEAD_DIM
            H = D // (2 * d)
            lambda_init = 0.8 - 0.6 * math.exp(-0.3 * i)
            qkv = _proj(x, a_in, j, a_in.shape[2], BF16)
            lam_p = jnp.stack([a_lam_q1[j], a_lam_k1[j], a_lam_q2[j], a_lam_k2[j]]).astype(F32)
            o = _diff_attn(qkv, lam_p, a_subln_g[j].reshape(1, 2 * d).astype(F32), B, S, H, d, lambda_init)
            w_out = a_out
        elif m == 1:
            qkv = _proj(x, b_in, j, b_in.shape[2], BF16)
            o = _moba_attn(qkv, B, S, B_HEADS, B_HEAD_DIM, MOBA_BLOCK)
            w_out = b_out
        else:
            main = _proj(x, c_in, j, n_main, BF16)
            tail = _proj(x, c_tail, j, LANES, F32)
            o = _dsa_attn(main, tail, B, S)
            w_out = c_out
        x = _out_ln(x, o, w_out, j, lng, lnb, i * n_ln + 1, alpha)
        x = _ffn(x, ffn2_w_in, ffn2_w_out, i, lng, lnb, i * n_ln + 2, alpha)
        x = _ple(x, p, ple_g, ple_p, i, lng, lnb, i * n_ln + 3, alpha)
    return x.reshape(B, S, D)
```

```python
import functools
import math

import numpy as np
import jax
import jax.numpy as jnp
from jax import lax
from jax.experimental import pallas as pl
from jax.experimental.pallas import tpu as pltpu

F32 = jnp.float32
BF16 = jnp.bfloat16

N_MIXERS = 3
A_HEAD_DIM = 128
B_HEADS = 16
B_HEAD_DIM = 128
MOBA_BLOCK = 256
MOBA_TOPK = 3
C_HEADS = 16
C_KV_HEADS = 4
C_HEAD_DIM = 128
IDX_HEADS = 16
IDX_DIM = 64
DSA_TOPK_MAX = 256
LN_EPS = 1e-5
RMS_EPS = 1e-6

LANES = 128
SUBLANES = 8
NEG = -1e30
INT_MIN = -(2 ** 31)
LOG2E = math.log2(math.e)
VMEM_LIMIT = 60 * 1024 * 1024
ROW_SPLIT = 128


def _alibi_slopes(n):
    return 2.0 ** (-8.0 * np.arange(1, n + 1, dtype=np.float32) / n)


def _params(*sem):
    return pltpu.CompilerParams(dimension_semantics=sem, vmem_limit_bytes=VMEM_LIMIT)


def _dot(a, b):
    return jnp.dot(a, b, preferred_element_type=F32)


def _dot_nt(a, b):
    return lax.dot_general(a, b, (((1,), (1,)), ((), ())), preferred_element_type=F32)


def _layer_norm(z, g, b):
    mu = jnp.mean(z, -1, keepdims=True)
    zc = z - mu
    var = jnp.mean(zc * zc, -1, keepdims=True)
    return zc * lax.rsqrt(var + LN_EPS) * g + b


def _ffn_kernel(x_ref, wg_ref, wu_ref, wo_ref, g_ref, b_ref, o_ref, xb_ref, *, alpha):
    j = pl.program_id(1)

    @pl.when(j == 0)
    def _():
        xb_ref[...] = x_ref[...].astype(BF16)
        o_ref[...] = jnp.zeros_like(o_ref)

    xb = xb_ref[...]
    gate = _dot(xb, wg_ref[...].astype(BF16))
    up = _dot(xb, wu_ref[...].astype(BF16))
    h = (gate * jax.nn.sigmoid(gate) * up).astype(BF16)
    o_ref[...] += _dot(h, wo_ref[...].astype(BF16))

    @pl.when(j == pl.num_programs(1) - 1)
    def _():
        for r in range(0, o_ref.shape[0], ROW_SPLIT):
            rows = slice(r, r + ROW_SPLIT)
            z = alpha * x_ref[rows, :] + 0.5 * o_ref[rows, :]
            o_ref[rows, :] = _layer_norm(z, g_ref[...], b_ref[...])


def _ln_specs(ln, D, ngrid):
    if ngrid == 1:
        return pl.BlockSpec((None, 1, D), lambda i: (ln, 0, 0))
    return pl.BlockSpec((None, 1, D), lambda i, j: (ln, 0, 0))


def _ffn(x, w_in, w_out, layer, ln_g, ln_b, ln, alpha, tm=1024, tf=256):
    T, D = x.shape
    F = w_out.shape[1]
    nf = F // tf
    assert T % tm == 0 and F % tf == 0 and w_in.shape[1:] == (D, 2 * F)
    return pl.pallas_call(
        functools.partial(_ffn_kernel, alpha=alpha),
        grid=(T // tm, nf),
        in_specs=[
            pl.BlockSpec((tm, D), lambda i, j: (i, 0)),
            pl.BlockSpec((None, D, tf), lambda i, j: (layer, 0, j)),
            pl.BlockSpec((None, D, tf), lambda i, j: (layer, 0, j + nf)),
            pl.BlockSpec((None, tf, D), lambda i, j: (layer, j, 0)),
            _ln_specs(ln, D, 2),
            _ln_specs(ln, D, 2),
        ],
        out_specs=pl.BlockSpec((tm, D), lambda i, j: (i, 0)),
        out_shape=jax.ShapeDtypeStruct((T, D), F32),
        scratch_shapes=[pltpu.VMEM((tm, D), BF16)],
        compiler_params=_params("parallel", "arbitrary"),
        name="ffn",
    )(x, w_in, w_in, w_out, ln_g, ln_b)


def _proj_kernel(x_ref, w_ref, o_ref, xb_ref):
    @pl.when(pl.program_id(1) == 0)
    def _():
        xb_ref[...] = x_ref[...].astype(BF16)

    o_ref[...] = _dot(xb_ref[...], w_ref[...]).astype(o_ref.dtype)


def _proj(x, w, layer, n_cols, out_dtype, tm=1024, tn=1024):
    T, D = x.shape
    tm, tn = min(tm, T), min(tn, n_cols)
    assert T % tm == 0 and n_cols % tn == 0 and w.shape[1] == D and w.shape[2] >= n_cols
    return pl.pallas_call(
        _proj_kernel,
        grid=(T // tm, n_cols // tn),
        in_specs=[
            pl.BlockSpec((tm, D), lambda i, j: (i, 0)),
            pl.BlockSpec((None, D, tn), lambda i, j: (layer, 0, j)),
        ],
        out_specs=pl.BlockSpec((tm, tn), lambda i, j: (i, j)),
        out_shape=jax.ShapeDtypeStruct((T, n_cols), out_dtype),
        scratch_shapes=[pltpu.VMEM((tm, D), BF16)],
        compiler_params=_params("parallel", "arbitrary"),
        name="proj",
    )(x, w)


def _out_ln_kernel(x_ref, o_ref, w_ref, g_ref, b_ref, y_ref, *, alpha):
    for r in range(0, x_ref.shape[0], ROW_SPLIT):
        rows = slice(r, r + ROW_SPLIT)
        z = alpha * x_ref[rows, :] + _dot(o_ref[rows, :], w_ref[...])
        y_ref[rows, :] = _layer_norm(z, g_ref[...], b_ref[...])


def _out_ln(x, o, w, layer, ln_g, ln_b, ln, alpha, tm=1024):
    T, D = x.shape
    K = o.shape[1]
    assert T % tm == 0 and tm % ROW_SPLIT == 0 and w.shape[1:] == (K, D)
    return pl.pallas_call(
        functools.partial(_out_ln_kernel, alpha=alpha),
        grid=(T // tm,),
        in_specs=[
            pl.BlockSpec((tm, D), lambda i: (i, 0)),
            pl.BlockSpec((tm, K), lambda i: (i, 0)),
            pl.BlockSpec((None, K, D), lambda i: (layer, 0, 0), pipeline_mode=pl.Buffered(1)),
            _ln_specs(ln, D, 1),
            _ln_specs(ln, D, 1),
        ],
        out_specs=pl.BlockSpec((tm, D), lambda i: (i, 0)),
        out_shape=jax.ShapeDtypeStruct((T, D), F32),
        compiler_params=_params("parallel"),
        name="out_ln",
    )(x, o, w, ln_g, ln_b)


def _ple_kernel(x_ref, p_ref, wg_ref, wp_ref, g_ref, b_ref, y_ref, *, alpha):
    for r in range(0, x_ref.shape[0], ROW_SPLIT):
        rows = slice(r, r + ROW_SPLIT)
        x = x_ref[rows, :]
        gate = jax.nn.sigmoid(_dot(x.astype(BF16), wg_ref[...]))
        emb = _dot(p_ref[rows, :].astype(BF16), wp_ref[...])
        y_ref[rows, :] = _layer_norm(alpha * x + gate * emb, g_ref[...], b_ref[...])


def _ple(x, p, wg, wp, layer, ln_g, ln_b, ln, alpha, tm=1024):
    T, D = x.shape
    P = p.shape[2]
    assert T % tm == 0 and tm % ROW_SPLIT == 0
    return pl.pallas_call(
        functools.partial(_ple_kernel, alpha=alpha),
        grid=(T // tm,),
        in_specs=[
            pl.BlockSpec((tm, D), lambda i: (i, 0)),
            pl.BlockSpec((None, tm, P), lambda i: (layer, i, 0)),
            pl.BlockSpec((None, D, D), lambda i: (layer, 0, 0), pipeline_mode=pl.Buffered(1)),
            pl.BlockSpec((None, P, D), lambda i: (layer, 0, 0), pipeline_mode=pl.Buffered(1)),
            _ln_specs(ln, D, 1),
            _ln_specs(ln, D, 1),
        ],
        out_specs=pl.BlockSpec((tm, D), lambda i: (i, 0)),
        out_shape=jax.ShapeDtypeStruct((T, D), F32),
        compiler_params=_params("parallel"),
        name="ple",
    )(x, p, wg, wp, ln_g, ln_b)


def _row_max_bcast(mx_ref):
    m = jnp.max(mx_ref[...], -1, keepdims=True)
    mx_ref[...] = jnp.broadcast_to(m, mx_ref.shape)


def _causal_spans(qi):
    spans = [(qi - 1, 2, True)] if qi % 2 else [(qi, 1, True)]
    return spans + [(2 * j, 2, False) for j in range(qi // 2)]


def _row_minus_col(rows, cols):
    r = lax.broadcasted_iota(jnp.int32, (rows, cols), 0)
    c = lax.broadcasted_iota(jnp.int32, (rows, cols), 1)
    return r - c


DIFF_HEADS_PER_STEP = 2


def _diff_attn_kernel(slopes_ref, lam_ref, subg_ref, q_ref, k_ref, v_ref, o_ref,
                      s_ref, bias_ref, mx_ref, ls_ref, acc_ref, *, d, tq, scale, lambda_init):
    hp = DIFF_HEADS_PER_STEP
    h0 = pl.program_id(1) * hp
    nq = q_ref.shape[0] // tq
    units = range(2 * hp)
    slopes = [slopes_ref[h0 + hh] * (1.0 / scale) for hh in range(hp)]

    rel = _row_minus_col(tq, 2 * tq).astype(F32)
    for hh in range(hp):
        bias_ref[hh] = -slopes[hh] * rel
    lam_p = lam_ref[...]
    lam = (jnp.exp(jnp.sum(lam_p[0:1] * lam_p[1:2], -1, keepdims=True))
           - jnp.exp(jnp.sum(lam_p[2:3] * lam_p[3:4], -1, keepdims=True)) + lambda_init)

    for qi in range(nq):
        par = qi % 2
        q = [q_ref[qi * tq:(qi + 1) * tq, u * d:(u + 1) * d] for u in units]
        spans = _causal_spans(qi)

        for c0, width, diagonal in spans:
            for hh in range(hp):
                bias = bias_ref[hh, :, :width * tq] - slopes[hh] * float((qi - c0) * tq)
                for u in (2 * hh, 2 * hh + 1):
                    s = _dot_nt(q[u], k_ref[c0 * tq:(c0 + width) * tq, u * d:(u + 1) * d]) + bias
                    if diagonal:
                        s = jnp.where(_row_minus_col(tq, width * tq) + (qi - c0) * tq >= 0, s, NEG)
                    chunks = [s[:, w * tq:(w + 1) * tq] for w in range(width)]
                    for w in range(width):
                        s_ref[par, u, c0 + w] = chunks[w]
                    mx = functools.reduce(jnp.maximum, chunks)
                    mx_ref[par, u] = mx if diagonal else jnp.maximum(mx_ref[par, u], mx)

        for u in units:
            _row_max_bcast(mx_ref.at[par, u])

        for c0, width, diagonal in spans:
            for u in units:
                hh = u // 2
                ps = [jnp.exp2((s_ref[par, u, c0 + w] - mx_ref[par, u]) * (scale * LOG2E))
                      for w in range(width)]
                psum = functools.reduce(lambda a, b: a + b, ps)
                p = ps[0] if width == 1 else jnp.concatenate(ps, axis=1)
                pv = _dot(p.astype(BF16), v_ref[c0 * tq:(c0 + width) * tq, hh * 2 * d:(hh + 1) * 2 * d])
                if diagonal:
                    ls_ref[par, u] = psum
                    acc_ref[par, u] = pv
                else:
                    ls_ref[par, u] += psum
                    acc_ref[par, u] += pv

        for hh in range(hp):
            l0 = jnp.sum(ls_ref[par, 2 * hh], -1, keepdims=True)
            l1 = jnp.sum(ls_ref[par, 2 * hh + 1], -1, keepdims=True)
            o = acc_ref[par, 2 * hh] / l0 - lam * (acc_ref[par, 2 * hh + 1] / l1)
            o = o * lax.rsqrt(jnp.mean(o * o, -1, keepdims=True) + RMS_EPS) * subg_ref[...] * (1.0 - lambda_init)
            o_ref[qi * tq:(qi + 1) * tq, hh * 2 * d:(hh + 1) * 2 * d] = o.astype(o_ref.dtype)


def _diff_attn(qkv, lam_p, subln_g, B, S, H, d, lambda_init, tq=256):
    T = B * S
    nq = S // tq
    hp = DIFF_HEADS_PER_STEP
    w = hp * 2 * d
    assert S % tq == 0 and H % hp == 0 and qkv.shape == (T, 3 * H * 2 * d)
    slopes = jnp.asarray(_alibi_slopes(H), F32)
    return pl.pallas_call(
        functools.partial(_diff_attn_kernel, d=d, tq=tq, scale=d ** -0.5, lambda_init=lambda_init),
        grid=(B, H // hp),
        in_specs=[
            pl.BlockSpec(memory_space=pltpu.SMEM),
            pl.BlockSpec((4, d), lambda b, h: (0, 0)),
            pl.BlockSpec((1, 2 * d), lambda b, h: (0, 0)),
            pl.BlockSpec((S, w), lambda b, h: (b, h)),
            pl.BlockSpec((S, w), lambda b, h: (b, H // hp + h)),
            pl.BlockSpec((S, w), lambda b, h: (b, 2 * (H // hp) + h)),
        ],
        out_specs=pl.BlockSpec((S, w), lambda b, h: (b, h)),
        out_shape=jax.ShapeDtypeStruct((T, H * 2 * d), BF16),
        scratch_shapes=[pltpu.VMEM((2, 2 * hp, nq, tq, tq), F32), pltpu.VMEM((hp, tq, 2 * tq), F32),
                        pltpu.VMEM((2, 2 * hp, tq, tq), F32), pltpu.VMEM((2, 2 * hp, tq, tq), F32),
                        pltpu.VMEM((2, 2 * hp, tq, 2 * d), F32)],
        compiler_params=_params("parallel", "parallel"),
        name="diff_attn",
    )(slopes, lam_p, subln_g, qkv, qkv, qkv)


MOBA_HEADS_PER_STEP = 4


def _moba_kernel(slopes_ref, q_ref, k_ref, v_ref, o_ref, km_ref, mb_ref, s_ref, bias_ref,
                 mx_ref, ls_ref, acc_ref, *, nb, n_sel, d, bs, scale):
    hp = MOBA_HEADS_PER_STEP
    h0 = pl.program_id(1) * hp
    nbp = -(-nb // SUBLANES) * SUBLANES
    heads = range(hp)
    cols = [slice(hh * d, (hh + 1) * d) for hh in heads]
    slopes = [slopes_ref[h0 + hh] * (1.0 / scale) for hh in heads]

    km_ref[...] = jnp.zeros_like(km_ref)
    rel_i = _row_minus_col(bs, bs)
    for hh in heads:
        for n in range(nb):
            km_ref[hh, n:n + 1, :] = jnp.mean(
                k_ref[n * bs:(n + 1) * bs, cols[hh]].astype(F32), 0, keepdims=True)
        bias_ref[hh] = -slopes[hh] * rel_i.astype(F32)
    block = lax.broadcasted_iota(jnp.int32, (nbp, bs), 0)

    for own in range(nb):
        par = own % 2
        q = [q_ref[own * bs:(own + 1) * bs, cols[hh]] for hh in heads]
        for hh in heads if own > 0 else ():
            km = km_ref[hh]
            km_hi = km.astype(BF16)
            km_lo = (km - km_hi.astype(F32)).astype(BF16)
            gate = (_dot_nt(km_hi, q[hh]) + _dot_nt(km_lo, q[hh]))[0:nbp, :]
            rank = jnp.zeros((nbp, bs), jnp.int32)
            for m in range(own):
                gm = gate[m:m + 1, :]
                beats = (gm > gate) | ((gm == gate) & (block > m))
                rank = rank + beats.astype(jnp.int32)
            sel = (rank < n_sel) & (block < own)
            off = -slopes[hh] * ((own - block) * bs).astype(F32)
            mb_t = jnp.where(sel, off, NEG)
            mb = jnp.transpose(jnp.concatenate([mb_t, jnp.zeros((LANES - nbp, bs), F32)], axis=0))
            for n in range(own):
                mb_ref[par, hh, n] = jnp.broadcast_to(mb[:, n:n + 1], (bs, LANES))

        spans = _causal_spans(own)

        for c0, width, diagonal in spans:
            for hh in heads:
                qk = _dot_nt(q[hh], k_ref[c0 * bs:(c0 + width) * bs, cols[hh]])
                chunks = []
                for w in range(width):
                    sw = qk[:, w * bs:(w + 1) * bs] + bias_ref[hh]
                    if diagonal and w == width - 1:
                        sw = jnp.where(rel_i >= 0, sw, NEG)
                    else:
                        sw = sw + jnp.concatenate([mb_ref[par, hh, c0 + w]] * (bs // LANES), axis=1)
                    s_ref[par, hh, c0 + w] = sw
                    chunks.append(sw)
                mx = functools.reduce(jnp.maximum, chunks)
                mx_ref[par, hh] = mx if diagonal else jnp.maximum(mx_ref[par, hh], mx)

        for hh in heads:
            _row_max_bcast(mx_ref.at[par, hh])

        for c0, width, diagonal in spans:
            for hh in heads:
                ps = [jnp.exp2((s_ref[par, hh, c0 + w] - mx_ref[par, hh]) * (scale * LOG2E))
                      for w in range(width)]
                psum = functools.reduce(lambda a, b: a + b, ps)
                p = ps[0] if width == 1 else jnp.concatenate(ps, axis=1)
                pv = _dot(p.astype(BF16), v_ref[c0 * bs:(c0 + width) * bs, cols[hh]])
                if diagonal:
                    ls_ref[par, hh] = psum
                    acc_ref[par, hh] = pv
                else:
                    ls_ref[par, hh] += psum
                    acc_ref[par, hh] += pv

        for hh in heads:
            l = jnp.sum(ls_ref[par, hh], -1, keepdims=True)
            o_ref[own * bs:(own + 1) * bs, cols[hh]] = (acc_ref[par, hh] / l).astype(o_ref.dtype)


def _moba_attn(qkv, B, S, H, d, bs):
    T = B * S
    nb = S // bs
    hp = MOBA_HEADS_PER_STEP
    assert S % bs == 0 and nb <= LANES and H % hp == 0 and qkv.shape == (T, 3 * H * d)
    n_sel = min(MOBA_TOPK, nb - 1)
    slopes = jnp.asarray(_alibi_slopes(H), F32)
    return pl.pallas_call(
        functools.partial(_moba_kernel, nb=nb, n_sel=n_sel, d=d, bs=bs, scale=d ** -0.5),
        grid=(B, H // hp),
        in_specs=[
            pl.BlockSpec(memory_space=pltpu.SMEM),
            pl.BlockSpec((S, hp * d), lambda b, h: (b, h)),
            pl.BlockSpec((S, hp * d), lambda b, h: (b, H // hp + h)),
            pl.BlockSpec((S, hp * d), lambda b, h: (b, 2 * (H // hp) + h)),
        ],
        out_specs=pl.BlockSpec((S, hp * d), lambda b, h: (b, h)),
        out_shape=jax.ShapeDtypeStruct((T, H * d), BF16),
        scratch_shapes=[pltpu.VMEM((hp, LANES, d), F32), pltpu.VMEM((2, hp, nb, bs, LANES), F32),
                        pltpu.VMEM((2, hp, nb, bs, bs), F32), pltpu.VMEM((hp, bs, bs), F32),
                        pltpu.VMEM((2, hp, bs, bs), F32), pltpu.VMEM((2, hp, bs, bs), F32),
                        pltpu.VMEM((2, hp, bs, d), F32)],
        compiler_params=_params("parallel", "parallel"),
        name="moba_attn",
    )(slopes, qkv, qkv, qkv)


def _dsa_kernel(q_ref, k_ref, v_ref, qi_ref, tailq_ref, tailk_ref, o_ref,
                kib_ref, qis_ref, key_ref, thr_ref, s_ref, mx_ref, ls_ref, acc_ref, *, n_keep, tkc, scale):
    i = pl.program_id(1)
    tq = q_ref.shape[0]
    G, R, d = C_KV_HEADS, C_HEADS // C_KV_HEADS, C_HEAD_DIM
    qstart = i * tq
    nkc = (qstart + tq + tkc - 1) // tkc

    @pl.when(i == 0)
    def _():
        kib_ref[...] = tailk_ref[:, 0:IDX_DIM].astype(BF16)

    w_t = (jnp.transpose(tailq_ref[...])[IDX_DIM:IDX_DIM + IDX_HEADS, :]
           * (IDX_HEADS ** -0.5 * IDX_DIM ** -0.5))
    key_minus_query = _row_minus_col(tkc, tq)

    for hh in range(IDX_HEADS):
        qis_ref[hh * tq:(hh + 1) * tq, :] = qi_ref[:, hh * IDX_DIM:(hh + 1) * IDX_DIM]

    def score_chunk(c, carry):
        kc = kib_ref[pl.ds(pl.multiple_of(c * tkc, tkc), tkc), :]
        dots = _dot_nt(kc, qis_ref[...])
        acc = jnp.zeros((tkc, tq), F32)
        for hh in range(IDX_HEADS):
            acc = acc + w_t[hh:hh + 1, :] * jnp.maximum(dots[:, hh * tq:(hh + 1) * tq], 0.0)
        bits = lax.bitcast_convert_type(acc, jnp.int32)
        key = bits ^ ((bits >> 31) & jnp.int32(0x7FFFFFFF))
        causal = key_minus_query + (c * tkc - qstart) <= 0
        key_ref[c] = jnp.where(causal, key, jnp.int32(INT_MIN))
        return carry

    lax.fori_loop(0, nkc, score_chunk, 0)

    def count(pred):
        def body(c, part):
            hit = pred(key_ref[c]).astype(jnp.int32)
            return part + jnp.sum(hit.reshape(tkc // SUBLANES, SUBLANES, tq), axis=0)
        part = lax.fori_loop(0, nkc, body, jnp.zeros((SUBLANES, tq), jnp.int32))
        return jnp.sum(part, axis=0, keepdims=True)

    thr_ref[...] = jnp.full_like(thr_ref, INT_MIN + 1)

    @pl.when(qstart + tq > n_keep)
    def _():
        zero = jnp.zeros((1, tq), jnp.int32)
        thr0 = jnp.where(count(lambda k: k >= zero) >= n_keep, jnp.int32(0), jnp.int32(INT_MIN))

        def bit_step(b, thr):
            cand = thr | jnp.left_shift(jnp.int32(1), 30 - b)
            return jnp.where(count(lambda k: k >= cand) >= n_keep, cand, thr)

        thr = jnp.maximum(lax.fori_loop(0, 31, bit_step, thr0), jnp.int32(INT_MIN + 1))
        thr_ref[...] = thr

        @pl.when(jnp.max(count(lambda k: k >= thr)) > n_keep)
        def _():
            room = (n_keep - count(lambda k: k > thr)).astype(F32)
            lower_tri = jnp.where(_row_minus_col(tkc, tkc) >= 0, 1.0, 0.0).astype(BF16)

            def drop_excess(c, seen):
                key = key_ref[c]
                tie = key == thr
                tie_f = jnp.where(tie, 1.0, 0.0)
                rank = seen + _dot(lower_tri, tie_f.astype(BF16))
                key_ref[c] = jnp.where(tie & (rank > room), jnp.int32(INT_MIN), key)
                return seen + jnp.sum(tie_f, axis=0, keepdims=True)

            lax.fori_loop(0, nkc, drop_excess, jnp.zeros((1, tq), F32))

    thr = thr_ref[...]
    slopes = _alibi_slopes(C_HEADS).reshape(G, R)
    rel = _row_minus_col(tq, tkc).astype(F32)

    def score_pass(c, first):
        rows = pl.ds(pl.multiple_of(c * tkc, tkc), tkc)
        mask = jnp.transpose(jnp.where(key_ref[c] >= thr, 0.0, NEG))
        dist = rel + (qstart - c * tkc).astype(F32)
        for g in range(G):
            qs = jnp.concatenate(
                [q_ref[:, (g * R + r) * d:(g * R + r + 1) * d] for r in range(R)], axis=0)
            bias = jnp.concatenate([mask - float(slopes[g, r] / scale) * dist for r in range(R)], axis=0)
            s = _dot_nt(qs, k_ref[rows, g * d:(g + 1) * d]) + bias
            s_ref[g, c] = s
            mx_ref[g] = s if first else jnp.maximum(mx_ref[g], s)

    def acc_pass(c, first):
        rows = pl.ds(pl.multiple_of(c * tkc, tkc), tkc)
        for g in range(G):
            p = jnp.exp2((s_ref[g, c] - mx_ref[g]) * (scale * LOG2E))
            pv = _dot(p.astype(BF16), v_ref[rows, g * d:(g + 1) * d])
            if first:
                ls_ref[g] = p
                acc_ref[g] = pv
            else:
                ls_ref[g] += p
                acc_ref[g] += pv

    def sweep(fn):
        fn(0, True)

        def body(c, carry):
            fn(c, False)
            return carry

        lax.fori_loop(1, nkc, body, 0)

    sweep(score_pass)
    for g in range(G):
        _row_max_bcast(mx_ref.at[g])
    sweep(acc_pass)

    for g in range(G):
        o = acc_ref[g] / jnp.sum(ls_ref[g], -1, keepdims=True)
        for r in range(R):
            o_ref[:, (g * R + r) * d:(g * R + r + 1) * d] = o[r * tq:(r + 1) * tq].astype(o_ref.dtype)


def _dsa_attn(main, tail, B, S, tq=128, tkc=256):
    T = B * S
    nq = S // tq
    G, H, d = C_KV_HEADS, C_HEADS, C_HEAD_DIM
    R = H // G
    c_q, c_kv, c_iq = H * d, G * d, IDX_HEADS * IDX_DIM
    assert S % tq == 0 and S % tkc == 0 and tkc % tq == 0
    assert main.shape == (T, c_q + 2 * c_kv + c_iq) and tail.shape == (T, LANES)
    assert c_q % c_kv == 0 and (c_q + 2 * c_kv) % c_iq == 0
    n_keep = min(DSA_TOPK_MAX, S // 4)
    return pl.pallas_call(
        functools.partial(_dsa_kernel, n_keep=n_keep, tkc=tkc, scale=d ** -0.5),
        grid=(B, nq),
        in_specs=[
            pl.BlockSpec((tq, c_q), lambda b, i: (b * nq + i, 0)),
            pl.BlockSpec((S, c_kv), lambda b, i: (b, c_q // c_kv)),
            pl.BlockSpec((S, c_kv), lambda b, i: (b, c_q // c_kv + 1)),
            pl.BlockSpec((tq, c_iq), lambda b, i: (b * nq + i, (c_q + 2 * c_kv) // c_iq)),
            pl.BlockSpec((tq, LANES), lambda b, i: (b * nq + i, 0)),
            pl.BlockSpec((S, LANES), lambda b, i: (b, 0)),
        ],
        out_specs=pl.BlockSpec((tq, c_q), lambda b, i: (b * nq + i, 0)),
        out_shape=jax.ShapeDtypeStruct((T, c_q), BF16),
        scratch_shapes=[pltpu.VMEM((S, IDX_DIM), BF16), pltpu.VMEM((IDX_HEADS * tq, IDX_DIM), BF16),
                        pltpu.VMEM((S // tkc, tkc, tq), jnp.int32),
                        pltpu.VMEM((1, tq), jnp.int32),
                        pltpu.VMEM((G, S // tkc, R * tq, tkc), F32),
                        pltpu.VMEM((G, R * tq, tkc), F32), pltpu.VMEM((G, R * tq, tkc), F32),
                        pltpu.VMEM((G, R * tq, d), F32)],
        compiler_params=_params("parallel", "arbitrary"),
        name="dsa_attn",
    )(main, main, main, main, tail, tail)


def kernel(x, p, ffn1_w_in, ffn1_w_out, ffn2_w_in, ffn2_w_out, ln_g, ln_b, ple_w_gate, ple_w_proj,
           a_w_in, a_w_out, a_lam_q1, a_lam_k1, a_lam_q2, a_lam_k2, a_subln_g, b_w_in, b_w_out,
           c_w_in, c_w_out):
    B, S, D = x.shape
    depth = p.shape[0]
    T = B * S
    alpha = (2.0 * depth) ** 0.25
    bf = lambda w: w.astype(BF16)
    ple_g, ple_p = bf(ple_w_gate), bf(ple_w_proj)
    a_in, a_out, b_in, b_out, c_out = bf(a_w_in), bf(a_w_out), bf(b_w_in), bf(b_w_out), bf(c_w_out)
    n_main = C_HEADS * C_HEAD_DIM + 2 * C_KV_HEADS * C_HEAD_DIM + IDX_HEADS * IDX_DIM
    c_in = bf(c_w_in)
    c_tail = c_w_in[:, :, n_main:]
    c_tail = bf(jnp.pad(c_tail, ((0, 0), (0, 0), (0, LANES - c_tail.shape[2]))))
    lng = ln_g.reshape(-1, 1, D).astype(F32)
    lnb = ln_b.reshape(-1, 1, D).astype(F32)
    n_ln = ln_g.shape[1]
    p = p.reshape(depth, T, -1)
    x = x.reshape(T, D)
    for i in range(depth):
        m, j = i % N_MIXERS, i // N_MIXERS
        x = _ffn(x, ffn1_w_in, ffn1_w_out, i, lng, lnb, i * n_ln, alpha)
        if m == 0:
            d = A_HEAD_DIM
            H = D // (2 * d)
            lambda_init = 0.8 - 0.6 * math.exp(-0.3 * i)
            qkv = _proj(x, a_in, j, a_in.shape[2], BF16)
            lam_p = jnp.stack([a_lam_q1[j], a_lam_k1[j], a_lam_q2[j], a_lam_k2[j]]).astype(F32)
            o = _diff_attn(qkv, lam_p, a_subln_g[j].reshape(1, 2 * d).astype(F32), B, S, H, d, lambda_init)
            w_out = a_out
        elif m == 1:
            qkv = _proj(x, b_in, j, b_in.shape[2], BF16)
            o = _moba_attn(qkv, B, S, B_HEADS, B_HEAD_DIM, MOBA_BLOCK)
            w_out = b_out
        else:
            main = _proj(x, c_in, j, n_main, BF16)
            tail = _proj(x, c_tail, j, LANES, F32)
            o = _dsa_attn(main, tail, B, S)
            w_out = c_out
        x = _out_ln(x, o, w_out, j, lng, lnb, i * n_ln + 1, alpha)
        x = _ffn(x, ffn2_w_in, ffn2_w_out, i, lng, lnb, i * n_ln + 2, alpha)
        x = _ple(x, p, ple_g, ple_p, i, lng, lnb, i * n_ln + 3, alpha)
    return x.reshape(B, S, D)
```

```python
import functools
import math

import numpy as np
import jax
import jax.numpy as jnp
from jax import lax
from jax.experimental import pallas as pl
from jax.experimental.pallas import tpu as pltpu

F32 = jnp.float32
BF16 = jnp.bfloat16

N_MIXERS = 3
A_HEAD_DIM = 128
B_HEADS = 16
B_HEAD_DIM = 128
MOBA_BLOCK = 256
MOBA_TOPK = 3
C_HEADS = 16
C_KV_HEADS = 4
C_HEAD_DIM = 128
IDX_HEADS = 16
IDX_DIM = 64
DSA_TOPK_MAX = 256
LN_EPS = 1e-5
RMS_EPS = 1e-6

LANES = 128
SUBLANES = 8
NEG = -1e30
INT_MIN = -(2 ** 31)
LOG2E = math.log2(math.e)
VMEM_LIMIT = 60 * 1024 * 1024
ROW_SPLIT = 128


def _alibi_slopes(n):
    return 2.0 ** (-8.0 * np.arange(1, n + 1, dtype=np.float32) / n)


def _params(*sem):
    return pltpu.CompilerParams(dimension_semantics=sem, vmem_limit_bytes=VMEM_LIMIT)


def _dot(a, b):
    return jnp.dot(a, b, preferred_element_type=F32)


def _dot_nt(a, b):
    return lax.dot_general(a, b, (((1,), (1,)), ((), ())), preferred_element_type=F32)


def _layer_norm(z, g, b):
    mu = jnp.mean(z, -1, keepdims=True)
    zc = z - mu
    var = jnp.mean(zc * zc, -1, keepdims=True)
    return zc * lax.rsqrt(var + LN_EPS) * g + b


def _ffn_kernel(x_ref, wg_ref, wu_ref, wo_ref, g_ref, b_ref, o_ref, xb_ref, *, alpha):
    j = pl.program_id(1)

    @pl.when(j == 0)
    def _():
        xb_ref[...] = x_ref[...].astype(BF16)
        o_ref[...] = jnp.zeros_like(o_ref)

    xb = xb_ref[...]
    gate = _dot(xb, wg_ref[...].astype(BF16))
    up = _dot(xb, wu_ref[...].astype(BF16))
    h = (gate * jax.nn.sigmoid(gate) * up).astype(BF16)
    o_ref[...] += _dot(h, wo_ref[...].astype(BF16))

    @pl.when(j == pl.num_programs(1) - 1)
    def _():
        for r in range(0, o_ref.shape[0], ROW_SPLIT):
            rows = slice(r, r + ROW_SPLIT)
            z = alpha * x_ref[rows, :] + 0.5 * o_ref[rows, :]
            o_ref[rows, :] = _layer_norm(z, g_ref[...], b_ref[...])


def _ln_specs(ln, D, ngrid):
    if ngrid == 1:
        return pl.BlockSpec((None, 1, D), lambda i: (ln, 0, 0))
    return pl.BlockSpec((None, 1, D), lambda i, j: (ln, 0, 0))


def _ffn(x, w_in, w_out, layer, ln_g, ln_b, ln, alpha, tm=1024, tf=256):
    T, D = x.shape
    F = w_out.shape[1]
    nf = F // tf
    assert T % tm == 0 and F % tf == 0 and w_in.shape[1:] == (D, 2 * F)
    return pl.pallas_call(
        functools.partial(_ffn_kernel, alpha=alpha),
        grid=(T // tm, nf),
        in_specs=[
            pl.BlockSpec((tm, D), lambda i, j: (i, 0)),
            pl.BlockSpec((None, D, tf), lambda i, j: (layer, 0, j)),
            pl.BlockSpec((None, D, tf), lambda i, j: (layer, 0, j + nf)),
            pl.BlockSpec((None, tf, D), lambda i, j: (layer, j, 0)),
            _ln_specs(ln, D, 2),
            _ln_specs(ln, D, 2),
        ],
        out_specs=pl.BlockSpec((tm, D), lambda i, j: (i, 0)),
        out_shape=jax.ShapeDtypeStruct((T, D), F32),
        scratch_shapes=[pltpu.VMEM((tm, D), BF16)],
        compiler_params=_params("parallel", "arbitrary"),
        name="ffn",
    )(x, w_in, w_in, w_out, ln_g, ln_b)


def _proj_kernel(x_ref, w_ref, o_ref, xb_ref):
    @pl.when(pl.program_id(1) == 0)
    def _():
        xb_ref[...] = x_ref[...].astype(BF16)

    o_ref[...] = _dot(xb_ref[...], w_ref[...].astype(BF16)).astype(o_ref.dtype)


def _proj(x, w, layer, n_cols, out_dtype, tm=1024, tn=1024):
    T, D = x.shape
    tm, tn = min(tm, T), min(tn, n_cols)
    assert T % tm == 0 and n_cols % tn == 0 and w.shape[1] == D and w.shape[2] >= n_cols
    return pl.pallas_call(
        _proj_kernel,
        grid=(T // tm, n_cols // tn),
        in_specs=[
            pl.BlockSpec((tm, D), lambda i, j: (i, 0)),
            pl.BlockSpec((None, D, tn), lambda i, j: (layer, 0, j)),
        ],
        out_specs=pl.BlockSpec((tm, tn), lambda i, j: (i, j)),
        out_shape=jax.ShapeDtypeStruct((T, n_cols), out_dtype),
        scratch_shapes=[pltpu.VMEM((tm, D), BF16)],
        compiler_params=_params("parallel", "arbitrary"),
        name="proj",
    )(x, w)


def _out_ln_kernel(x_ref, o_ref, w_ref, g_ref, b_ref, y_ref, *, alpha):
    for r in range(0, x_ref.shape[0], ROW_SPLIT):
        rows = slice(r, r + ROW_SPLIT)
        z = alpha * x_ref[rows, :] + _dot(o_ref[rows, :], w_ref[...])
        y_ref[rows, :] = _layer_norm(z, g_ref[...], b_ref[...])


def _out_ln(x, o, w, layer, ln_g, ln_b, ln, alpha, tm=512):
    T, D = x.shape
    K = o.shape[1]
    assert T % tm == 0 and tm % ROW_SPLIT == 0 and w.shape[1:] == (K, D)
    return pl.pallas_call(
        functools.partial(_out_ln_kernel, alpha=alpha),
        grid=(T // tm,),
        in_specs=[
            pl.BlockSpec((tm, D), lambda i: (i, 0)),
            pl.BlockSpec((tm, K), lambda i: (i, 0)),
            pl.BlockSpec((None, K, D), lambda i: (layer, 0, 0)),
            _ln_specs(ln, D, 1),
            _ln_specs(ln, D, 1),
        ],
        out_specs=pl.BlockSpec((tm, D), lambda i: (i, 0)),
        out_shape=jax.ShapeDtypeStruct((T, D), F32),
        compiler_params=_params("parallel"),
        name="out_ln",
    )(x, o, w, ln_g, ln_b)


def _ple_kernel(x_ref, p_ref, wg_ref, wp_ref, g_ref, b_ref, y_ref, *, alpha):
    for r in range(0, x_ref.shape[0], ROW_SPLIT):
        rows = slice(r, r + ROW_SPLIT)
        x = x_ref[rows, :]
        gate = jax.nn.sigmoid(_dot(x.astype(BF16), wg_ref[...]))
        emb = _dot(p_ref[rows, :].astype(BF16), wp_ref[...])
        y_ref[rows, :] = _layer_norm(alpha * x + gate * emb, g_ref[...], b_ref[...])


def _ple(x, p, wg, wp, layer, ln_g, ln_b, ln, alpha, tm=512):
    T, D = x.shape
    P = p.shape[2]
    assert T % tm == 0 and tm % ROW_SPLIT == 0
    return pl.pallas_call(
        functools.partial(_ple_kernel, alpha=alpha),
        grid=(T // tm,),
        in_specs=[
            pl.BlockSpec((tm, D), lambda i: (i, 0)),
            pl.BlockSpec((None, tm, P), lambda i: (layer, i, 0)),
            pl.BlockSpec((None, D, D), lambda i: (layer, 0, 0)),
            pl.BlockSpec((None, P, D), lambda i: (layer, 0, 0)),
            _ln_specs(ln, D, 1),
            _ln_specs(ln, D, 1),
        ],
        out_specs=pl.BlockSpec((tm, D), lambda i: (i, 0)),
        out_shape=jax.ShapeDtypeStruct((T, D), F32),
        compiler_params=_params("parallel"),
        name="ple",
    )(x, p, wg, wp, ln_g, ln_b)


def _row_max_bcast(mx_ref):
    m = jnp.max(mx_ref[...], -1, keepdims=True)
    mx_ref[...] = jnp.broadcast_to(m, mx_ref.shape)


def _causal_spans(qi):
    spans = [(qi - 1, 2, True)] if qi % 2 else [(qi, 1, True)]
    return spans + [(2 * j, 2, False) for j in range(qi // 2)]


def _row_minus_col(rows, cols):
    r = lax.broadcasted_iota(jnp.int32, (rows, cols), 0)
    c = lax.broadcasted_iota(jnp.int32, (rows, cols), 1)
    return r - c


DIFF_HEADS_PER_STEP = 2


def _diff_attn_kernel(slopes_ref, lam_ref, subg_ref, q_ref, k_ref, v_ref, o_ref,
                      s_ref, bias_ref, mx_ref, ls_ref, acc_ref, *, d, tq, scale, lambda_init):
    hp = DIFF_HEADS_PER_STEP
    h0 = pl.program_id(1) * hp
    nq = q_ref.shape[0] // tq
    units = range(2 * hp)
    slopes = [slopes_ref[h0 + hh] * (1.0 / scale) for hh in range(hp)]

    rel = _row_minus_col(tq, 2 * tq).astype(F32)
    for hh in range(hp):
        bias_ref[hh] = -slopes[hh] * rel
    lam_p = lam_ref[...]
    lam = (jnp.exp(jnp.sum(lam_p[0:1] * lam_p[1:2], -1, keepdims=True))
           - jnp.exp(jnp.sum(lam_p[2:3] * lam_p[3:4], -1, keepdims=True)) + lambda_init)

    for qi in range(nq):
        par = qi % 2
        q = [q_ref[qi * tq:(qi + 1) * tq, u * d:(u + 1) * d] for u in units]
        spans = _causal_spans(qi)

        for c0, width, diagonal in spans:
            for hh in range(hp):
                bias = bias_ref[hh, :, :width * tq] - slopes[hh] * float((qi - c0) * tq)
                for u in (2 * hh, 2 * hh + 1):
                    s = _dot_nt(q[u], k_ref[c0 * tq:(c0 + width) * tq, u * d:(u + 1) * d]) + bias
                    if diagonal:
                        s = jnp.where(_row_minus_col(tq, width * tq) + (qi - c0) * tq >= 0, s, NEG)
                    chunks = [s[:, w * tq:(w + 1) * tq] for w in range(width)]
                    for w in range(width):
                        s_ref[par, u, c0 + w] = chunks[w]
                    mx = functools.reduce(jnp.maximum, chunks)
                    mx_ref[par, u] = mx if diagonal else jnp.maximum(mx_ref[par, u], mx)

        for u in units:
            _row_max_bcast(mx_ref.at[par, u])

        for c0, width, diagonal in spans:
            for u in units:
                hh = u // 2
                ps = [jnp.exp2((s_ref[par, u, c0 + w] - mx_ref[par, u]) * (scale * LOG2E))
                      for w in range(width)]
                psum = functools.reduce(lambda a, b: a + b, ps)
                p = ps[0] if width == 1 else jnp.concatenate(ps, axis=1)
                pv = _dot(p.astype(BF16), v_ref[c0 * tq:(c0 + width) * tq, hh * 2 * d:(hh + 1) * 2 * d])
                if diagonal:
                    ls_ref[par, u] = psum
                    acc_ref[par, u] = pv
                else:
                    ls_ref[par, u] += psum
                    acc_ref[par, u] += pv

        for hh in range(hp):
            l0 = jnp.sum(ls_ref[par, 2 * hh], -1, keepdims=True)
            l1 = jnp.sum(ls_ref[par, 2 * hh + 1], -1, keepdims=True)
            o = acc_ref[par, 2 * hh] / l0 - lam * (acc_ref[par, 2 * hh + 1] / l1)
            o = o * lax.rsqrt(jnp.mean(o * o, -1, keepdims=True) + RMS_EPS) * subg_ref[...] * (1.0 - lambda_init)
            o_ref[qi * tq:(qi + 1) * tq, hh * 2 * d:(hh + 1) * 2 * d] = o.astype(o_ref.dtype)


def _diff_attn(qkv, lam_p, subln_g, B, S, H, d, lambda_init, tq=256):
    T = B * S
    nq = S // tq
    hp = DIFF_HEADS_PER_STEP
    w = hp * 2 * d
    assert S % tq == 0 and H % hp == 0 and qkv.shape == (T, 3 * H * 2 * d)
    slopes = jnp.asarray(_alibi_slopes(H), F32)
    return pl.pallas_call(
        functools.partial(_diff_attn_kernel, d=d, tq=tq, scale=d ** -0.5, lambda_init=lambda_init),
        grid=(B, H // hp),
        in_specs=[
            pl.BlockSpec(memory_space=pltpu.SMEM),
            pl.BlockSpec((4, d), lambda b, h: (0, 0)),
            pl.BlockSpec((1, 2 * d), lambda b, h: (0, 0)),
            pl.BlockSpec((S, w), lambda b, h: (b, h)),
            pl.BlockSpec((S, w), lambda b, h: (b, H // hp + h)),
            pl.BlockSpec((S, w), lambda b, h: (b, 2 * (H // hp) + h)),
        ],
        out_specs=pl.BlockSpec((S, w), lambda b, h: (b, h)),
        out_shape=jax.ShapeDtypeStruct((T, H * 2 * d), BF16),
        scratch_shapes=[pltpu.VMEM((2, 2 * hp, nq, tq, tq), F32), pltpu.VMEM((hp, tq, 2 * tq), F32),
                        pltpu.VMEM((2, 2 * hp, tq, tq), F32), pltpu.VMEM((2, 2 * hp, tq, tq), F32),
                        pltpu.VMEM((2, 2 * hp, tq, 2 * d), F32)],
        compiler_params=_params("parallel", "parallel"),
        name="diff_attn",
    )(slopes, lam_p, subln_g, qkv, qkv, qkv)


MOBA_HEADS_PER_STEP = 4


def _moba_kernel(slopes_ref, q_ref, k_ref, v_ref, o_ref, km_ref, mb_ref, s_ref, bias_ref,
                 mx_ref, ls_ref, acc_ref, *, nb, n_sel, d, bs, scale):
    hp = MOBA_HEADS_PER_STEP
    h0 = pl.program_id(1) * hp
    nbp = -(-nb // SUBLANES) * SUBLANES
    heads = range(hp)
    cols = [slice(hh * d, (hh + 1) * d) for hh in heads]
    slopes = [slopes_ref[h0 + hh] * (1.0 / scale) for hh in heads]

    km_ref[...] = jnp.zeros_like(km_ref)
    rel_i = _row_minus_col(bs, bs)
    for hh in heads:
        for n in range(nb):
            km_ref[hh, n:n + 1, :] = jnp.mean(
                k_ref[n * bs:(n + 1) * bs, cols[hh]].astype(F32), 0, keepdims=True)
        bias_ref[hh] = -slopes[hh] * rel_i.astype(F32)
    block = lax.broadcasted_iota(jnp.int32, (nbp, bs), 0)

    for own in range(nb):
        par = own % 2
        q = [q_ref[own * bs:(own + 1) * bs, cols[hh]] for hh in heads]
        for hh in heads if own > 0 else ():
            km = km_ref[hh]
            km_hi = km.astype(BF16)
            km_lo = (km - km_hi.astype(F32)).astype(BF16)
            gate = (_dot_nt(km_hi, q[hh]) + _dot_nt(km_lo, q[hh]))[0:nbp, :]
            rank = jnp.zeros((nbp, bs), jnp.int32)
            for m in range(own):
                gm = gate[m:m + 1, :]
                beats = (gm > gate) | ((gm == gate) & (block > m))
                rank = rank + beats.astype(jnp.int32)
            sel = (rank < n_sel) & (block < own)
            off = -slopes[hh] * ((own - block) * bs).astype(F32)
            mb_t = jnp.where(sel, off, NEG)
            mb = jnp.transpose(jnp.concatenate([mb_t, jnp.zeros((LANES - nbp, bs), F32)], axis=0))
            for n in range(own):
                mb_ref[par, hh, n] = jnp.broadcast_to(mb[:, n:n + 1], (bs, LANES))

        spans = _causal_spans(own)

        for c0, width, diagonal in spans:
            for hh in heads:
                qk = _dot_nt(q[hh], k_ref[c0 * bs:(c0 + width) * bs, cols[hh]])
                chunks = []
                for w in range(width):
                    sw = qk[:, w * bs:(w + 1) * bs] + bias_ref[hh]
                    if diagonal and w == width - 1:
                        sw = jnp.where(rel_i >= 0, sw, NEG)
                    else:
                        sw = sw + jnp.concatenate([mb_ref[par, hh, c0 + w]] * (bs // LANES), axis=1)
                    s_ref[par, hh, c0 + w] = sw
                    chunks.append(sw)
                mx = functools.reduce(jnp.maximum, chunks)
                mx_ref[par, hh] = mx if diagonal else jnp.maximum(mx_ref[par, hh], mx)

        for hh in heads:
            _row_max_bcast(mx_ref.at[par, hh])

        for c0, width, diagonal in spans:
            for hh in heads:
                ps = [jnp.exp2((s_ref[par, hh, c0 + w] - mx_ref[par, hh]) * (scale * LOG2E))
                      for w in range(width)]
                psum = functools.reduce(lambda a, b: a + b, ps)
                p = ps[0] if width == 1 else jnp.concatenate(ps, axis=1)
                pv = _dot(p.astype(BF16), v_ref[c0 * bs:(c0 + width) * bs, cols[hh]])
                if diagonal:
                    ls_ref[par, hh] = psum
                    acc_ref[par, hh] = pv
                else:
                    ls_ref[par, hh] += psum
                    acc_ref[par, hh] += pv

        for hh in heads:
            l = jnp.sum(ls_ref[par, hh], -1, keepdims=True)
            o_ref[own * bs:(own + 1) * bs, cols[hh]] = (acc_ref[par, hh] / l).astype(o_ref.dtype)


def _moba_attn(qkv, B, S, H, d, bs):
    T = B * S
    nb = S // bs
    hp = MOBA_HEADS_PER_STEP
    assert S % bs == 0 and nb <= LANES and H % hp == 0 and qkv.shape == (T, 3 * H * d)
    n_sel = min(MOBA_TOPK, nb - 1)
    slopes = jnp.asarray(_alibi_slopes(H), F32)
    return pl.pallas_call(
        functools.partial(_moba_kernel, nb=nb, n_sel=n_sel, d=d, bs=bs, scale=d ** -0.5),
        grid=(B, H // hp),
        in_specs=[
            pl.BlockSpec(memory_space=pltpu.SMEM),
            pl.BlockSpec((S, hp * d), lambda b, h: (b, h)),
            pl.BlockSpec((S, hp * d), lambda b, h: (b, H // hp + h)),
            pl.BlockSpec((S, hp * d), lambda b, h: (b, 2 * (H // hp) + h)),
        ],
        out_specs=pl.BlockSpec((S, hp * d), lambda b, h: (b, h)),
        out_shape=jax.ShapeDtypeStruct((T, H * d), BF16),
        scratch_shapes=[pltpu.VMEM((hp, LANES, d), F32), pltpu.VMEM((2, hp, nb, bs, LANES), F32),
                        pltpu.VMEM((2, hp, nb, bs, bs), F32), pltpu.VMEM((hp, bs, bs), F32),
                        pltpu.VMEM((2, hp, bs, bs), F32), pltpu.VMEM((2, hp, bs, bs), F32),
                        pltpu.VMEM((2, hp, bs, d), F32)],
        compiler_params=_params("parallel", "parallel"),
        name="moba_attn",
    )(slopes, qkv, qkv, qkv)


def _dsa_kernel(q_ref, k_ref, v_ref, qi_ref, tailq_ref, tailk_ref, o_ref,
                kib_ref, qis_ref, key_ref, thr_ref, s_ref, mx_ref, ls_ref, acc_ref, *, n_keep, tkc, scale):
    i = pl.program_id(1)
    tq = q_ref.shape[0]
    G, R, d = C_KV_HEADS, C_HEADS // C_KV_HEADS, C_HEAD_DIM
    qstart = i * tq
    nkc = (qstart + tq + tkc - 1) // tkc

    @pl.when(i == 0)
    def _():
        kib_ref[...] = tailk_ref[:, 0:IDX_DIM].astype(BF16)

    w_t = (jnp.transpose(tailq_ref[...])[IDX_DIM:IDX_DIM + IDX_HEADS, :]
           * (IDX_HEADS ** -0.5 * IDX_DIM ** -0.5))
    key_minus_query = _row_minus_col(tkc, tq)

    for hh in range(IDX_HEADS):
        qis_ref[hh * tq:(hh + 1) * tq, :] = qi_ref[:, hh * IDX_DIM:(hh + 1) * IDX_DIM]

    def score_chunk(c, carry):
        kc = kib_ref[pl.ds(pl.multiple_of(c * tkc, tkc), tkc), :]
        dots = _dot_nt(kc, qis_ref[...])
        acc = jnp.zeros((tkc, tq), F32)
        for hh in range(IDX_HEADS):
            acc = acc + w_t[hh:hh + 1, :] * jnp.maximum(dots[:, hh * tq:(hh + 1) * tq], 0.0)
        bits = lax.bitcast_convert_type(acc, jnp.int32)
        key = bits ^ ((bits >> 31) & jnp.int32(0x7FFFFFFF))
        causal = key_minus_query + (c * tkc - qstart) <= 0
        key_ref[c] = jnp.where(causal, key, jnp.int32(INT_MIN))
        return carry

    lax.fori_loop(0, nkc, score_chunk, 0)

    def count(pred):
        def body(c, part):
            hit = pred(key_ref[c]).astype(jnp.int32)
            return part + jnp.sum(hit.reshape(tkc // SUBLANES, SUBLANES, tq), axis=0)
        part = lax.fori_loop(0, nkc, body, jnp.zeros((SUBLANES, tq), jnp.int32))
        return jnp.sum(part, axis=0, keepdims=True)

    thr_ref[...] = jnp.full_like(thr_ref, INT_MIN + 1)

    @pl.when(qstart + tq > n_keep)
    def _():
        zero = jnp.zeros((1, tq), jnp.int32)
        thr0 = jnp.where(count(lambda k: k >= zero) >= n_keep, jnp.int32(0), jnp.int32(INT_MIN))

        def bit_step(b, thr):
            cand = thr | jnp.left_shift(jnp.int32(1), 30 - b)
            return jnp.where(count(lambda k: k >= cand) >= n_keep, cand, thr)

        thr = jnp.maximum(lax.fori_loop(0, 31, bit_step, thr0), jnp.int32(INT_MIN + 1))
        thr_ref[...] = thr

        @pl.when(jnp.max(count(lambda k: k >= thr)) > n_keep)
        def _():
            room = (n_keep - count(lambda k: k > thr)).astype(F32)
            lower_tri = jnp.where(_row_minus_col(tkc, tkc) >= 0, 1.0, 0.0).astype(BF16)

            def drop_excess(c, seen):
                key = key_ref[c]
                tie = key == thr
                tie_f = jnp.where(tie, 1.0, 0.0)
                rank = seen + _dot(lower_tri, tie_f.astype(BF16))
                key_ref[c] = jnp.where(tie & (rank > room), jnp.int32(INT_MIN), key)
                return seen + jnp.sum(tie_f, axis=0, keepdims=True)

            lax.fori_loop(0, nkc, drop_excess, jnp.zeros((1, tq), F32))

    thr = thr_ref[...]
    slopes = _alibi_slopes(C_HEADS).reshape(G, R)
    rel = _row_minus_col(tq, tkc).astype(F32)

    def score_pass(c, first):
        rows = pl.ds(pl.multiple_of(c * tkc, tkc), tkc)
        mask = jnp.transpose(jnp.where(key_ref[c] >= thr, 0.0, NEG))
        dist = rel + (qstart - c * tkc).astype(F32)
        for g in range(G):
            qs = jnp.concatenate(
                [q_ref[:, (g * R + r) * d:(g * R + r + 1) * d] for r in range(R)], axis=0)
            bias = jnp.concatenate([mask - float(slopes[g, r] / scale) * dist for r in range(R)], axis=0)
            s = _dot_nt(qs, k_ref[rows, g * d:(g + 1) * d]) + bias
            s_ref[g, c] = s
            mx_ref[g] = s if first else jnp.maximum(mx_ref[g], s)

    def acc_pass(c, first):
        rows = pl.ds(pl.multiple_of(c * tkc, tkc), tkc)
        for g in range(G):
            p = jnp.exp2((s_ref[g, c] - mx_ref[g]) * (scale * LOG2E))
            pv = _dot(p.astype(BF16), v_ref[rows, g * d:(g + 1) * d])
            if first:
                ls_ref[g] = p
                acc_ref[g] = pv
            else:
                ls_ref[g] += p
                acc_ref[g] += pv

    def sweep(fn):
        fn(0, True)

        def body(c, carry):
            fn(c, False)
            return carry

        lax.fori_loop(1, nkc, body, 0)

    sweep(score_pass)
    for g in range(G):
        _row_max_bcast(mx_ref.at[g])
    sweep(acc_pass)

    for g in range(G):
        o = acc_ref[g] / jnp.sum(ls_ref[g], -1, keepdims=True)
        for r in range(R):
            o_ref[:, (g * R + r) * d:(g * R + r + 1) * d] = o[r * tq:(r + 1) * tq].astype(o_ref.dtype)


def _dsa_attn(main, tail, B, S, tq=128, tkc=256):
    T = B * S
    nq = S // tq
    G, H, d = C_KV_HEADS, C_HEADS, C_HEAD_DIM
    R = H // G
    c_q, c_kv, c_iq = H * d, G * d, IDX_HEADS * IDX_DIM
    assert S % tq == 0 and S % tkc == 0 and tkc % tq == 0
    assert main.shape == (T, c_q + 2 * c_kv + c_iq) and tail.shape == (T, LANES)
    assert c_q % c_kv == 0 and (c_q + 2 * c_kv) % c_iq == 0
    n_keep = min(DSA_TOPK_MAX, S // 4)
    return pl.pallas_call(
        functools.partial(_dsa_kernel, n_keep=n_keep, tkc=tkc, scale=d ** -0.5),
        grid=(B, nq),
        in_specs=[
            pl.BlockSpec((tq, c_q), lambda b, i: (b * nq + i, 0)),
            pl.BlockSpec((S, c_kv), lambda b, i: (b, c_q // c_kv)),
            pl.BlockSpec((S, c_kv), lambda b, i: (b, c_q // c_kv + 1)),
            pl.BlockSpec((tq, c_iq), lambda b, i: (b * nq + i, (c_q + 2 * c_kv) // c_iq)),
            pl.BlockSpec((tq, LANES), lambda b, i: (b * nq + i, 0)),
            pl.BlockSpec((S, LANES), lambda b, i: (b, 0)),
        ],
        out_specs=pl.BlockSpec((tq, c_q), lambda b, i: (b * nq + i, 0)),
        out_shape=jax.ShapeDtypeStruct((T, c_q), BF16),
        scratch_shapes=[pltpu.VMEM((S, IDX_DIM), BF16), pltpu.VMEM((IDX_HEADS * tq, IDX_DIM), BF16),
                        pltpu.VMEM((S // tkc, tkc, tq), jnp.int32),
                        pltpu.VMEM((1, tq), jnp.int32),
                        pltpu.VMEM((G, S // tkc, R * tq, tkc), F32),
                        pltpu.VMEM((G, R * tq, tkc), F32), pltpu.VMEM((G, R * tq, tkc), F32),
                        pltpu.VMEM((G, R * tq, d), F32)],
        compiler_params=_params("parallel", "arbitrary"),
        name="dsa_attn",
    )(main, main, main, main, tail, tail)


def kernel(x, p, ffn1_w_in, ffn1_w_out, ffn2_w_in, ffn2_w_out, ln_g, ln_b, ple_w_gate, ple_w_proj,
           a_w_in, a_w_out, a_lam_q1, a_lam_k1, a_lam_q2, a_lam_k2, a_subln_g, b_w_in, b_w_out,
           c_w_in, c_w_out):
    B, S, D = x.shape
    depth = p.shape[0]
    T = B * S
    alpha = (2.0 * depth) ** 0.25
    bf = lambda w: w.astype(BF16)
    ple_g, ple_p = bf(ple_w_gate), bf(ple_w_proj)
    a_in, b_in = a_w_in, b_w_in
    a_out, b_out, c_out = bf(a_w_out), bf(b_w_out), bf(c_w_out)
    n_main = C_HEADS * C_HEAD_DIM + 2 * C_KV_HEADS * C_HEAD_DIM + IDX_HEADS * IDX_DIM
    c_in = c_w_in
    c_tail = c_w_in[:, :, n_main:]
    c_tail = bf(jnp.pad(c_tail, ((0, 0), (0, 0), (0, LANES - c_tail.shape[2]))))
    lng = ln_g.reshape(-1, 1, D).astype(F32)
    lnb = ln_b.reshape(-1, 1, D).astype(F32)
    n_ln = ln_g.shape[1]
    p = p.reshape(depth, T, -1)
    x = x.reshape(T, D)
    for i in range(depth):
        m, j = i % N_MIXERS, i // N_MIXERS
        x = _ffn(x, ffn1_w_in, ffn1_w_out, i, lng, lnb, i * n_ln, alpha)
        if m == 0:
            d = A_HEAD_DIM
            H = D // (2 * d)
            lambda_init = 0.8 - 0.6 * math.exp(-0.3 * i)
            qkv = _proj(x, a_in, j, a_in.shape[2], BF16)
            lam_p = jnp.stack([a_lam_q1[j], a_lam_k1[j], a_lam_q2[j], a_lam_k2[j]]).astype(F32)
            o = _diff_attn(qkv, lam_p, a_subln_g[j].reshape(1, 2 * d).astype(F32), B, S, H, d, lambda_init)
            w_out = a_out
        elif m == 1:
            qkv = _proj(x, b_in, j, b_in.shape[2], BF16)
            o = _moba_attn(qkv, B, S, B_HEADS, B_HEAD_DIM, MOBA_BLOCK)
            w_out = b_out
        else:
            main = _proj(x, c_in, j, n_main, BF16)
            tail = _proj(x, c_tail, j, LANES, F32)
            o = _dsa_attn(main, tail, B, S)
            w_out = c_out
        x = _out_ln(x, o, w_out, j, lng, lnb, i * n_ln + 1, alpha)
        x = _ffn(x, ffn2_w_in, ffn2_w_out, i, lng, lnb, i * n_ln + 2, alpha)
        x = _ple(x, p, ple_g, ple_p, i, lng, lnb, i * n_ln + 3, alpha)
    return x.reshape(B, S, D)
```

```python
import functools
import math

import numpy as np
import jax
import jax.numpy as jnp
from jax import lax
from jax.experimental import pallas as pl
from jax.experimental.pallas import tpu as pltpu

F32 = jnp.float32
BF16 = jnp.bfloat16

N_MIXERS = 3
A_HEAD_DIM = 128
B_HEADS = 16
B_HEAD_DIM = 128
MOBA_BLOCK = 256
MOBA_TOPK = 3
C_HEADS = 16
C_KV_HEADS = 4
C_HEAD_DIM = 128
IDX_HEADS = 16
IDX_DIM = 64
DSA_TOPK_MAX = 256
LN_EPS = 1e-5
RMS_EPS = 1e-6

LANES = 128
SUBLANES = 8
NEG = -1e30
INT_MIN = -(2 ** 31)
LOG2E = math.log2(math.e)
VMEM_LIMIT = 60 * 1024 * 1024
ROW_SPLIT = 128


def _alibi_slopes(n):
    return 2.0 ** (-8.0 * np.arange(1, n + 1, dtype=np.float32) / n)


def _params(*sem):
    return pltpu.CompilerParams(dimension_semantics=sem, vmem_limit_bytes=VMEM_LIMIT)


def _dot(a, b):
    return jnp.dot(a, b, preferred_element_type=F32)


def _dot_nt(a, b):
    return lax.dot_general(a, b, (((1,), (1,)), ((), ())), preferred_element_type=F32)


def _layer_norm(z, g, b):
    mu = jnp.mean(z, -1, keepdims=True)
    zc = z - mu
    var = jnp.mean(zc * zc, -1, keepdims=True)
    return zc * lax.rsqrt(var + LN_EPS) * g + b


def _ffn_kernel(x_ref, wg_ref, wu_ref, wo_ref, g_ref, b_ref, o_ref, xb_ref, *, alpha):
    j = pl.program_id(1)

    @pl.when(j == 0)
    def _():
        xb_ref[...] = x_ref[...].astype(BF16)
        o_ref[...] = jnp.zeros_like(o_ref)

    xb = xb_ref[...]
    gate = _dot(xb, wg_ref[...].astype(BF16))
    up = _dot(xb, wu_ref[...].astype(BF16))
    h = (gate * jax.nn.sigmoid(gate) * up).astype(BF16)
    o_ref[...] += _dot(h, wo_ref[...].astype(BF16))

    @pl.when(j == pl.num_programs(1) - 1)
    def _():
        for r in range(0, o_ref.shape[0], ROW_SPLIT):
            rows = slice(r, r + ROW_SPLIT)
            z = alpha * x_ref[rows, :] + 0.5 * o_ref[rows, :]
            o_ref[rows, :] = _layer_norm(z, g_ref[...], b_ref[...])


def _ln_specs(ln, D, ngrid):
    if ngrid == 1:
        return pl.BlockSpec((None, 1, D), lambda i: (ln, 0, 0))
    return pl.BlockSpec((None, 1, D), lambda i, j: (ln, 0, 0))


def _ffn(x, w_in, w_out, layer, ln_g, ln_b, ln, alpha, tm=1024, tf=256):
    T, D = x.shape
    F = w_out.shape[1]
    nf = F // tf
    assert T % tm == 0 and F % tf == 0 and w_in.shape[1:] == (D, 2 * F)
    return pl.pallas_call(
        functools.partial(_ffn_kernel, alpha=alpha),
        grid=(T // tm, nf),
        in_specs=[
            pl.BlockSpec((tm, D), lambda i, j: (i, 0)),
            pl.BlockSpec((None, D, tf), lambda i, j: (layer, 0, j)),
            pl.BlockSpec((None, D, tf), lambda i, j: (layer, 0, j + nf)),
            pl.BlockSpec((None, tf, D), lambda i, j: (layer, j, 0)),
            _ln_specs(ln, D, 2),
            _ln_specs(ln, D, 2),
        ],
        out_specs=pl.BlockSpec((tm, D), lambda i, j: (i, 0)),
        out_shape=jax.ShapeDtypeStruct((T, D), F32),
        scratch_shapes=[pltpu.VMEM((tm, D), BF16)],
        compiler_params=_params("parallel", "arbitrary"),
        name="ffn",
    )(x, w_in, w_in, w_out, ln_g, ln_b)


def _proj_kernel(x_ref, w_ref, o_ref, xb_ref):
    @pl.when(pl.program_id(1) == 0)
    def _():
        xb_ref[...] = x_ref[...].astype(BF16)

    o_ref[...] = _dot(xb_ref[...], w_ref[...]).astype(o_ref.dtype)


def _proj(x, w, layer, n_cols, out_dtype, tm=1024, tn=1024):
    T, D = x.shape
    tm, tn = min(tm, T), min(tn, n_cols)
    assert T % tm == 0 and n_cols % tn == 0 and w.shape[1] == D and w.shape[2] >= n_cols
    return pl.pallas_call(
        _proj_kernel,
        grid=(T // tm, n_cols // tn),
        in_specs=[
            pl.BlockSpec((tm, D), lambda i, j: (i, 0)),
            pl.BlockSpec((None, D, tn), lambda i, j: (layer, 0, j)),
        ],
        out_specs=pl.BlockSpec((tm, tn), lambda i, j: (i, j)),
        out_shape=jax.ShapeDtypeStruct((T, n_cols), out_dtype),
        scratch_shapes=[pltpu.VMEM((tm, D), BF16)],
        compiler_params=_params("parallel", "arbitrary"),
        name="proj",
    )(x, w)


def _out_ln_kernel(x_ref, o_ref, w_ref, g_ref, b_ref, y_ref, *, alpha):
    for r in range(0, x_ref.shape[0], ROW_SPLIT):
        rows = slice(r, r + ROW_SPLIT)
        z = alpha * x_ref[rows, :] + _dot(o_ref[rows, :], w_ref[...])
        y_ref[rows, :] = _layer_norm(z, g_ref[...], b_ref[...])


def _out_ln(x, o, w, layer, ln_g, ln_b, ln, alpha, tm=512):
    T, D = x.shape
    K = o.shape[1]
    assert T % tm == 0 and tm % ROW_SPLIT == 0 and w.shape[1:] == (K, D)
    return pl.pallas_call(
        functools.partial(_out_ln_kernel, alpha=alpha),
        grid=(T // tm,),
        in_specs=[
            pl.BlockSpec((tm, D), lambda i: (i, 0)),
            pl.BlockSpec((tm, K), lambda i: (i, 0)),
            pl.BlockSpec((None, K, D), lambda i: (layer, 0, 0)),
            _ln_specs(ln, D, 1),
            _ln_specs(ln, D, 1),
        ],
        out_specs=pl.BlockSpec((tm, D), lambda i: (i, 0)),
        out_shape=jax.ShapeDtypeStruct((T, D), F32),
        compiler_params=_params("parallel"),
        name="out_ln",
    )(x, o, w, ln_g, ln_b)


def _ple_kernel(x_ref, p_ref, wg_ref, wp_ref, g_ref, b_ref, y_ref, *, alpha):
    for r in range(0, x_ref.shape[0], ROW_SPLIT):
        rows = slice(r, r + ROW_SPLIT)
        x = x_ref[rows, :]
        gate = jax.nn.sigmoid(_dot(x.astype(BF16), wg_ref[...]))
        emb = _dot(p_ref[rows, :].astype(BF16), wp_ref[...])
        y_ref[rows, :] = _layer_norm(alpha * x + gate * emb, g_ref[...], b_ref[...])


def _ple(x, p, wg, wp, layer, ln_g, ln_b, ln, alpha, tm=512):
    T, D = x.shape
    P = p.shape[2]
    assert T % tm == 0 and tm % ROW_SPLIT == 0
    return pl.pallas_call(
        functools.partial(_ple_kernel, alpha=alpha),
        grid=(T // tm,),
        in_specs=[
            pl.BlockSpec((tm, D), lambda i: (i, 0)),
            pl.BlockSpec((None, tm, P), lambda i: (layer, i, 0)),
            pl.BlockSpec((None, D, D), lambda i: (layer, 0, 0)),
            pl.BlockSpec((None, P, D), lambda i: (layer, 0, 0)),
            _ln_specs(ln, D, 1),
            _ln_specs(ln, D, 1),
        ],
        out_specs=pl.BlockSpec((tm, D), lambda i: (i, 0)),
        out_shape=jax.ShapeDtypeStruct((T, D), F32),
        compiler_params=_params("parallel"),
        name="ple",
    )(x, p, wg, wp, ln_g, ln_b)


def _row_max_bcast(mx_ref):
    m = jnp.max(mx_ref[...], -1, keepdims=True)
    mx_ref[...] = jnp.broadcast_to(m, mx_ref.shape)


def _causal_spans(qi):
    spans = [(qi - 1, 2, True)] if qi % 2 else [(qi, 1, True)]
    return spans + [(2 * j, 2, False) for j in range(qi // 2)]


def _row_minus_col(rows, cols):
    r = lax.broadcasted_iota(jnp.int32, (rows, cols), 0)
    c = lax.broadcasted_iota(jnp.int32, (rows, cols), 1)
    return r - c


DIFF_HEADS_PER_STEP = 2


def _diff_attn_kernel(slopes_ref, lam_ref, subg_ref, q_ref, k_ref, v_ref, o_ref,
                      s_ref, bias_ref, mx_ref, ls_ref, acc_ref, *, d, tq, scale, lambda_init):
    hp = DIFF_HEADS_PER_STEP
    h0 = pl.program_id(1) * hp
    nq = q_ref.shape[0] // tq
    units = range(2 * hp)
    slopes = [slopes_ref[h0 + hh] * (1.0 / scale) for hh in range(hp)]

    rel = _row_minus_col(tq, 2 * tq).astype(F32)
    for hh in range(hp):
        bias_ref[hh] = -slopes[hh] * rel
    lam_p = lam_ref[...]
    lam = (jnp.exp(jnp.sum(lam_p[0:1] * lam_p[1:2], -1, keepdims=True))
           - jnp.exp(jnp.sum(lam_p[2:3] * lam_p[3:4], -1, keepdims=True)) + lambda_init)

    for qi in range(nq):
        par = qi % 2
        q = [q_ref[qi * tq:(qi + 1) * tq, u * d:(u + 1) * d] for u in units]
        spans = _causal_spans(qi)

        for c0, width, diagonal in spans:
            for hh in range(hp):
                bias = bias_ref[hh, :, :width * tq] - slopes[hh] * float((qi - c0) * tq)
                for u in (2 * hh, 2 * hh + 1):
                    s = _dot_nt(q[u], k_ref[c0 * tq:(c0 + width) * tq, u * d:(u + 1) * d]) + bias
                    if diagonal:
                        s = jnp.where(_row_minus_col(tq, width * tq) + (qi - c0) * tq >= 0, s, NEG)
                    chunks = [s[:, w * tq:(w + 1) * tq] for w in range(width)]
                    for w in range(width):
                        s_ref[par, u, c0 + w] = chunks[w]
                    mx = functools.reduce(jnp.maximum, chunks)
                    mx_ref[par, u] = mx if diagonal else jnp.maximum(mx_ref[par, u], mx)

        for u in units:
            _row_max_bcast(mx_ref.at[par, u])

        for c0, width, diagonal in spans:
            for u in units:
                hh = u // 2
                ps = [jnp.exp2((s_ref[par, u, c0 + w] - mx_ref[par, u]) * (scale * LOG2E))
                      for w in range(width)]
                psum = functools.reduce(lambda a, b: a + b, ps)
                p = ps[0] if width == 1 else jnp.concatenate(ps, axis=1)
                pv = _dot(p.astype(BF16), v_ref[c0 * tq:(c0 + width) * tq, hh * 2 * d:(hh + 1) * 2 * d])
                if diagonal:
                    ls_ref[par, u] = psum
                    acc_ref[par, u] = pv
                else:
                    ls_ref[par, u] += psum
                    acc_ref[par, u] += pv

        for hh in range(hp):
            l0 = jnp.sum(ls_ref[par, 2 * hh], -1, keepdims=True)
            l1 = jnp.sum(ls_ref[par, 2 * hh + 1], -1, keepdims=True)
            o = acc_ref[par, 2 * hh] / l0 - lam * (acc_ref[par, 2 * hh + 1] / l1)
            o = o * lax.rsqrt(jnp.mean(o * o, -1, keepdims=True) + RMS_EPS) * subg_ref[...] * (1.0 - lambda_init)
            o_ref[qi * tq:(qi + 1) * tq, hh * 2 * d:(hh + 1) * 2 * d] = o.astype(o_ref.dtype)


def _diff_attn(qkv, lam_p, subln_g, B, S, H, d, lambda_init, tq=256):
    T = B * S
    nq = S // tq
    hp = DIFF_HEADS_PER_STEP
    w = hp * 2 * d
    assert S % tq == 0 and H % hp == 0 and qkv.shape == (T, 3 * H * 2 * d)
    slopes = jnp.asarray(_alibi_slopes(H), F32)
    return pl.pallas_call(
        functools.partial(_diff_attn_kernel, d=d, tq=tq, scale=d ** -0.5, lambda_init=lambda_init),
        grid=(B, H // hp),
        in_specs=[
            pl.BlockSpec(memory_space=pltpu.SMEM),
            pl.BlockSpec((4, d), lambda b, h: (0, 0)),
            pl.BlockSpec((1, 2 * d), lambda b, h: (0, 0)),
            pl.BlockSpec((S, w), lambda b, h: (b, h)),
            pl.BlockSpec((S, w), lambda b, h: (b, H // hp + h)),
            pl.BlockSpec((S, w), lambda b, h: (b, 2 * (H // hp) + h)),
        ],
        out_specs=pl.BlockSpec((S, w), lambda b, h: (b, h)),
        out_shape=jax.ShapeDtypeStruct((T, H * 2 * d), BF16),
        scratch_shapes=[pltpu.VMEM((2, 2 * hp, nq, tq, tq), F32), pltpu.VMEM((hp, tq, 2 * tq), F32),
                        pltpu.VMEM((2, 2 * hp, tq, tq), F32), pltpu.VMEM((2, 2 * hp, tq, tq), F32),
                        pltpu.VMEM((2, 2 * hp, tq, 2 * d), F32)],
        compiler_params=_params("parallel", "parallel"),
        name="diff_attn",
    )(slopes, lam_p, subln_g, qkv, qkv, qkv)


MOBA_HEADS_PER_STEP = 4


def _moba_kernel(slopes_ref, q_ref, k_ref, v_ref, o_ref, km_ref, mb_ref, s_ref, bias_ref,
                 mx_ref, ls_ref, acc_ref, *, nb, n_sel, d, bs, scale):
    hp = MOBA_HEADS_PER_STEP
    h0 = pl.program_id(1) * hp
    nbp = -(-nb // SUBLANES) * SUBLANES
    heads = range(hp)
    cols = [slice(hh * d, (hh + 1) * d) for hh in heads]
    slopes = [slopes_ref[h0 + hh] * (1.0 / scale) for hh in heads]

    km_ref[...] = jnp.zeros_like(km_ref)
    rel_i = _row_minus_col(bs, bs)
    for hh in heads:
        for n in range(nb):
            km_ref[hh, n:n + 1, :] = jnp.mean(
                k_ref[n * bs:(n + 1) * bs, cols[hh]].astype(F32), 0, keepdims=True)
        bias_ref[hh] = -slopes[hh] * rel_i.astype(F32)
    block = lax.broadcasted_iota(jnp.int32, (nbp, bs), 0)

    for own in range(nb):
        par = own % 2
        q = [q_ref[own * bs:(own + 1) * bs, cols[hh]] for hh in heads]
        for hh in heads if own > 0 else ():
            km = km_ref[hh]
            km_hi = km.astype(BF16)
            km_lo = (km - km_hi.astype(F32)).astype(BF16)
            gate = (_dot_nt(km_hi, q[hh]) + _dot_nt(km_lo, q[hh]))[0:nbp, :]
            rank = jnp.zeros((nbp, bs), jnp.int32)
            for m in range(own):
                gm = gate[m:m + 1, :]
                beats = (gm > gate) | ((gm == gate) & (block > m))
                rank = rank + beats.astype(jnp.int32)
            sel = (rank < n_sel) & (block < own)
            off = -slopes[hh] * ((own - block) * bs).astype(F32)
            mb_t = jnp.where(sel, off, NEG)
            mb = jnp.transpose(jnp.concatenate([mb_t, jnp.zeros((LANES - nbp, bs), F32)], axis=0))
            for n in range(own):
                mb_ref[par, hh, n] = jnp.broadcast_to(mb[:, n:n + 1], (bs, LANES))

        spans = _causal_spans(own)

        for c0, width, diagonal in spans:
            for hh in heads:
                qk = _dot_nt(q[hh], k_ref[c0 * bs:(c0 + width) * bs, cols[hh]])
                chunks = []
                for w in range(width):
                    sw = qk[:, w * bs:(w + 1) * bs] + bias_ref[hh]
                    if diagonal and w == width - 1:
                        sw = jnp.where(rel_i >= 0, sw, NEG)
                    else:
                        sw = sw + jnp.concatenate([mb_ref[par, hh, c0 + w]] * (bs // LANES), axis=1)
                    s_ref[par, hh, c0 + w] = sw
                    chunks.append(sw)
                mx = functools.reduce(jnp.maximum, chunks)
                mx_ref[par, hh] = mx if diagonal else jnp.maximum(mx_ref[par, hh], mx)

        for hh in heads:
            _row_max_bcast(mx_ref.at[par, hh])

        for c0, width, diagonal in spans:
            for hh in heads:
                ps = [jnp.exp2((s_ref[par, hh, c0 + w] - mx_ref[par, hh]) * (scale * LOG2E))
                      for w in range(width)]
                psum = functools.reduce(lambda a, b: a + b, ps)
                p = ps[0] if width == 1 else jnp.concatenate(ps, axis=1)
                pv = _dot(p.astype(BF16), v_ref[c0 * bs:(c0 + width) * bs, cols[hh]])
                if diagonal:
                    ls_ref[par, hh] = psum
                    acc_ref[par, hh] = pv
                else:
                    ls_ref[par, hh] += psum
                    acc_ref[par, hh] += pv

        for hh in heads:
            l = jnp.sum(ls_ref[par, hh], -1, keepdims=True)
            o_ref[own * bs:(own + 1) * bs, cols[hh]] = (acc_ref[par, hh] / l).astype(o_ref.dtype)


def _moba_attn(qkv, B, S, H, d, bs):
    T = B * S
    nb = S // bs
    hp = MOBA_HEADS_PER_STEP
    assert S % bs == 0 and nb <= LANES and H % hp == 0 and qkv.shape == (T, 3 * H * d)
    n_sel = min(MOBA_TOPK, nb - 1)
    slopes = jnp.asarray(_alibi_slopes(H), F32)
    return pl.pallas_call(
        functools.partial(_moba_kernel, nb=nb, n_sel=n_sel, d=d, bs=bs, scale=d ** -0.5),
        grid=(B, H // hp),
        in_specs=[
            pl.BlockSpec(memory_space=pltpu.SMEM),
            pl.BlockSpec((S, hp * d), lambda b, h: (b, h)),
            pl.BlockSpec((S, hp * d), lambda b, h: (b, H // hp + h)),
            pl.BlockSpec((S, hp * d), lambda b, h: (b, 2 * (H // hp) + h)),
        ],
        out_specs=pl.BlockSpec((S, hp * d), lambda b, h: (b, h)),
        out_shape=jax.ShapeDtypeStruct((T, H * d), BF16),
        scratch_shapes=[pltpu.VMEM((hp, LANES, d), F32), pltpu.VMEM((2, hp, nb, bs, LANES), F32),
                        pltpu.VMEM((2, hp, nb, bs, bs), F32), pltpu.VMEM((hp, bs, bs), F32),
                        pltpu.VMEM((2, hp, bs, bs), F32), pltpu.VMEM((2, hp, bs, bs), F32),
                        pltpu.VMEM((2, hp, bs, d), F32)],
        compiler_params=_params("parallel", "parallel"),
        name="moba_attn",
    )(slopes, qkv, qkv, qkv)


def _dsa_kernel(q_ref, k_ref, v_ref, qi_ref, tailq_ref, tailk_ref, o_ref,
                kib_ref, qis_ref, key_ref, thr_ref, s_ref, mx_ref, ls_ref, acc_ref, *, n_keep, tkc, scale):
    i = pl.program_id(1)
    tq = q_ref.shape[0]
    G, R, d = C_KV_HEADS, C_HEADS // C_KV_HEADS, C_HEAD_DIM
    qstart = i * tq
    nkc = (qstart + tq + tkc - 1) // tkc

    @pl.when(i == 0)
    def _():
        kib_ref[...] = tailk_ref[:, 0:IDX_DIM].astype(BF16)

    w_t = (jnp.transpose(tailq_ref[...])[IDX_DIM:IDX_DIM + IDX_HEADS, :]
           * (IDX_HEADS ** -0.5 * IDX_DIM ** -0.5))
    key_minus_query = _row_minus_col(tkc, tq)

    for hh in range(IDX_HEADS):
        qis_ref[hh * tq:(hh + 1) * tq, :] = qi_ref[:, hh * IDX_DIM:(hh + 1) * IDX_DIM]

    def score_chunk(c, carry):
        kc = kib_ref[pl.ds(pl.multiple_of(c * tkc, tkc), tkc), :]
        dots = _dot_nt(kc, qis_ref[...])
        acc = jnp.zeros((tkc, tq), F32)
        for hh in range(IDX_HEADS):
            acc = acc + w_t[hh:hh + 1, :] * jnp.maximum(dots[:, hh * tq:(hh + 1) * tq], 0.0)
        bits = lax.bitcast_convert_type(acc, jnp.int32)
        key = bits ^ ((bits >> 31) & jnp.int32(0x7FFFFFFF))
        causal = key_minus_query + (c * tkc - qstart) <= 0
        key_ref[c] = jnp.where(causal, key, jnp.int32(INT_MIN))
        return carry

    lax.fori_loop(0, nkc, score_chunk, 0)

    @pl.when(nkc % 2 == 1)
    def _():
        key_ref[nkc] = jnp.full((tkc, tq), INT_MIN, jnp.int32)

    def count(pred):
        def body(c2, part):
            for w in range(2):
                hit = pred(key_ref[2 * c2 + w]).astype(jnp.int32)
                part = part + jnp.sum(hit.reshape(tkc // SUBLANES, SUBLANES, tq), axis=0)
            return part
        part = lax.fori_loop(0, (nkc + 1) // 2, body, jnp.zeros((SUBLANES, tq), jnp.int32))
        return jnp.sum(part, axis=0, keepdims=True)

    thr_ref[...] = jnp.full_like(thr_ref, INT_MIN + 1)

    @pl.when(qstart + tq > n_keep)
    def _():
        zero = jnp.zeros((1, tq), jnp.int32)
        thr0 = jnp.where(count(lambda k: k >= zero) >= n_keep, jnp.int32(0), jnp.int32(INT_MIN))

        def bit_step(b, thr):
            cand = thr | jnp.left_shift(jnp.int32(1), 30 - b)
            return jnp.where(count(lambda k: k >= cand) >= n_keep, cand, thr)

        thr = jnp.maximum(lax.fori_loop(0, 31, bit_step, thr0), jnp.int32(INT_MIN + 1))
        thr_ref[...] = thr

        @pl.when(jnp.max(count(lambda k: k >= thr)) > n_keep)
        def _():
            room = (n_keep - count(lambda k: k > thr)).astype(F32)
            lower_tri = jnp.where(_row_minus_col(tkc, tkc) >= 0, 1.0, 0.0).astype(BF16)

            def drop_excess(c, seen):
                key = key_ref[c]
                tie = key == thr
                tie_f = jnp.where(tie, 1.0, 0.0)
                rank = seen + _dot(lower_tri, tie_f.astype(BF16))
                key_ref[c] = jnp.where(tie & (rank > room), jnp.int32(INT_MIN), key)
                return seen + jnp.sum(tie_f, axis=0, keepdims=True)

            lax.fori_loop(0, nkc, drop_excess, jnp.zeros((1, tq), F32))

    thr = thr_ref[...]
    slopes = _alibi_slopes(C_HEADS).reshape(G, R)
    rel = _row_minus_col(tq, tkc).astype(F32)

    def score_pass(c, first):
        rows = pl.ds(pl.multiple_of(c * tkc, tkc), tkc)
        mask = jnp.transpose(jnp.where(key_ref[c] >= thr, 0.0, NEG))
        dist = rel + (qstart - c * tkc).astype(F32)
        for g in range(G):
            qs = jnp.concatenate(
                [q_ref[:, (g * R + r) * d:(g * R + r + 1) * d] for r in range(R)], axis=0)
            bias = jnp.concatenate([mask - float(slopes[g, r] / scale) * dist for r in range(R)], axis=0)
            s = _dot_nt(qs, k_ref[rows, g * d:(g + 1) * d]) + bias
            s_ref[g, c] = s
            mx_ref[g] = s if first else jnp.maximum(mx_ref[g], s)

    def acc_pass(c, first):
        rows = pl.ds(pl.multiple_of(c * tkc, tkc), tkc)
        for g in range(G):
            p = jnp.exp2((s_ref[g, c] - mx_ref[g]) * (scale * LOG2E))
            pv = _dot(p.astype(BF16), v_ref[rows, g * d:(g + 1) * d])
            if first:
                ls_ref[g] = p
                acc_ref[g] = pv
            else:
                ls_ref[g] += p
                acc_ref[g] += pv

    def sweep(fn):
        fn(0, True)

        def body(c, carry):
            fn(c, False)
            return carry

        lax.fori_loop(1, nkc, body, 0)

    sweep(score_pass)
    for g in range(G):
        _row_max_bcast(mx_ref.at[g])
    sweep(acc_pass)

    for g in range(G):
        o = acc_ref[g] / jnp.sum(ls_ref[g], -1, keepdims=True)
        for r in range(R):
            o_ref[:, (g * R + r) * d:(g * R + r + 1) * d] = o[r * tq:(r + 1) * tq].astype(o_ref.dtype)


def _dsa_attn(main, tail, B, S, tq=128, tkc=256):
    T = B * S
    nq = S // tq
    G, H, d = C_KV_HEADS, C_HEADS, C_HEAD_DIM
    R = H // G
    c_q, c_kv, c_iq = H * d, G * d, IDX_HEADS * IDX_DIM
    assert S % tq == 0 and S % (2 * tkc) == 0 and tkc % tq == 0
    assert main.shape == (T, c_q + 2 * c_kv + c_iq) and tail.shape == (T, LANES)
    assert c_q % c_kv == 0 and (c_q + 2 * c_kv) % c_iq == 0
    n_keep = min(DSA_TOPK_MAX, S // 4)
    return pl.pallas_call(
        functools.partial(_dsa_kernel, n_keep=n_keep, tkc=tkc, scale=d ** -0.5),
        grid=(B, nq),
        in_specs=[
            pl.BlockSpec((tq, c_q), lambda b, i: (b * nq + i, 0)),
            pl.BlockSpec((S, c_kv), lambda b, i: (b, c_q // c_kv)),
            pl.BlockSpec((S, c_kv), lambda b, i: (b, c_q // c_kv + 1)),
            pl.BlockSpec((tq, c_iq), lambda b, i: (b * nq + i, (c_q + 2 * c_kv) // c_iq)),
            pl.BlockSpec((tq, LANES), lambda b, i: (b * nq + i, 0)),
            pl.BlockSpec((S, LANES), lambda b, i: (b, 0)),
        ],
        out_specs=pl.BlockSpec((tq, c_q), lambda b, i: (b * nq + i, 0)),
        out_shape=jax.ShapeDtypeStruct((T, c_q), BF16),
        scratch_shapes=[pltpu.VMEM((S, IDX_DIM), BF16), pltpu.VMEM((IDX_HEADS * tq, IDX_DIM), BF16),
                        pltpu.VMEM((S // tkc, tkc, tq), jnp.int32),
                        pltpu.VMEM((1, tq), jnp.int32),
                        pltpu.VMEM((G, S // tkc, R * tq, tkc), F32),
                        pltpu.VMEM((G, R * tq, tkc), F32), pltpu.VMEM((G, R * tq, tkc), F32),
                        pltpu.VMEM((G, R * tq, d), F32)],
        compiler_params=_params("parallel", "arbitrary"),
        name="dsa_attn",
    )(main, main, main, main, tail, tail)


def kernel(x, p, ffn1_w_in, ffn1_w_out, ffn2_w_in, ffn2_w_out, ln_g, ln_b, ple_w_gate, ple_w_proj,
           a_w_in, a_w_out, a_lam_q1, a_lam_k1, a_lam_q2, a_lam_k2, a_subln_g, b_w_in, b_w_out,
           c_w_in, c_w_out):
    B, S, D = x.shape
    depth = p.shape[0]
    T = B * S
    alpha = (2.0 * depth) ** 0.25
    bf = lambda w: w.astype(BF16)
    ple_g, ple_p = bf(ple_w_gate), bf(ple_w_proj)
    a_in, a_out, b_in, b_out, c_out = bf(a_w_in), bf(a_w_out), bf(b_w_in), bf(b_w_out), bf(c_w_out)
    n_main = C_HEADS * C_HEAD_DIM + 2 * C_KV_HEADS * C_HEAD_DIM + IDX_HEADS * IDX_DIM
    c_in = bf(c_w_in)
    c_tail = c_w_in[:, :, n_main:]
    c_tail = bf(jnp.pad(c_tail, ((0, 0), (0, 0), (0, LANES - c_tail.shape[2]))))
    lng = ln_g.reshape(-1, 1, D).astype(F32)
    lnb = ln_b.reshape(-1, 1, D).astype(F32)
    n_ln = ln_g.shape[1]
    p = p.reshape(depth, T, -1)
    x = x.reshape(T, D)
    for i in range(depth):
        m, j = i % N_MIXERS, i // N_MIXERS
        x = _ffn(x, ffn1_w_in, ffn1_w_out, i, lng, lnb, i * n_ln, alpha)
        if m == 0:
            d = A_HEAD_DIM
            H = D // (2 * d)
            lambda_init = 0.8 - 0.6 * math.exp(-0.3 * i)
            qkv = _proj(x, a_in, j, a_in.shape[2], BF16)
            lam_p = jnp.stack([a_lam_q1[j], a_lam_k1[j], a_lam_q2[j], a_lam_k2[j]]).astype(F32)
            o = _diff_attn(qkv, lam_p, a_subln_g[j].reshape(1, 2 * d).astype(F32), B, S, H, d, lambda_init)
            w_out = a_out
        elif m == 1:
            qkv = _proj(x, b_in, j, b_in.shape[2], BF16)
            o = _moba_attn(qkv, B, S, B_HEADS, B_HEAD_DIM, MOBA_BLOCK)
            w_out = b_out
        else:
            main = _proj(x, c_in, j, n_main, BF16)
            tail = _proj(x, c_tail, j, LANES, F32)
            o = _dsa_attn(main, tail, B, S)
            w_out = c_out
        x = _out_ln(x, o, w_out, j, lng, lnb, i * n_ln + 1, alpha)
        x = _ffn(x, ffn2_w_in, ffn2_w_out, i, lng, lnb, i * n_ln + 2, alpha)
        x = _ple(x, p, ple_g, ple_p, i, lng, lnb, i * n_ln + 3, alpha)
    return x.reshape(B, S, D)
```
